```python
import math
import jax, jax.numpy as jnp
from jax import lax
import numpy as np

D_MODEL = 1024
BATCH = 4
SEQ = 4096
DEPTH = 4
DEC_BATCH = 128
DEC_SEQ = 4
PAST_LEN = 8192
PAGE_SIZE = 128

HEAD_DIM = 64
R_HEADS = 4
R_WIDTH = R_HEADS * HEAD_DIM
R_DECAY_LORA = 32
R_AAA_LORA = 32
R_GATE_LORA = 64
R_PROJ = 3 * R_WIDTH + R_DECAY_LORA + R_AAA_LORA + R_GATE_LORA
R_SPLITS = [R_WIDTH, 2 * R_WIDTH, 3 * R_WIDTH, 3 * R_WIDTH + R_DECAY_LORA,
            3 * R_WIDTH + R_DECAY_LORA + R_AAA_LORA]
R_GN_EPS = 64e-5
S_Q_HEADS = 8
S_KV_HEADS = 2
S_GROUP = S_Q_HEADS // S_KV_HEADS
S_WIDTH = S_Q_HEADS * HEAD_DIM
S_KV_WIDTH = S_KV_HEADS * HEAD_DIM
S_PROJ = S_WIDTH + 2 * S_KV_WIDTH
WINDOW = 128
ATTN_BLOCK = 128
REL_BUCKETS = 32
REL_MAX_EXACT = 16
REL_MAX_DIST = 128
NEG_INF = -1e30
G_HEADS = 4
G_WIDTH = G_HEADS * HEAD_DIM
G_CONV = 4
G_CHUNK = 64
G_PROJ = 4 * G_WIDTH + 2 * G_HEADS
G_NORM_EPS = 1e-6

MIX_WIDTH = R_WIDTH + S_WIDTH + G_WIDTH
PROJ_DIM = R_PROJ + S_PROJ + G_PROJ
D_FF = 2816
LN_EPS = 1e-5
DN_ALPHA = (2 * DEPTH) ** 0.25
DN_BETA = (8 * DEPTH) ** -0.25

kernel_name = "hymba_rwkv7_swa_gdn_macaron_deepnorm_step"


def layer_norm(x, g, b):
    xf = x.astype(jnp.float32)
    mu = jnp.mean(xf, -1, keepdims=True)
    var = jnp.mean(jnp.square(xf - mu), -1, keepdims=True)
    return ((xf - mu) * lax.rsqrt(var + LN_EPS) * g + b).astype(x.dtype)


def swiglu(x, w_in, w_out):
    gate, up = jnp.split(x @ w_in, 2, axis=-1)
    return (jax.nn.silu(gate) * up) @ w_out


def l2norm(t, eps=1e-6):
    tf = t.astype(jnp.float32)
    return tf * lax.rsqrt(jnp.sum(tf * tf, -1, keepdims=True) + eps)


def rwkv7_scan(r, decay, k, v, kk, a, S0):
    def step(S, inp):
        r_t, w_t, k_t, v_t, kk_t, a_t = inp
        sa = jnp.einsum('bhvk,bhk->bhv', S, -kk_t)
        S = (S * w_t[:, :, None, :] + sa[..., None] * (kk_t * a_t)[:, :, None, :]
             + v_t[..., None] * k_t[:, :, None, :])
        return S, jnp.einsum('bhvk,bhk->bhv', S, r_t)
    xs = tuple(jnp.moveaxis(t, 1, 0) for t in (r, decay, k, v, kk, a))
    S, y = lax.scan(step, S0.astype(jnp.float32), xs)
    return jnp.moveaxis(y, 0, 1), S


def rwkv7_mixer(f, shift_prev, S0, mu, w0, w_up, a0, a_up, g_up, k_k, k_a, r_k, gn_g, gn_b):
    Bn, T, _ = f.shape
    prev = jnp.concatenate([shift_prev[:, None].astype(f.dtype), f[:, :-1]], axis=1)
    fs = f + (prev - f) * mu
    r, k, v, wd, ad, gd = jnp.split(fs, R_SPLITS, axis=-1)
    w = -jax.nn.softplus(-(w0 + jnp.tanh(wd) @ w_up)) - 0.5
    a = jax.nn.sigmoid(a0 + ad @ a_up)
    g = jax.nn.sigmoid(gd) @ g_up
    heads = lambda t: t.reshape(Bn, T, R_HEADS, HEAD_DIM).astype(jnp.float32)
    kk = l2norm(heads(k * k_k))
    k = k * (1 + (a - 1) * k_a)
    r_h, k_h, v_h, a_h = heads(r), heads(k), heads(v), heads(a)
    decay = jnp.exp(-jnp.exp(heads(w)))
    y, S = rwkv7_scan(r_h, decay, k_h, v_h, kk, a_h, S0)
    mean = jnp.mean(y, -1, keepdims=True)
    var = jnp.mean(jnp.square(y - mean), -1, keepdims=True)
    y = ((y - mean) * lax.rsqrt(var + R_GN_EPS)).reshape(Bn, T, R_WIDTH) * gn_g + gn_b
    bonus = jnp.sum(r_h * k_h * r_k, -1, keepdims=True) * v_h
    y = (y + bonus.reshape(Bn, T, R_WIDTH)) * g.astype(jnp.float32)
    return y.astype(f.dtype), S, f[:, -1]


def rel_bias(dist, table):
    n = jnp.maximum(dist, 0)
    nf = jnp.maximum(n, 1).astype(jnp.float32)
    large = REL_MAX_EXACT + (jnp.log(nf / REL_MAX_EXACT) / math.log(REL_MAX_DIST / REL_MAX_EXACT)
                             * (REL_BUCKETS - REL_MAX_EXACT)).astype(jnp.int32)
    bucket = jnp.where(n < REL_MAX_EXACT, n, jnp.minimum(large, REL_BUCKETS - 1))
    return table[bucket].astype(jnp.float32)


def head_bias(dist, table):
    b = jnp.moveaxis(rel_bias(dist, table), -1, 0)
    return b.reshape((S_KV_HEADS, S_GROUP) + dist.shape)


def sink_softmax(logits, mask, sinks):
    logits = jnp.where(mask, logits, NEG_INF)
    m = jnp.maximum(jnp.max(logits, -1, keepdims=True), sinks)
    p = jnp.exp(logits - m)
    return p / (jnp.sum(p, -1, keepdims=True) + jnp.exp(sinks - m))


def swa_prompt(q, k, v, sinks, table):
    Bn, T, _ = q.shape
    nb = T // ATTN_BLOCK
    qb = q.reshape(Bn, nb, ATTN_BLOCK, S_KV_HEADS, S_GROUP, HEAD_DIM)
    kb = k.reshape(Bn, nb, ATTN_BLOCK, S_KV_HEADS, HEAD_DIM)
    vb = v.reshape(Bn, nb, ATTN_BLOCK, S_KV_HEADS, HEAD_DIM)
    band = lambda t: jnp.concatenate(
        [jnp.pad(t, ((0, 0), (1, 0), (0, 0), (0, 0), (0, 0)))[:, :-1], t], axis=2)
    kband, vband = band(kb), band(vb)
    logits = jnp.einsum('bnqhgd,bnkhd->bnhgqk', qb, kband,
                        preferred_element_type=jnp.float32) * HEAD_DIM ** -0.5
    dist = ATTN_BLOCK + jnp.arange(ATTN_BLOCK)[:, None] - jnp.arange(2 * ATTN_BLOCK)[None, :]
    in_window = (dist >= 0) & (dist < WINDOW)
    has_prev = (jnp.arange(nb)[:, None, None] > 0) | (jnp.arange(2 * ATTN_BLOCK)[None, None, :] >= ATTN_BLOCK)
    mask = (in_window[None] & has_prev)[:, None, None]
    s = sinks.astype(jnp.float32).reshape(S_KV_HEADS, S_GROUP, 1, 1)
    probs = sink_softmax(logits + head_bias(dist, table), mask, s)
    out = jnp.einsum('bnhgqk,bnkhd->bnqhgd', probs.astype(v.dtype), vband)
    k_all = k.reshape(Bn, T, S_KV_HEADS, HEAD_DIM)
    v_all = v.reshape(Bn, T, S_KV_HEADS, HEAD_DIM)
    return out.reshape(Bn, T, S_WIDTH), k_all[:, -WINDOW:], v_all[:, -WINDOW:]


def swa_sample(q, k, v, cache_k, cache_v, sinks, table):
    Bn, L, _ = q.shape
    Wc = cache_k.shape[1]
    qh = q.reshape(Bn, L, S_KV_HEADS, S_GROUP, HEAD_DIM)
    keys = jnp.concatenate([cache_k.astype(k.dtype), k.reshape(Bn, L, S_KV_HEADS, HEAD_DIM)], axis=1)
    vals = jnp.concatenate([cache_v.astype(v.dtype), v.reshape(Bn, L, S_KV_HEADS, HEAD_DIM)], axis=1)
    logits = jnp.einsum('bqhgd,bkhd->bhgqk', qh, keys,
                        preferred_element_type=jnp.float32) * HEAD_DIM ** -0.5
    dist = Wc + jnp.arange(L)[:, None] - jnp.arange(Wc + L)[None, :]
    mask = (dist >= 0) & (dist < WINDOW)
    s = sinks.astype(jnp.float32).reshape(S_KV_HEADS, S_GROUP, 1, 1)
    probs = sink_softmax(logits + head_bias(dist, table), mask, s)
    out = jnp.einsum('bhgqk,bkhd->bqhgd', probs.astype(v.dtype), vals)
    return out.reshape(Bn, L, S_WIDTH), keys[:, -Wc:], vals[:, -Wc:]


def gdn_chunked(q, k, v, g, beta, S0):
    Bn, T = q.shape[:2]
    nc = -(-T // G_CHUNK)
    pad = nc * G_CHUNK - T
    def blocks(t):
        t = jnp.pad(t, [(0, 0), (0, pad)] + [(0, 0)] * (t.ndim - 2))
        t = t.reshape((Bn, nc, G_CHUNK) + t.shape[2:])
        return jnp.moveaxis(t, (1, 3), (0, 2))
    qc, kc, vc, gc, bc = (blocks(t) for t in (q, k, v, g, beta))
    G = jnp.cumsum(gc, axis=-1)
    diff = G[..., :, None] - G[..., None, :]
    idx = jnp.arange(G_CHUNK)
    dec_strict = jnp.exp(jnp.where(idx[:, None] > idx[None, :], diff, -jnp.inf))
    dec_causal = jnp.exp(jnp.where(idx[:, None] >= idx[None, :], diff, -jnp.inf))
    kb = kc * bc[..., None]
    A = jnp.eye(G_CHUNK, dtype=jnp.float32) + jnp.einsum('nbhid,nbhjd->nbhij', kb, kc) * dec_strict
    rhs = jnp.concatenate([vc * bc[..., None], kb * jnp.exp(G)[..., None]], axis=-1)
    sol = lax.linalg.triangular_solve(A, rhs, left_side=True, lower=True, unit_diagonal=True)
    u_base, w_state = sol[..., :HEAD_DIM], sol[..., HEAD_DIM:]
    qk = jnp.einsum('nbhid,nbhjd->nbhij', qc, kc) * dec_causal
    q_dec = qc * jnp.exp(G)[..., None]
    k_dec = kc * jnp.exp(G[..., -1:] - G)[..., None]
    chunk_decay = jnp.exp(G[..., -1])[..., None, None]
    def step(S, inp):
        u_b, w_s, qk_n, q_n, k_n, dec_n = inp
        u = u_b - jnp.einsum('bhck,bhkv->bhcv', w_s, S)
        o = jnp.einsum('bhck,bhkv->bhcv', q_n, S) + jnp.einsum('bhij,bhjv->bhiv', qk_n, u)
        S = dec_n * S + jnp.einsum('bhck,bhcv->bhkv', k_n, u)
        return S, o
    S, o = lax.scan(step, S0, (u_base, w_state, qk, q_dec, k_dec, chunk_decay))
    o = jnp.moveaxis(o, (0, 2), (1, 3)).reshape(Bn, nc * G_CHUNK, G_HEADS, HEAD_DIM)[:, :T]
    return o, S


def gdn_mixer(f, conv_buf, S0, conv_w, a_log, dt_bias, norm_g):
    Bn, T, _ = f.shape
    qkv, z, b_logit, a_logit = jnp.split(f, [3 * G_WIDTH, 4 * G_WIDTH, 4 * G_WIDTH + G_HEADS], axis=-1)
    full = jnp.concatenate([conv_buf.astype(f.dtype), qkv], axis=1)
    conv = full[:, 0:T] * conv_w[0]
    for j in range(1, G_CONV):
        conv = conv + full[:, j:j + T] * conv_w[j]
    q, k, v = jnp.split(jax.nn.silu(conv), 3, axis=-1)
    heads = lambda t: t.reshape(Bn, T, G_HEADS, HEAD_DIM)
    q = l2norm(heads(q)) * HEAD_DIM ** -0.5
    k = l2norm(heads(k))
    v = heads(v).astype(jnp.float32)
    beta = jax.nn.sigmoid(b_logit.astype(jnp.float32))
    g = -jnp.exp(a_log.astype(jnp.float32)) * jax.nn.softplus(a_logit.astype(jnp.float32) + dt_bias)
    o, S = gdn_chunked(q, k, v, g, beta, S0.astype(jnp.float32))
    o = o * lax.rsqrt(jnp.mean(o * o, -1, keepdims=True) + G_NORM_EPS) * norm_g
    o = o * jax.nn.silu(heads(z).astype(jnp.float32))
    return o.reshape(Bn, T, G_WIDTH).astype(f.dtype), S, full[:, -(G_CONV - 1):]


def run_trunk(x, rwkv_S, rwkv_shift, swa_k, swa_v, gdn_S, gdn_conv, params, prompt):
    (ffn1_w_in, ffn1_w_out, ln1_g, ln1_b, w_in, rwkv_mu, rwkv_w0, rwkv_w_up, rwkv_a0, rwkv_a_up,
     rwkv_g_up, rwkv_k_k, rwkv_k_a, rwkv_r_k, rwkv_gn_g, rwkv_gn_b, swa_sinks, rel_table,
     gdn_conv_w, gdn_a_log, gdn_dt_bias, gdn_norm_g, w_out, ln2_g, ln2_b,
     ffn2_w_in, ffn2_w_out, ln3_g, ln3_b) = params
    new = [[] for _ in range(6)]
    for l in range(DEPTH):
        x = layer_norm(DN_ALPHA * x + 0.5 * swiglu(x, ffn1_w_in[l], ffn1_w_out[l]), ln1_g[l], ln1_b[l])
        proj = x @ w_in[l]
        f_r, f_s, f_g = jnp.split(proj, [R_PROJ, R_PROJ + S_PROJ], axis=-1)
        y_r, s_r, sh_r = rwkv7_mixer(f_r, rwkv_shift[l], rwkv_S[l], rwkv_mu[l], rwkv_w0[l], rwkv_w_up[l],
                                     rwkv_a0[l], rwkv_a_up[l], rwkv_g_up[l], rwkv_k_k[l], rwkv_k_a[l],
                                     rwkv_r_k[l], rwkv_gn_g[l], rwkv_gn_b[l])
        q, k, v = jnp.split(f_s, [S_WIDTH, S_WIDTH + S_KV_WIDTH], axis=-1)
        if prompt:
            y_s, kc, vc = swa_prompt(q, k, v, swa_sinks[l], rel_table)
        else:
            y_s, kc, vc = swa_sample(q, k, v, swa_k[l], swa_v[l], swa_sinks[l], rel_table)
        y_g, s_g, cb = gdn_mixer(f_g, gdn_conv[l], gdn_S[l], gdn_conv_w[l], gdn_a_log[l],
                                 gdn_dt_bias[l], gdn_norm_g[l])
        mix = jnp.concatenate([y_r, y_s, y_g], axis=-1) @ w_out[l]
        x = layer_norm(DN_ALPHA * x + mix, ln2_g[l], ln2_b[l])
        x = layer_norm(DN_ALPHA * x + 0.5 * swiglu(x, ffn2_w_in[l], ffn2_w_out[l]), ln3_g[l], ln3_b[l])
        for lst, t in zip(new, (s_r, sh_r, kc, vc, s_g, cb)):
            lst.append(t.astype(x.dtype))
    return x, [jnp.stack(lst, axis=0) for lst in new]


def setup_inputs(seed: int = 0) -> dict:
    key = jax.random.key(seed)
    ks = iter(jax.random.split(key, 64))
    nrm = lambda shape, scale: jax.random.normal(next(ks), shape, jnp.float32) * scale
    uni = lambda shape, lo, hi: jax.random.uniform(next(ks), shape, jnp.float32, lo, hi)
    w_buf = min(WINDOW, PAST_LEN)
    dt = jnp.exp(uni((DEPTH, G_HEADS), math.log(1e-3), math.log(1e-1)))
    return {
        "x_prompt": nrm((BATCH, SEQ, D_MODEL), 1.0),
        "x_sample": nrm((DEC_BATCH, DEC_SEQ, D_MODEL), 1.0),
        "state_rwkv": nrm((DEPTH, DEC_BATCH, R_HEADS, HEAD_DIM, HEAD_DIM), 0.3),
        "state_rwkv_shift": nrm((DEPTH, DEC_BATCH, R_PROJ), 1.0),
        "cache_swa_k": nrm((DEPTH, DEC_BATCH, w_buf, S_KV_HEADS, HEAD_DIM), 1.0),
        "cache_swa_v": nrm((DEPTH, DEC_BATCH, w_buf, S_KV_HEADS, HEAD_DIM), 1.0),
        "state_gdn": nrm((DEPTH, DEC_BATCH, G_HEADS, HEAD_DIM, HEAD_DIM), 0.3),
        "state_gdn_conv": nrm((DEPTH, DEC_BATCH, G_CONV - 1, 3 * G_WIDTH), 1.0),
        "ffn1_w_in": nrm((DEPTH, D_MODEL, 2 * D_FF), D_MODEL ** -0.5),
        "ffn1_w_out": nrm((DEPTH, D_FF, D_MODEL), DN_BETA * D_FF ** -0.5),
        "ln1_g": 1.0 + nrm((DEPTH, D_MODEL), 0.02),
        "ln1_b": nrm((DEPTH, D_MODEL), 0.02),
        "w_in": nrm((DEPTH, D_MODEL, PROJ_DIM), D_MODEL ** -0.5),
        "rwkv_mu": uni((DEPTH, R_PROJ), 0.0, 1.0),
        "rwkv_w0": uni((DEPTH, R_WIDTH), -6.0, -1.0),
        "rwkv_w_up": nrm((DEPTH, R_DECAY_LORA, R_WIDTH), 0.5 * R_DECAY_LORA ** -0.5),
        "rwkv_a0": nrm((DEPTH, R_WIDTH), 0.1),
        "rwkv_a_up": nrm((DEPTH, R_AAA_LORA, R_WIDTH), 0.5 * R_AAA_LORA ** -0.5),
        "rwkv_g_up": nrm((DEPTH, R_GATE_LORA, R_WIDTH), R_GATE_LORA ** -0.5),
        "rwkv_k_k": uni((DEPTH, R_WIDTH), 0.7, 1.0),
        "rwkv_k_a": uni((DEPTH, R_WIDTH), 0.8, 1.2),
        "rwkv_r_k": nrm((DEPTH, R_HEADS, HEAD_DIM), 0.1),
        "rwkv_gn_g": 1.0 + nrm((DEPTH, R_WIDTH), 0.02),
        "rwkv_gn_b": nrm((DEPTH, R_WIDTH), 0.02),
        "swa_sinks": nrm((DEPTH, S_Q_HEADS), 1.0),
        "rel_table": nrm((REL_BUCKETS, S_Q_HEADS), 0.5),
        "gdn_conv_w": nrm((DEPTH, G_CONV, 3 * G_WIDTH), G_CONV ** -0.5),
        "gdn_a_log": jnp.log(uni((DEPTH, G_HEADS), 1.0, 16.0)),
        "gdn_dt_bias": dt + jnp.log(-jnp.expm1(-dt)),
        "gdn_norm_g": 1.0 + nrm((DEPTH, HEAD_DIM), 0.02),
        "w_out": nrm((DEPTH, MIX_WIDTH, D_MODEL), DN_BETA * MIX_WIDTH ** -0.5),
        "ln2_g": 1.0 + nrm((DEPTH, D_MODEL), 0.02),
        "ln2_b": nrm((DEPTH, D_MODEL), 0.02),
        "ffn2_w_in": nrm((DEPTH, D_MODEL, 2 * D_FF), D_MODEL ** -0.5),
        "ffn2_w_out": nrm((DEPTH, D_FF, D_MODEL), DN_BETA * D_FF ** -0.5),
        "ln3_g": 1.0 + nrm((DEPTH, D_MODEL), 0.02),
        "ln3_b": nrm((DEPTH, D_MODEL), 0.02),
    }


def reference(x_prompt, x_sample, state_rwkv, state_rwkv_shift, cache_swa_k, cache_swa_v, state_gdn,
              state_gdn_conv, ffn1_w_in, ffn1_w_out, ln1_g, ln1_b, w_in, rwkv_mu, rwkv_w0, rwkv_w_up,
              rwkv_a0, rwkv_a_up, rwkv_g_up, rwkv_k_k, rwkv_k_a, rwkv_r_k, rwkv_gn_g, rwkv_gn_b,
              swa_sinks, rel_table, gdn_conv_w, gdn_a_log, gdn_dt_bias, gdn_norm_g, w_out,
              ln2_g, ln2_b, ffn2_w_in, ffn2_w_out, ln3_g, ln3_b):
    params = (ffn1_w_in, ffn1_w_out, ln1_g, ln1_b, w_in, rwkv_mu, rwkv_w0, rwkv_w_up, rwkv_a0, rwkv_a_up,
              rwkv_g_up, rwkv_k_k, rwkv_k_a, rwkv_r_k, rwkv_gn_g, rwkv_gn_b, swa_sinks, rel_table,
              gdn_conv_w, gdn_a_log, gdn_dt_bias, gdn_norm_g, w_out, ln2_g, ln2_b,
              ffn2_w_in, ffn2_w_out, ln3_g, ln3_b)
    bp = x_prompt.shape[0]
    dtp = x_prompt.dtype
    y_prompt, p_states = run_trunk(
        x_prompt,
        jnp.zeros((DEPTH, bp, R_HEADS, HEAD_DIM, HEAD_DIM), dtp),
        jnp.zeros((DEPTH, bp, R_PROJ), dtp),
        None, None,
        jnp.zeros((DEPTH, bp, G_HEADS, HEAD_DIM, HEAD_DIM), dtp),
        jnp.zeros((DEPTH, bp, G_CONV - 1, 3 * G_WIDTH), dtp),
        params, True)
    y_sample, s_states = run_trunk(x_sample, state_rwkv, state_rwkv_shift, cache_swa_k, cache_swa_v,
                                   state_gdn, state_gdn_conv, params, False)
    p_rwkv, p_rwkv_shift, p_swa_k, p_swa_v, p_gdn, p_gdn_conv = p_states
    s_rwkv, s_rwkv_shift, s_swa_k, s_swa_v, s_gdn, s_gdn_conv = s_states
    return (y_prompt, y_sample, p_rwkv, p_rwkv_shift, p_swa_k, p_swa_v, p_gdn, p_gdn_conv,
            s_rwkv, s_rwkv_shift, s_swa_k, s_swa_v, s_gdn, s_gdn_conv)
```

```python
import functools
import math

import numpy as np
import jax
import jax.numpy as jnp
from jax import lax
from jax.experimental import pallas as pl
from jax.experimental.pallas import tpu as pltpu

F32 = jnp.float32
BF16 = jnp.bfloat16

HEAD_DIM = 64
N_HEADS = 4
R_WIDTH = N_HEADS * HEAD_DIM
R_PROJ = 896
S_Q_HEADS = 8
S_KV_HEADS = 2
S_GROUP = S_Q_HEADS // S_KV_HEADS
S_WIDTH = S_Q_HEADS * HEAD_DIM
S_KV_WIDTH = S_KV_HEADS * HEAD_DIM
S_PROJ = S_WIDTH + 2 * S_KV_WIDTH
G_WIDTH = N_HEADS * HEAD_DIM
G_CONV = 4
G_QKV = 3 * G_WIDTH
G_PROJ_PAD = 4 * G_WIDTH + 128
WINDOW = 128
ATTN_BLOCK = 128
REL_BUCKETS = 32
REL_MAX_EXACT = 16
REL_MAX_DIST = 128
NEG_INF = -1e30
R_GN_EPS = 64e-5
G_NORM_EPS = 1e-6
LN_EPS = 1e-5
L2_EPS = 1e-6
PROJ_PAD = R_PROJ + S_PROJ + G_PROJ_PAD

LANES = 128
SUBLANES = 8
VMEM_LIMIT_BYTES = 56 * 1024 * 1024

NN = ((1,), (0,))
NT = ((1,), (1,))
TN = ((0,), (0,))


def _dg(a, b, dims):
    return lax.dot_general(a, b, (dims, ((), ())), preferred_element_type=F32)


def _split2(a):
    hi = a.astype(BF16)
    lo = (a - hi.astype(F32)).astype(BF16)
    return hi, lo


def _mm1(a, b, dims=NN):
    return _dg(a.astype(BF16), b.astype(BF16), dims)


def _mm3(a, b, dims=NN):
    ah, al = _split2(a)
    bh, bl = _split2(b)
    return _dg(ah, bh, dims) + (_dg(ah, bl, dims) + _dg(al, bh, dims))


def _mm_exact_lhs(a_bf16, b, dims=NN):
    b1 = b.astype(BF16)
    r1 = b - b1.astype(F32)
    b2 = r1.astype(BF16)
    b3 = (r1 - b2.astype(F32)).astype(BF16)
    return _dg(a_bf16, b1, dims) + (_dg(a_bf16, b2, dims) + _dg(a_bf16, b3, dims))


def _sigmoid(x):
    return 1.0 / (1.0 + jnp.exp(-x))


def _silu(x):
    return x * _sigmoid(x)


def _softplus(x):
    return jnp.maximum(x, 0.0) + jnp.log(1.0 + jnp.exp(-jnp.abs(x)))


def _layer_norm(z, g, b):
    mu = jnp.mean(z, axis=-1, keepdims=True)
    zc = z - mu
    var = jnp.mean(zc * zc, axis=-1, keepdims=True)
    return zc * lax.rsqrt(var + LN_EPS) * g + b


def _tri_inverse(m, c):
    row = lax.broadcasted_iota(jnp.int32, (c, c), 0)
    col = lax.broadcasted_iota(jnp.int32, (c, c), 1)
    t = jnp.where(row == col, 1.0, 0.0).astype(F32) + m
    mp = m
    span = 2
    while span < c:
        mp = _mm3(mp, mp)
        t = t + _mm3(t, mp)
        span *= 2
    return t


def _full_spec(shape):
    nd = len(shape)
    return pl.BlockSpec(shape, lambda *_: (0,) * nd)


def _resident_spec(shape, index_map):
    return pl.BlockSpec(shape, index_map, pipeline_mode=pl.Buffered(1))


def _swiglu(xb, wi_ref, wo_ref, d_ff, n_chunks):
    cw = d_ff // n_chunks
    acc = None
    for j in range(n_chunks):
        gate = _dg(xb, wi_ref[:, j * cw:(j + 1) * cw], NN)
        up = _dg(xb, wi_ref[:, d_ff + j * cw:d_ff + (j + 1) * cw], NN)
        act = (_silu(gate) * up).astype(BF16)
        part = _dg(act, wo_ref[j * cw:(j + 1) * cw, :], NN)
        acc = part if acc is None else acc + part
    return acc


def _ffn_proj_kernel(x_ref, wi_ref, wo_ref, g_ref, b_ref, win_ref,
                     x1_ref, pr_ref, ps_ref, pg_ref, *, alpha, d_ff, n_chunks):
    x = x_ref[...]
    y = _swiglu(x.astype(BF16), wi_ref, wo_ref, d_ff, n_chunks)
    x1 = _layer_norm(alpha * x + 0.5 * y, g_ref[...], b_ref[...])
    x1_ref[...] = x1
    p = _dg(x1.astype(BF16), win_ref[...], NN)
    pr_ref[...] = p[:, :R_PROJ]
    ps_ref[...] = p[:, R_PROJ:R_PROJ + S_PROJ]
    pg_ref[...] = p[:, R_PROJ + S_PROJ:]


def _ffn_proj(x, wi, wo, g, b, win, *, alpha, tm, n_chunks):
    m, d = x.shape
    d_ff = wo.shape[0]
    kern = functools.partial(_ffn_proj_kernel, alpha=alpha, d_ff=d_ff, n_chunks=n_chunks)
    row = lambda w: pl.BlockSpec((tm, w), lambda i: (i, 0))
    const = lambda i: (0, 0)
    return pl.pallas_call(
        kern,
        grid=(m // tm,),
        in_specs=[row(d),
                  _resident_spec(wi.shape, const), _resident_spec(wo.shape, const),
                  _resident_spec(g.shape, const), _resident_spec(b.shape, const),
                  _resident_spec(win.shape, const)],
        out_specs=[row(d), row(R_PROJ), row(S_PROJ), row(G_PROJ_PAD)],
        out_shape=[jax.ShapeDtypeStruct((m, d), F32), jax.ShapeDtypeStruct((m, R_PROJ), F32),
                   jax.ShapeDtypeStruct((m, S_PROJ), F32), jax.ShapeDtypeStruct((m, G_PROJ_PAD), F32)],
        compiler_params=pltpu.CompilerParams(dimension_semantics=("arbitrary",),
                                             vmem_limit_bytes=VMEM_LIMIT_BYTES),
        name="ffn_proj",
    )(x, wi, wo, g, b, win)


def _out_ffn_kernel(x_ref, yr_ref, ys_ref, yg_ref, wout_ref, g2_ref, b2_ref, wi_ref, wo_ref, g3_ref, b3_ref,
                    o_ref, *, alpha, d_ff, n_chunks):
    x = x_ref[...]
    mix = (_dg(yr_ref[...].astype(BF16), wout_ref[0:R_WIDTH, :], NN)
           + _dg(ys_ref[...].astype(BF16), wout_ref[R_WIDTH:R_WIDTH + S_WIDTH, :], NN)
           + _dg(yg_ref[...].astype(BF16), wout_ref[R_WIDTH + S_WIDTH:, :], NN))
    x2 = _layer_norm(alpha * x + mix, g2_ref[...], b2_ref[...])
    y = _swiglu(x2.astype(BF16), wi_ref, wo_ref, d_ff, n_chunks)
    o_ref[...] = _layer_norm(alpha * x2 + 0.5 * y, g3_ref[...], b3_ref[...])


def _out_ffn(x, yr, ys, yg, wout, g2, b2, wi, wo, g3, b3, *, alpha, tm, n_chunks):
    m, d = x.shape
    d_ff = wo.shape[0]
    kern = functools.partial(_out_ffn_kernel, alpha=alpha, d_ff=d_ff, n_chunks=n_chunks)
    row = lambda w: pl.BlockSpec((tm, w), lambda i: (i, 0))
    const = lambda i: (0, 0)
    res = lambda a: _resident_spec(a.shape, const)
    return pl.pallas_call(
        kern,
        grid=(m // tm,),
        in_specs=[row(d), row(R_WIDTH), row(S_WIDTH), row(G_WIDTH),
                  res(wout), res(g2), res(b2), res(wi), res(wo), res(g3), res(b3)],
        out_specs=row(d),
        out_shape=jax.ShapeDtypeStruct((m, d), F32),
        compiler_params=pltpu.CompilerParams(dimension_semantics=("arbitrary",),
                                             vmem_limit_bytes=VMEM_LIMIT_BYTES),
        name="out_ffn",
    )(x, yr, ys, yg, wout, g2, b2, wi, wo, g3, b3)


def _rwkv_kernel(f_ref, shift_ref, s0_ref, mu_ref, w0_ref, wup_ref, a0_ref, aup_ref, gup_ref,
                 kk_ref, ka_ref, rk_ref, gng_ref, gnb_ref,
                 y_ref, sout_ref, shout_ref, s_scr, prev_scr, *, bb, c, tv, nc):
    ci = pl.program_id(1)

    @pl.when(ci == 0)
    def _():
        s_scr[...] = s0_ref[...]
        prev_scr[...] = shift_ref[...]

    row = lax.broadcasted_iota(jnp.int32, (c, 1), 0)
    ri = lax.broadcasted_iota(jnp.int32, (c, 2 * c), 0)
    cj = lax.broadcasted_iota(jnp.int32, (c, 2 * c), 1)
    cj = jnp.where(cj >= c, cj - c, cj)
    strict = ri > cj
    incl = ri >= cj
    tri = (lax.broadcasted_iota(jnp.int32, (c, c), 0) >= lax.broadcasted_iota(jnp.int32, (c, c), 1))
    tri = jnp.where(tri, 1.0, 0.0).astype(BF16)
    valid = row < tv

    def one_seq(i, carry):
        f = f_ref[i]
        prev = jnp.where(row == 0, prev_scr[i], pltpu.roll(f, 1, 0))
        prev_scr[i] = f[c - 1:c, :]
        fs = f + (prev - f) * mu_ref[...]
        r = fs[:, 0:R_WIDTH]
        k = fs[:, R_WIDTH:2 * R_WIDTH]
        v = fs[:, 2 * R_WIDTH:3 * R_WIDTH]
        wd = fs[:, 768:800]
        ad = fs[:, 800:832]
        gd = fs[:, 832:896]
        w = -_softplus(-(w0_ref[...] + _mm3(jnp.tanh(wd), wup_ref[...]))) - 0.5
        lr = _sigmoid(a0_ref[...] + _mm3(ad, aup_ref[...]))
        gate = _mm3(_sigmoid(gd), gup_ref[...])
        lw = -jnp.exp(w)
        kkx = k * kk_ref[...]
        k2 = k * (1.0 + (lr - 1.0) * ka_ref[...])
        if tv < c:
            lw = jnp.where(valid, lw, 0.0)
            kkx = jnp.where(valid, kkx, 0.0)
            k2 = jnp.where(valid, k2, 0.0)
        cum = _mm_exact_lhs(tri, lw)
        p_in = jnp.exp(cum)
        p_prev = jnp.exp(cum - lw)
        p_inv = jnp.exp(-cum)
        for h in range(N_HEADS):
            sl = slice(h * HEAD_DIM, (h + 1) * HEAD_DIM)
            kx = kkx[:, sl]
            kkh = kx * lax.rsqrt(jnp.sum(kx * kx, axis=-1, keepdims=True) + L2_EPS)
            at = -kkh * p_prev[:, sl]
            bt = kkh * lr[:, sl] * p_inv[:, sl]
            kt = k2[:, sl] * p_inv[:, sl]
            qt = r[:, sl] * p_in[:, sl]
            vh = v[:, sl]
            y2 = jnp.concatenate([bt, kt], axis=0)
            xa = _mm3(at, y2, NT)
            xq = _mm3(qt, y2, NT)
            xa = jnp.where(strict, xa, 0.0)
            xq = jnp.where(incl, xq, 0.0)
            t = _tri_inverse(xa[:, :c], c)
            s = s_scr[i, h]
            u = _mm3(t, _mm3(at, s, NT) + _mm3(xa[:, c:], vh))
            uv = jnp.concatenate([u, vh], axis=0)
            o = _mm3(qt, s, NT) + _mm3(xq, uv)
            s_new = (s + _mm3(uv, y2, TN)) * p_in[c - 1:c, sl]
            s_scr[i, h] = s_new
            mean = jnp.mean(o, axis=-1, keepdims=True)
            oc = o - mean
            var = jnp.mean(oc * oc, axis=-1, keepdims=True)
            on = oc * lax.rsqrt(var + R_GN_EPS) * gng_ref[:, sl] + gnb_ref[:, sl]
            bonus = jnp.sum(r[:, sl] * k2[:, sl] * rk_ref[:, sl], axis=-1, keepdims=True) * vh
            y_ref[i, :, sl] = (on + bonus) * gate[:, sl]

        @pl.when(ci == nc - 1)
        def _():
            shout_ref[i] = f[tv - 1:tv, :]
        return carry

    lax.fori_loop(0, bb, one_seq, 0)

    @pl.when(ci == nc - 1)
    def _():
        sout_ref[...] = s_scr[...]


def _rwkv_mixer(f, shift_prev, s0, params, *, bb, c, tv):
    b, t, _ = f.shape
    nc = t // c
    kern = functools.partial(_rwkv_kernel, bb=bb, c=c, tv=tv, nc=nc)
    shift3 = shift_prev.reshape(b, 1, R_PROJ)
    y, s_out, sh_out = pl.pallas_call(
        kern,
        grid=(b // bb, nc),
        in_specs=[pl.BlockSpec((bb, c, R_PROJ), lambda i, j: (i, j, 0)),
                  pl.BlockSpec((bb, 1, R_PROJ), lambda i, j: (i, 0, 0)),
                  pl.BlockSpec((bb, N_HEADS, HEAD_DIM, HEAD_DIM), lambda i, j: (i, 0, 0, 0))]
                 + [_full_spec(p.shape) for p in params],
        out_specs=[pl.BlockSpec((bb, c, R_WIDTH), lambda i, j: (i, j, 0)),
                   pl.BlockSpec((bb, N_HEADS, HEAD_DIM, HEAD_DIM), lambda i, j: (i, 0, 0, 0)),
                   pl.BlockSpec((bb, 1, R_PROJ), lambda i, j: (i, 0, 0))],
        out_shape=[jax.ShapeDtypeStruct((b, t, R_WIDTH), F32),
                   jax.ShapeDtypeStruct((b, N_HEADS, HEAD_DIM, HEAD_DIM), F32),
                   jax.ShapeDtypeStruct((b, 1, R_PROJ), F32)],
        scratch_shapes=[pltpu.VMEM((bb, N_HEADS, HEAD_DIM, HEAD_DIM), F32),
                        pltpu.VMEM((bb, 1, R_PROJ), F32)],
        compiler_params=pltpu.CompilerParams(dimension_semantics=("arbitrary", "arbitrary"),
                                             vmem_limit_bytes=VMEM_LIMIT_BYTES),
        name="rwkv7",
    )(f, shift3, s0, *params)
    return y, s_out, sh_out.reshape(b, R_PROJ)


def _gdn_kernel(x_ref, z_ref, gb_ref, cb_ref, s0_ref, cw_ref, alog_ref, dtb_ref, ng_ref,
                y_ref, sout_ref, cbout_ref, s_scr, tail_scr, *, bb, c, tv, nc):
    ci = pl.program_id(1)

    @pl.when(ci == 0)
    def _():
        s_scr[...] = s0_ref[...]
        tail_scr[...] = jnp.zeros_like(tail_scr)
        tail_scr[:, SUBLANES - (G_CONV - 1):SUBLANES, :] = cb_ref[...]

    row = lax.broadcasted_iota(jnp.int32, (c, 1), 0)
    ri = lax.broadcasted_iota(jnp.int32, (c, c), 0)
    cj = lax.broadcasted_iota(jnp.int32, (c, c), 1)
    strict = ri > cj
    incl = ri >= cj
    tri = jnp.where(incl, 1.0, 0.0).astype(BF16)
    valid = row < tv

    def one_seq(i, carry):
        x = x_ref[i]
        xe = jnp.concatenate([tail_scr[i], x], axis=0)
        conv = x * cw_ref[G_CONV - 1:G_CONV, :]
        for s in range(1, G_CONV):
            conv = conv + pltpu.roll(xe, s, 0)[SUBLANES:, :] * cw_ref[G_CONV - 1 - s:G_CONV - s, :]
        tail_scr[i] = x[c - SUBLANES:, :]
        act = _silu(conv)
        gbv = gb_ref[i]
        beta_all = _sigmoid(gbv)
        g_all = -jnp.exp(alog_ref[...]) * _softplus(gbv + dtb_ref[...])
        if tv < c:
            beta_all = jnp.where(valid, beta_all, 0.0)
            g_all = jnp.where(valid, g_all, 0.0)
        gcum = _mm_exact_lhs(tri, g_all)
        gpad = jnp.concatenate([gcum, jnp.zeros((LANES - c, LANES), F32)], axis=0) if c < LANES else gcum
        gcum_t = gpad.T
        zv = z_ref[i]
        for h in range(N_HEADS):
            sl = slice(h * HEAD_DIM, (h + 1) * HEAD_DIM)
            qx = act[:, sl]
            kx = act[:, G_WIDTH + h * HEAD_DIM:G_WIDTH + (h + 1) * HEAD_DIM]
            vh = act[:, 2 * G_WIDTH + h * HEAD_DIM:2 * G_WIDTH + (h + 1) * HEAD_DIM]
            qh = qx * lax.rsqrt(jnp.sum(qx * qx, axis=-1, keepdims=True) + L2_EPS) * (HEAD_DIM ** -0.5)
            kh = kx * lax.rsqrt(jnp.sum(kx * kx, axis=-1, keepdims=True) + L2_EPS)
            if tv < c:
                kh = jnp.where(valid, kh, 0.0)
            beta = beta_all[:, h:h + 1]
            gcol = gcum[:, N_HEADS + h:N_HEADS + h + 1]
            grow = gcum_t[N_HEADS + h:N_HEADS + h + 1, 0:c]
            glast = gcum[c - 1:c, N_HEADS + h:N_HEADS + h + 1]
            diff = gcol - grow
            dec_strict = jnp.exp(jnp.where(strict, diff, NEG_INF))
            dec_causal = jnp.exp(jnp.where(incl, diff, NEG_INF))
            eg = jnp.exp(gcol)
            kb = kh * beta
            gram = _mm3(jnp.concatenate([kb, qh], axis=0), kh, NT)
            t = _tri_inverse(-(gram[:c] * dec_strict), c)
            sol = _mm3(t, jnp.concatenate([vh * beta, kb * eg], axis=1))
            u_base = sol[:, :HEAD_DIM]
            w_state = sol[:, HEAD_DIM:]
            qk = gram[c:] * dec_causal
            s = s_scr[i, h]
            ws = _mm3(jnp.concatenate([w_state, qh * eg], axis=0), s)
            u = u_base - ws[:c]
            o = ws[c:] + _mm3(qk, u)
            s_scr[i, h] = jnp.exp(glast) * s + _mm3(kh * jnp.exp(glast - gcol), u, TN)
            o = o * lax.rsqrt(jnp.mean(o * o, axis=-1, keepdims=True) + G_NORM_EPS) * ng_ref[...]
            y_ref[i, :, sl] = o * _silu(zv[:, sl])

        @pl.when(ci == nc - 1)
        def _():
            shift = (c - (tv - (G_CONV - 1))) % c
            xs = pltpu.roll(x, shift, 0) if shift else x
            cbout_ref[i] = xs[0:G_CONV - 1, :]
        return carry

    lax.fori_loop(0, bb, one_seq, 0)

    @pl.when(ci == nc - 1)
    def _():
        sout_ref[...] = s_scr[...]


def _gdn_mixer(pg, conv_buf, s0, params, *, bb, c, tv):
    b, t, _ = pg.shape
    nc = t // c
    kern = functools.partial(_gdn_kernel, bb=bb, c=c, tv=tv, nc=nc)
    state_spec = pl.BlockSpec((bb, N_HEADS, HEAD_DIM, HEAD_DIM), lambda i, j: (i, 0, 0, 0))
    cb_spec = pl.BlockSpec((bb, G_CONV - 1, G_QKV), lambda i, j: (i, 0, 0))
    return pl.pallas_call(
        kern,
        grid=(b // bb, nc),
        in_specs=[pl.BlockSpec((bb, c, G_QKV), lambda i, j: (i, j, 0)),
                  pl.BlockSpec((bb, c, G_WIDTH), lambda i, j: (i, j, G_QKV // G_WIDTH)),
                  pl.BlockSpec((bb, c, LANES), lambda i, j: (i, j, (G_QKV + G_WIDTH) // LANES)),
                  cb_spec, state_spec] + [_full_spec(p.shape) for p in params],
        out_specs=[pl.BlockSpec((bb, c, G_WIDTH), lambda i, j: (i, j, 0)), state_spec, cb_spec],
        out_shape=[jax.ShapeDtypeStruct((b, t, G_WIDTH), F32),
                   jax.ShapeDtypeStruct((b, N_HEADS, HEAD_DIM, HEAD_DIM), F32),
                   jax.ShapeDtypeStruct((b, G_CONV - 1, G_QKV), F32)],
        scratch_shapes=[pltpu.VMEM((bb, N_HEADS, HEAD_DIM, HEAD_DIM), F32),
                        pltpu.VMEM((bb, SUBLANES, G_QKV), F32)],
        compiler_params=pltpu.CompilerParams(dimension_semantics=("arbitrary", "arbitrary"),
                                             vmem_limit_bytes=VMEM_LIMIT_BYTES),
        name="gdn",
    )(pg, pg, pg, conv_buf, s0, *params)


def _rel_buckets(dist):
    n = np.maximum(dist, 0)
    nf = np.maximum(n, 1).astype(np.float32)
    large = REL_MAX_EXACT + (np.log(nf / REL_MAX_EXACT) / math.log(REL_MAX_DIST / REL_MAX_EXACT)
                             * (REL_BUCKETS - REL_MAX_EXACT)).astype(np.int32)
    return np.where(n < REL_MAX_EXACT, n, np.minimum(large, REL_BUCKETS - 1)).astype(np.int32)


def _bias_kernel(table_ref, bucket_ref, o_ref):
    bucket = bucket_ref[...]
    for h in range(S_Q_HEADS):
        acc = jnp.zeros(bucket.shape, F32)
        for k in range(REL_BUCKETS):
            acc = jnp.where(bucket == k, table_ref[k, h], acc)
        o_ref[h] = acc


def _rel_bias(table, dist):
    bucket = jnp.asarray(_rel_buckets(dist))
    return pl.pallas_call(
        _bias_kernel,
        in_specs=[pl.BlockSpec(memory_space=pltpu.SMEM), _full_spec(bucket.shape)],
        out_specs=_full_spec((S_Q_HEADS,) + bucket.shape),
        out_shape=jax.ShapeDtypeStruct((S_Q_HEADS,) + bucket.shape, F32),
        grid=(1,),
        name="rel_bias",
    )(table, bucket)


def _swa_prompt_kernel(sink_ref, q_ref, kp_ref, kc_ref, vp_ref, vc_ref, bias_ref, o_ref):
    n = pl.program_id(1)
    qi = lax.broadcasted_iota(jnp.int32, (ATTN_BLOCK, 2 * ATTN_BLOCK), 0)
    kj = lax.broadcasted_iota(jnp.int32, (ATTN_BLOCK, 2 * ATTN_BLOCK), 1)
    dist = ATTN_BLOCK + qi - kj
    mask = (dist >= 0) & (dist < WINDOW) & ((kj >= ATTN_BLOCK) | (n > 0))
    q = q_ref[...]
    kband = jnp.concatenate([kp_ref[...], kc_ref[...]], axis=0).astype(BF16)
    vband = jnp.concatenate([vp_ref[...], vc_ref[...]], axis=0).astype(BF16)
    for h in range(S_Q_HEADS):
        kv = h // S_GROUP
        kvs = slice(kv * HEAD_DIM, (kv + 1) * HEAD_DIM)
        qh = q[:, h * HEAD_DIM:(h + 1) * HEAD_DIM].astype(BF16)
        logits = _dg(qh, kband[:, kvs], NT) * (HEAD_DIM ** -0.5) + bias_ref[h]
        logits = jnp.where(mask, logits, NEG_INF)
        sink = sink_ref[h]
        m = jnp.maximum(jnp.max(logits, axis=-1, keepdims=True), sink)
        p = jnp.exp(logits - m)
        probs = p / (jnp.sum(p, axis=-1, keepdims=True) + jnp.exp(sink - m))
        o_ref[:, h * HEAD_DIM:(h + 1) * HEAD_DIM] = _dg(probs.astype(BF16), vband[:, kvs], NN)


def _swa_prompt(ps, sinks, bias):
    b, t, _ = ps.shape
    nb = t // ATTN_BLOCK
    kcol = S_WIDTH // S_KV_WIDTH
    prev = lambda i, j: jnp.maximum(j - 1, 0)
    blk = lambda w: (None, ATTN_BLOCK, w)
    return pl.pallas_call(
        _swa_prompt_kernel,
        grid=(b, nb),
        in_specs=[pl.BlockSpec(memory_space=pltpu.SMEM),
                  pl.BlockSpec(blk(S_WIDTH), lambda i, j: (i, j, 0)),
                  pl.BlockSpec(blk(S_KV_WIDTH), lambda i, j: (i, prev(i, j), kcol)),
                  pl.BlockSpec(blk(S_KV_WIDTH), lambda i, j: (i, j, kcol)),
                  pl.BlockSpec(blk(S_KV_WIDTH), lambda i, j: (i, prev(i, j), kcol + 1)),
                  pl.BlockSpec(blk(S_KV_WIDTH), lambda i, j: (i, j, kcol + 1)),
                  _full_spec(bias.shape)],
        out_specs=pl.BlockSpec(blk(S_WIDTH), lambda i, j: (i, j, 0)),
        out_shape=jax.ShapeDtypeStruct((b, t, S_WIDTH), F32),
        compiler_params=pltpu.CompilerParams(dimension_semantics=("arbitrary", "arbitrary"),
                                             vmem_limit_bytes=VMEM_LIMIT_BYTES),
        name="swa_prompt",
    )(sinks, ps, ps, ps, ps, ps, bias)


def _swa_sample_kernel(sink_ref, q_ref, kn_ref, vn_ref, ck_ref, cv_ref, bias_c_ref, bias_n_ref,
                       o_ref, cko_ref, cvo_ref, *, bb, tp, tv):
    wc = WINDOW
    step = lambda n: jnp.concatenate([lax.broadcasted_iota(jnp.int32, (tp, n), 0)] * S_GROUP, axis=0)
    ti_c = step(wc)
    kj_c = lax.broadcasted_iota(jnp.int32, (S_GROUP * tp, wc), 1)
    dist_c = wc + ti_c - kj_c
    mask_c = (dist_c >= 0) & (dist_c < WINDOW)
    ti_n = step(tp)
    kj_n = lax.broadcasted_iota(jnp.int32, (S_GROUP * tp, tp), 1)
    dist_n = ti_n - kj_n
    mask_n = (dist_n >= 0) & (dist_n < WINDOW) & (kj_n < tv)

    def one_seq(i, carry):
        q = q_ref[i]
        kn = kn_ref[i]
        vn = vn_ref[i]
        ck = ck_ref[i]
        cv = cv_ref[i]
        for kv in range(S_KV_HEADS):
            kvs = slice(kv * HEAD_DIM, (kv + 1) * HEAD_DIM)
            qg = jnp.concatenate([q[:, (kv * S_GROUP + g) * HEAD_DIM:(kv * S_GROUP + g + 1) * HEAD_DIM]
                                  for g in range(S_GROUP)], axis=0).astype(BF16)
            lc = _dg(qg, ck[:, kvs].astype(BF16), NT) * (HEAD_DIM ** -0.5) + bias_c_ref[kv]
            ln = _dg(qg, kn[:, kvs].astype(BF16), NT) * (HEAD_DIM ** -0.5) + bias_n_ref[kv]
            lc = jnp.where(mask_c, lc, NEG_INF)
            ln = jnp.where(mask_n, ln, NEG_INF)
            sink = jnp.concatenate([jnp.full((tp, 1), sink_ref[kv * S_GROUP + g], F32)
                                    for g in range(S_GROUP)], axis=0)
            m = jnp.maximum(jnp.maximum(jnp.max(lc, axis=-1, keepdims=True),
                                        jnp.max(ln, axis=-1, keepdims=True)), sink)
            pc = jnp.exp(lc - m)
            pn = jnp.exp(ln - m)
            den = (jnp.sum(pc, axis=-1, keepdims=True) + jnp.sum(pn, axis=-1, keepdims=True)
                   + jnp.exp(sink - m))
            og = (_dg((pc / den).astype(BF16), cv[:, kvs].astype(BF16), NN)
                  + _dg((pn / den).astype(BF16), vn[:, kvs].astype(BF16), NN))
            for g in range(S_GROUP):
                hq = kv * S_GROUP + g
                o_ref[i, :, hq * HEAD_DIM:(hq + 1) * HEAD_DIM] = og[g * tp:(g + 1) * tp, :]
        cko_ref[i] = jnp.where(
            lax.broadcasted_iota(jnp.int32, (wc, 1), 0) < wc - tv,
            pltpu.roll(ck, wc - tv, 0),
            pltpu.roll(jnp.concatenate([kn, jnp.zeros((wc - tp, S_KV_WIDTH), F32)], axis=0), wc - tv, 0))
        cvo_ref[i] = jnp.where(
            lax.broadcasted_iota(jnp.int32, (wc, 1), 0) < wc - tv,
            pltpu.roll(cv, wc - tv, 0),
            pltpu.roll(jnp.concatenate([vn, jnp.zeros((wc - tp, S_KV_WIDTH), F32)], axis=0), wc - tv, 0))
        return carry

    lax.fori_loop(0, bb, one_seq, 0)


def _swa_sample(ps, cache_k, cache_v, sinks, bias_c, bias_n, *, bb, tv):
    b, tp, _ = ps.shape
    kcol = S_WIDTH // S_KV_WIDTH
    kern = functools.partial(_swa_sample_kernel, bb=bb, tp=tp, tv=tv)
    cache_spec = pl.BlockSpec((bb, WINDOW, S_KV_WIDTH), lambda i: (i, 0, 0))
    return pl.pallas_call(
        kern,
        grid=(b // bb,),
        in_specs=[pl.BlockSpec(memory_space=pltpu.SMEM),
                  pl.BlockSpec((bb, tp, S_WIDTH), lambda i: (i, 0, 0)),
                  pl.BlockSpec((bb, tp, S_KV_WIDTH), lambda i: (i, 0, kcol)),
                  pl.BlockSpec((bb, tp, S_KV_WIDTH), lambda i: (i, 0, kcol + 1)),
                  cache_spec, cache_spec, _full_spec(bias_c.shape), _full_spec(bias_n.shape)],
        out_specs=[pl.BlockSpec((bb, tp, S_WIDTH), lambda i: (i, 0, 0)), cache_spec, cache_spec],
        out_shape=[jax.ShapeDtypeStruct((b, tp, S_WIDTH), F32),
                   jax.ShapeDtypeStruct(cache_k.shape, F32), jax.ShapeDtypeStruct(cache_v.shape, F32)],
        compiler_params=pltpu.CompilerParams(dimension_semantics=("arbitrary",),
                                             vmem_limit_bytes=VMEM_LIMIT_BYTES),
        name="swa_sample",
    )(sinks, ps, ps, ps, cache_k, cache_v, bias_c, bias_n)


def _prep_layer(l, ffn1_w_in, ffn1_w_out, ln1_g, ln1_b, w_in, rwkv_mu, rwkv_w0, rwkv_w_up, rwkv_a0,
                rwkv_a_up, rwkv_g_up, rwkv_k_k, rwkv_k_a, rwkv_r_k, rwkv_gn_g, rwkv_gn_b, swa_sinks,
                gdn_conv_w, gdn_a_log, gdn_dt_bias, gdn_norm_g, w_out, ln2_g, ln2_b,
                ffn2_w_in, ffn2_w_out, ln3_g, ln3_b):
    row = lambda a: a[l].reshape(1, -1)
    d = w_in.shape[1]
    n_beta = 4 * G_WIDTH
    win = jnp.concatenate([w_in[l], jnp.zeros((d, PROJ_PAD - w_in.shape[2]), F32)], axis=1).astype(BF16)
    lane_tile = lambda a: jnp.zeros((1, LANES), F32).at[0, N_HEADS:2 * N_HEADS].set(a[l])
    del n_beta
    return dict(
        ffn1=(ffn1_w_in[l].astype(BF16), ffn1_w_out[l].astype(BF16), row(ln1_g), row(ln1_b), win),
        rwkv=(row(rwkv_mu), row(rwkv_w0), rwkv_w_up[l], row(rwkv_a0), rwkv_a_up[l], rwkv_g_up[l],
              row(rwkv_k_k), row(rwkv_k_a), row(rwkv_r_k), row(rwkv_gn_g), row(rwkv_gn_b)),
        sinks=swa_sinks[l],
        gdn=(gdn_conv_w[l], lane_tile(gdn_a_log), lane_tile(gdn_dt_bias), row(gdn_norm_g)),
        out=(w_out[l].astype(BF16), row(ln2_g), row(ln2_b), ffn2_w_in[l].astype(BF16),
             ffn2_w_out[l].astype(BF16), row(ln3_g), row(ln3_b)),
    )


def _run_trunk(x, rwkv_s, rwkv_shift, swa_k, swa_v, gdn_s, gdn_conv, layers, biases, *, prompt, tv, alpha,
               tm, n_chunks, bb, c):
    b, t, d = x.shape
    xf = x.reshape(b * t, d)
    new = [[] for _ in range(6)]
    for l, lp in enumerate(layers):
        x1, pr, ps, pg = _ffn_proj(xf, *lp["ffn1"], alpha=alpha, tm=tm, n_chunks=n_chunks)
        pr = pr.reshape(b, t, R_PROJ)
        ps = ps.reshape(b, t, S_PROJ)
        pg = pg.reshape(b, t, G_PROJ_PAD)
        y_r, s_r, sh_r = _rwkv_mixer(pr, rwkv_shift[l], rwkv_s[l], lp["rwkv"], bb=bb, c=c, tv=tv)
        if prompt:
            y_s = _swa_prompt(ps, lp["sinks"], biases[0])
            kc = ps[:, t - WINDOW:, S_WIDTH:S_WIDTH + S_KV_WIDTH]
            vc = ps[:, t - WINDOW:, S_WIDTH + S_KV_WIDTH:]
        else:
            y_s, kc, vc = _swa_sample(ps, swa_k[l], swa_v[l], lp["sinks"], biases[1], biases[2], bb=bb, tv=tv)
        y_g, s_g, cb = _gdn_mixer(pg, gdn_conv[l], gdn_s[l], lp["gdn"], bb=bb, c=c, tv=tv)
        xf = _out_ffn(x1, y_r.reshape(b * t, R_WIDTH), y_s.reshape(b * t, S_WIDTH), y_g.reshape(b * t, G_WIDTH),
                      *lp["out"], alpha=alpha, tm=tm, n_chunks=n_chunks)
        for lst, a in zip(new, (s_r, sh_r, kc.reshape(b, WINDOW, S_KV_HEADS, HEAD_DIM),
                                vc.reshape(b, WINDOW, S_KV_HEADS, HEAD_DIM), s_g, cb)):
            lst.append(a)
    return xf.reshape(b, t, d), [jnp.stack(lst, axis=0) for lst in new]


def kernel(x_prompt, x_sample, state_rwkv, state_rwkv_shift, cache_swa_k, cache_swa_v, state_gdn, state_gdn_conv, ffn1_w_in, ffn1_w_out, ln1_g, ln1_b, w_in, rwkv_mu, rwkv_w0, rwkv_w_up, rwkv_a0, rwkv_a_up, rwkv_g_up, rwkv_k_k, rwkv_k_a, rwkv_r_k, rwkv_gn_g, rwkv_gn_b, swa_sinks, rel_table, gdn_conv_w, gdn_a_log, gdn_dt_bias, gdn_norm_g, w_out, ln2_g, ln2_b, ffn2_w_in, ffn2_w_out, ln3_g, ln3_b):
    depth = ffn1_w_in.shape[0]
    alpha = (2 * depth) ** 0.25
    layers = [_prep_layer(l, ffn1_w_in, ffn1_w_out, ln1_g, ln1_b, w_in, rwkv_mu, rwkv_w0, rwkv_w_up, rwkv_a0,
                          rwkv_a_up, rwkv_g_up, rwkv_k_k, rwkv_k_a, rwkv_r_k, rwkv_gn_g, rwkv_gn_b, swa_sinks,
                          gdn_conv_w, gdn_a_log, gdn_dt_bias, gdn_norm_g, w_out, ln2_g, ln2_b,
                          ffn2_w_in, ffn2_w_out, ln3_g, ln3_b) for l in range(depth)]
    bp, tp_len, d = x_prompt.shape
    bs, ts, _ = x_sample.shape
    ts_pad = -(-ts // SUBLANES) * SUBLANES

    qi = np.arange(ATTN_BLOCK)[:, None]
    bias_p = _rel_bias(rel_table, ATTN_BLOCK + qi - np.arange(2 * ATTN_BLOCK)[None, :])
    ti = (np.arange(S_GROUP * ts_pad) % ts_pad)[:, None]
    wc = cache_swa_k.shape[2]
    bias_c = _rel_bias(rel_table, wc + ti - np.arange(wc)[None, :])
    bias_n = _rel_bias(rel_table, ti - np.arange(ts_pad)[None, :])
    regroup = lambda a: jnp.stack([jnp.concatenate([a[kv * S_GROUP + g, g * ts_pad:(g + 1) * ts_pad]
                                                    for g in range(S_GROUP)], axis=0)
                                   for kv in range(S_KV_HEADS)], axis=0)
    biases = (bias_p, regroup(bias_c), regroup(bias_n))

    zeros = lambda *s: jnp.zeros((depth, bp) + s, F32)
    y_prompt, p_states = _run_trunk(
        x_prompt, zeros(N_HEADS, HEAD_DIM, HEAD_DIM), zeros(R_PROJ), None, None,
        zeros(N_HEADS, HEAD_DIM, HEAD_DIM), zeros(G_CONV - 1, G_QKV), layers, biases,
        prompt=True, tv=64, alpha=alpha, tm=256, n_chunks=2, bb=1, c=64)

    xs = jnp.concatenate([x_sample, jnp.zeros((bs, ts_pad - ts, d), F32)], axis=1)
    ck = cache_swa_k.reshape(depth, bs, wc, S_KV_WIDTH)
    cv = cache_swa_v.reshape(depth, bs, wc, S_KV_WIDTH)
    y_sample, s_states = _run_trunk(
        xs, state_rwkv, state_rwkv_shift, ck, cv, state_gdn, state_gdn_conv, layers, biases,
        prompt=False, tv=ts, alpha=alpha, tm=256, n_chunks=2, bb=8, c=ts_pad)
    return (y_prompt, y_sample[:, :ts]) + tuple(p_states) + tuple(s_states)
```

```python
import functools
import math

import numpy as np
import jax
import jax.numpy as jnp
from jax import lax
from jax.experimental import pallas as pl
from jax.experimental.pallas import tpu as pltpu

F32 = jnp.float32
BF16 = jnp.bfloat16

HEAD_DIM = 64
N_HEADS = 4
R_WIDTH = N_HEADS * HEAD_DIM
R_PROJ = 896
S_Q_HEADS = 8
S_KV_HEADS = 2
S_GROUP = S_Q_HEADS // S_KV_HEADS
S_WIDTH = S_Q_HEADS * HEAD_DIM
S_KV_WIDTH = S_KV_HEADS * HEAD_DIM
S_PROJ = S_WIDTH + 2 * S_KV_WIDTH
G_WIDTH = N_HEADS * HEAD_DIM
G_CONV = 4
G_QKV = 3 * G_WIDTH
G_PROJ_PAD = 4 * G_WIDTH + 128
WINDOW = 128
ATTN_BLOCK = 128
REL_BUCKETS = 32
REL_MAX_EXACT = 16
REL_MAX_DIST = 128
NEG_INF = -1e30
R_GN_EPS = 64e-5
G_NORM_EPS = 1e-6
LN_EPS = 1e-5
L2_EPS = 1e-6
PROJ_PAD = R_PROJ + S_PROJ + G_PROJ_PAD

LANES = 128
SUBLANES = 8
VMEM_LIMIT_BYTES = 56 * 1024 * 1024

NN = ((1,), (0,))
NT = ((1,), (1,))
TN = ((0,), (0,))


def _dg(a, b, dims):
    return lax.dot_general(a, b, (dims, ((), ())), preferred_element_type=F32)


def _split2(a):
    hi = a.astype(BF16)
    lo = (a - hi.astype(F32)).astype(BF16)
    return hi, lo


def _mm_exact_lhs(a_bf16, b, dims=NN):
    b1 = b.astype(BF16)
    r1 = b - b1.astype(F32)
    b2 = r1.astype(BF16)
    b3 = (r1 - b2.astype(F32)).astype(BF16)
    return _dg(a_bf16, b1, dims) + (_dg(a_bf16, b2, dims) + _dg(a_bf16, b3, dims))


def _sigmoid(x):
    return 1.0 / (1.0 + jnp.exp(-x))


def _silu(x):
    return x * _sigmoid(x)


def _softplus(x):
    return jnp.maximum(x, 0.0) + jnp.log(1.0 + jnp.exp(-jnp.abs(x)))


def _layer_norm(z, g, b):
    mu = jnp.mean(z, axis=-1, keepdims=True)
    zc = z - mu
    var = jnp.mean(zc * zc, axis=-1, keepdims=True)
    return zc * lax.rsqrt(var + LN_EPS) * g + b


def _mm3_each(a_list, b_list, dims=NN):
    sa = [_split2(a) for a in a_list]
    sb = [_split2(b) for b in b_list]
    return [_dg(ah, bh, dims) + (_dg(ah, bl, dims) + _dg(al, bh, dims))
            for (ah, al), (bh, bl) in zip(sa, sb)]


def _tri_inverse_each(m_list, c):
    row = lax.broadcasted_iota(jnp.int32, (c, c), 0)
    col = lax.broadcasted_iota(jnp.int32, (c, c), 1)
    eye = jnp.where(row == col, 1.0, 0.0).astype(F32)
    t = [eye + m for m in m_list]
    mp = m_list
    span = 2
    while span < c:
        mp = _mm3_each(mp, mp)
        t = [a + b for a, b in zip(t, _mm3_each(t, mp))]
        span *= 2
    return t


SEQ_UNROLL = 4


def _for_each_group(bb, one_group):
    unroll = min(SEQ_UNROLL, bb)
    if bb == unroll:
        one_group(list(range(unroll)))
        return

    def group(gi, carry):
        one_group([gi * unroll + u for u in range(unroll)])
        return carry

    lax.fori_loop(0, bb // unroll, group, 0)


def _full_spec(shape):
    nd = len(shape)
    return pl.BlockSpec(shape, lambda *_: (0,) * nd)


def _resident_spec(shape, index_map):
    return pl.BlockSpec(shape, index_map, pipeline_mode=pl.Buffered(1))


def _swiglu(xb, wi_ref, wo_ref, d_ff, n_chunks):
    cw = d_ff // n_chunks
    acc = None
    for j in range(n_chunks):
        gate = _dg(xb, wi_ref[:, j * cw:(j + 1) * cw], NN)
        up = _dg(xb, wi_ref[:, d_ff + j * cw:d_ff + (j + 1) * cw], NN)
        act = (_silu(gate) * up).astype(BF16)
        part = _dg(act, wo_ref[j * cw:(j + 1) * cw, :], NN)
        acc = part if acc is None else acc + part
    return acc


def _ffn_proj_kernel(x_ref, wi_ref, wo_ref, g_ref, b_ref, win_ref,
                     x1_ref, pr_ref, ps_ref, pg_ref, *, alpha, d_ff, n_chunks):
    x = x_ref[...]
    y = _swiglu(x.astype(BF16), wi_ref, wo_ref, d_ff, n_chunks)
    x1 = _layer_norm(alpha * x + 0.5 * y, g_ref[...], b_ref[...])
    x1_ref[...] = x1
    p = _dg(x1.astype(BF16), win_ref[...], NN)
    pr_ref[...] = p[:, :R_PROJ]
    ps_ref[...] = p[:, R_PROJ:R_PROJ + S_PROJ]
    pg_ref[...] = p[:, R_PROJ + S_PROJ:]


def _ffn_proj(x, wi, wo, g, b, win, *, alpha, tm, n_chunks):
    m, d = x.shape
    d_ff = wo.shape[0]
    kern = functools.partial(_ffn_proj_kernel, alpha=alpha, d_ff=d_ff, n_chunks=n_chunks)
    row = lambda w: pl.BlockSpec((tm, w), lambda i: (i, 0))
    const = lambda i: (0, 0)
    return pl.pallas_call(
        kern,
        grid=(m // tm,),
        in_specs=[row(d),
                  _resident_spec(wi.shape, const), _resident_spec(wo.shape, const),
                  _resident_spec(g.shape, const), _resident_spec(b.shape, const),
                  _resident_spec(win.shape, const)],
        out_specs=[row(d), row(R_PROJ), row(S_PROJ), row(G_PROJ_PAD)],
        out_shape=[jax.ShapeDtypeStruct((m, d), F32), jax.ShapeDtypeStruct((m, R_PROJ), F32),
                   jax.ShapeDtypeStruct((m, S_PROJ), F32), jax.ShapeDtypeStruct((m, G_PROJ_PAD), F32)],
        compiler_params=pltpu.CompilerParams(dimension_semantics=("arbitrary",),
                                             vmem_limit_bytes=VMEM_LIMIT_BYTES),
        name="ffn_proj",
    )(x, wi, wo, g, b, win)


def _out_ffn_kernel(x_ref, yr_ref, ys_ref, yg_ref, wout_ref, g2_ref, b2_ref, wi_ref, wo_ref, g3_ref, b3_ref,
                    o_ref, *, alpha, d_ff, n_chunks):
    x = x_ref[...]
    mix = (_dg(yr_ref[...].astype(BF16), wout_ref[0:R_WIDTH, :], NN)
           + _dg(ys_ref[...].astype(BF16), wout_ref[R_WIDTH:R_WIDTH + S_WIDTH, :], NN)
           + _dg(yg_ref[...].astype(BF16), wout_ref[R_WIDTH + S_WIDTH:, :], NN))
    x2 = _layer_norm(alpha * x + mix, g2_ref[...], b2_ref[...])
    y = _swiglu(x2.astype(BF16), wi_ref, wo_ref, d_ff, n_chunks)
    o_ref[...] = _layer_norm(alpha * x2 + 0.5 * y, g3_ref[...], b3_ref[...])


def _out_ffn(x, yr, ys, yg, wout, g2, b2, wi, wo, g3, b3, *, alpha, tm, n_chunks):
    m, d = x.shape
    d_ff = wo.shape[0]
    kern = functools.partial(_out_ffn_kernel, alpha=alpha, d_ff=d_ff, n_chunks=n_chunks)
    row = lambda w: pl.BlockSpec((tm, w), lambda i: (i, 0))
    const = lambda i: (0, 0)
    res = lambda a: _resident_spec(a.shape, const)
    return pl.pallas_call(
        kern,
        grid=(m // tm,),
        in_specs=[row(d), row(R_WIDTH), row(S_WIDTH), row(G_WIDTH),
                  res(wout), res(g2), res(b2), res(wi), res(wo), res(g3), res(b3)],
        out_specs=row(d),
        out_shape=jax.ShapeDtypeStruct((m, d), F32),
        compiler_params=pltpu.CompilerParams(dimension_semantics=("arbitrary",),
                                             vmem_limit_bytes=VMEM_LIMIT_BYTES),
        name="out_ffn",
    )(x, yr, ys, yg, wout, g2, b2, wi, wo, g3, b3)


def _rwkv_kernel(f_ref, shift_ref, s0_ref, mu_ref, w0_ref, wup_ref, a0_ref, aup_ref, gup_ref,
                 kk_ref, ka_ref, rk_ref, gng_ref, gnb_ref,
                 y_ref, sout_ref, shout_ref, s_scr, prev_scr, *, bb, c, tv, nc):
    ci = pl.program_id(1)

    @pl.when(ci == 0)
    def _():
        s_scr[...] = s0_ref[...]
        prev_scr[...] = shift_ref[...]

    row = lax.broadcasted_iota(jnp.int32, (c, 1), 0)
    ri = lax.broadcasted_iota(jnp.int32, (c, 2 * c), 0)
    cj = lax.broadcasted_iota(jnp.int32, (c, 2 * c), 1)
    cj = jnp.where(cj >= c, cj - c, cj)
    strict = ri > cj
    incl = ri >= cj
    tri = (lax.broadcasted_iota(jnp.int32, (c, c), 0) >= lax.broadcasted_iota(jnp.int32, (c, c), 1))
    tri = jnp.where(tri, 1.0, 0.0).astype(BF16)
    valid = row < tv

    def one_group(seqs):
        n_seq = len(seqs)
        heads = [(n, h) for n in range(n_seq) for h in range(N_HEADS)]
        hs = lambda arr, h: arr[:, h * HEAD_DIM:(h + 1) * HEAD_DIM]
        fl = [f_ref[i] for i in seqs]
        prevs = [jnp.where(row == 0, prev_scr[i], pltpu.roll(f, 1, 0)) for i, f in zip(seqs, fl)]
        for i, f in zip(seqs, fl):
            prev_scr[i] = f[c - 1:c, :]
        fsl = [f + (p - f) * mu_ref[...] for f, p in zip(fl, prevs)]
        w_l = _mm3_each([jnp.tanh(fs[:, 768:800]) for fs in fsl], [wup_ref[...]] * n_seq)
        lr_l = _mm3_each([fs[:, 800:832] for fs in fsl], [aup_ref[...]] * n_seq)
        gate_l = _mm3_each([_sigmoid(fs[:, 832:896]) for fs in fsl], [gup_ref[...]] * n_seq)
        lr_l = [_sigmoid(a0_ref[...] + x) for x in lr_l]
        lw_l = [-jnp.exp(-_softplus(-(w0_ref[...] + x)) - 0.5) for x in w_l]
        r_l = [fs[:, 0:R_WIDTH] for fs in fsl]
        k_l = [fs[:, R_WIDTH:2 * R_WIDTH] for fs in fsl]
        v_l = [fs[:, 2 * R_WIDTH:3 * R_WIDTH] for fs in fsl]
        kkx_l = [k * kk_ref[...] for k in k_l]
        k2_l = [k * (1.0 + (lr - 1.0) * ka_ref[...]) for k, lr in zip(k_l, lr_l)]
        if tv < c:
            lw_l = [jnp.where(valid, x, 0.0) for x in lw_l]
            kkx_l = [jnp.where(valid, x, 0.0) for x in kkx_l]
            k2_l = [jnp.where(valid, x, 0.0) for x in k2_l]
        cum_l = [_mm_exact_lhs(tri, lw) for lw in lw_l]
        p_in = [jnp.exp(x) for x in cum_l]
        p_prev = [jnp.exp(x - lw) for x, lw in zip(cum_l, lw_l)]
        p_inv = [jnp.exp(-x) for x in cum_l]
        xs, y2s, vhs, ss = [], [], [], []
        for n, h in heads:
            kx = hs(kkx_l[n], h)
            kkh = kx * lax.rsqrt(jnp.sum(kx * kx, axis=-1, keepdims=True) + L2_EPS)
            at = -kkh * hs(p_prev[n], h)
            bt = kkh * hs(lr_l[n], h) * hs(p_inv[n], h)
            kt = hs(k2_l[n], h) * hs(p_inv[n], h)
            qt = hs(r_l[n], h) * hs(p_in[n], h)
            xs.append(jnp.concatenate([at, qt], axis=0))
            y2s.append(jnp.concatenate([bt, kt], axis=0))
            vhs.append(hs(v_l[n], h))
            ss.append(s_scr[seqs[n], h])
        gram = _mm3_each(xs, y2s, NT)
        xst = _mm3_each(xs, ss, NT)
        xa = [jnp.where(strict, g[:c], 0.0) for g in gram]
        xq = [jnp.where(incl, g[c:], 0.0) for g in gram]
        lakv = _mm3_each([x[:, c:] for x in xa], vhs)
        t = _tri_inverse_each([x[:, :c] for x in xa], c)
        u = _mm3_each(t, [a[:c] + b for a, b in zip(xst, lakv)])
        uv = [jnp.concatenate([a, b], axis=0) for a, b in zip(u, vhs)]
        o_l = [a[c:] + b for a, b in zip(xst, _mm3_each(xq, uv))]
        ds = _mm3_each(uv, y2s, TN)
        for (n, h), s, d, o, vh in zip(heads, ss, ds, o_l, vhs):
            sl = slice(h * HEAD_DIM, (h + 1) * HEAD_DIM)
            s_scr[seqs[n], h] = (s + d) * p_in[n][c - 1:c, sl]
            mean = jnp.mean(o, axis=-1, keepdims=True)
            oc = o - mean
            var = jnp.mean(oc * oc, axis=-1, keepdims=True)
            on = oc * lax.rsqrt(var + R_GN_EPS) * gng_ref[:, sl] + gnb_ref[:, sl]
            bonus = jnp.sum(r_l[n][:, sl] * k2_l[n][:, sl] * rk_ref[:, sl], axis=-1, keepdims=True) * vh
            y_ref[seqs[n], :, sl] = (on + bonus) * gate_l[n][:, sl]

        @pl.when(ci == nc - 1)
        def _():
            for i, f in zip(seqs, fl):
                shout_ref[i] = f[tv - 1:tv, :]

    _for_each_group(bb, one_group)

    @pl.when(ci == nc - 1)
    def _():
        sout_ref[...] = s_scr[...]


def _rwkv_mixer(f, shift_prev, s0, params, *, bb, c, tv):
    b, t, _ = f.shape
    nc = t // c
    kern = functools.partial(_rwkv_kernel, bb=bb, c=c, tv=tv, nc=nc)
    shift3 = shift_prev.reshape(b, 1, R_PROJ)
    y, s_out, sh_out = pl.pallas_call(
        kern,
        grid=(b // bb, nc),
        in_specs=[pl.BlockSpec((bb, c, R_PROJ), lambda i, j: (i, j, 0)),
                  pl.BlockSpec((bb, 1, R_PROJ), lambda i, j: (i, 0, 0)),
                  pl.BlockSpec((bb, N_HEADS, HEAD_DIM, HEAD_DIM), lambda i, j: (i, 0, 0, 0))]
                 + [_full_spec(p.shape) for p in params],
        out_specs=[pl.BlockSpec((bb, c, R_WIDTH), lambda i, j: (i, j, 0)),
                   pl.BlockSpec((bb, N_HEADS, HEAD_DIM, HEAD_DIM), lambda i, j: (i, 0, 0, 0)),
                   pl.BlockSpec((bb, 1, R_PROJ), lambda i, j: (i, 0, 0))],
        out_shape=[jax.ShapeDtypeStruct((b, t, R_WIDTH), F32),
                   jax.ShapeDtypeStruct((b, N_HEADS, HEAD_DIM, HEAD_DIM), F32),
                   jax.ShapeDtypeStruct((b, 1, R_PROJ), F32)],
        scratch_shapes=[pltpu.VMEM((bb, N_HEADS, HEAD_DIM, HEAD_DIM), F32),
                        pltpu.VMEM((bb, 1, R_PROJ), F32)],
        compiler_params=pltpu.CompilerParams(dimension_semantics=("arbitrary", "arbitrary"),
                                             vmem_limit_bytes=VMEM_LIMIT_BYTES),
        name="rwkv7",
    )(f, shift3, s0, *params)
    return y, s_out, sh_out.reshape(b, R_PROJ)


def _gdn_kernel(x_ref, z_ref, gb_ref, cb_ref, s0_ref, cw_ref, alog_ref, dtb_ref, ng_ref,
                y_ref, sout_ref, cbout_ref, s_scr, tail_scr, *, bb, c, tv, nc):
    ci = pl.program_id(1)

    @pl.when(ci == 0)
    def _():
        s_scr[...] = s0_ref[...]
        tail_scr[...] = jnp.zeros_like(tail_scr)
        tail_scr[:, SUBLANES - (G_CONV - 1):SUBLANES, :] = cb_ref[...]

    row = lax.broadcasted_iota(jnp.int32, (c, 1), 0)
    ri = lax.broadcasted_iota(jnp.int32, (c, c), 0)
    cj = lax.broadcasted_iota(jnp.int32, (c, c), 1)
    strict = ri > cj
    incl = ri >= cj
    tri = jnp.where(incl, 1.0, 0.0).astype(BF16)
    valid = row < tv

    def one_group(seqs):
        n_seq = len(seqs)
        heads = [(n, h) for n in range(n_seq) for h in range(N_HEADS)]
        x_l = [x_ref[i] for i in seqs]
        act_l = []
        for i, x in zip(seqs, x_l):
            xe = jnp.concatenate([tail_scr[i], x], axis=0)
            conv = x * cw_ref[G_CONV - 1:G_CONV, :]
            for s in range(1, G_CONV):
                conv = conv + pltpu.roll(xe, s, 0)[SUBLANES:, :] * cw_ref[G_CONV - 1 - s:G_CONV - s, :]
            tail_scr[i] = x[c - SUBLANES:, :]
            act_l.append(_silu(conv))
        gb_l = [gb_ref[i] for i in seqs]
        beta_l = [_sigmoid(g) for g in gb_l]
        g_l = [-jnp.exp(alog_ref[...]) * _softplus(g + dtb_ref[...]) for g in gb_l]
        if tv < c:
            beta_l = [jnp.where(valid, x, 0.0) for x in beta_l]
            g_l = [jnp.where(valid, x, 0.0) for x in g_l]
        gcum_l = [_mm_exact_lhs(tri, g) for g in g_l]
        pad = lambda g: jnp.concatenate([g, jnp.zeros((LANES - c, LANES), F32)], axis=0) if c < LANES else g
        gcum_t = [pad(g).T for g in gcum_l]
        qhs, khs, vbs, kbs, egs, decs, deci, kdec, elast, ss = [], [], [], [], [], [], [], [], [], []
        for n, h in heads:
            act = act_l[n]
            qx = act[:, h * HEAD_DIM:(h + 1) * HEAD_DIM]
            kx = act[:, G_WIDTH + h * HEAD_DIM:G_WIDTH + (h + 1) * HEAD_DIM]
            vh = act[:, 2 * G_WIDTH + h * HEAD_DIM:2 * G_WIDTH + (h + 1) * HEAD_DIM]
            qh = qx * lax.rsqrt(jnp.sum(qx * qx, axis=-1, keepdims=True) + L2_EPS) * (HEAD_DIM ** -0.5)
            kh = kx * lax.rsqrt(jnp.sum(kx * kx, axis=-1, keepdims=True) + L2_EPS)
            if tv < c:
                kh = jnp.where(valid, kh, 0.0)
            beta = beta_l[n][:, h:h + 1]
            gcol = gcum_l[n][:, N_HEADS + h:N_HEADS + h + 1]
            grow = gcum_t[n][N_HEADS + h:N_HEADS + h + 1, 0:c]
            glast = gcum_l[n][c - 1:c, N_HEADS + h:N_HEADS + h + 1]
            diff = gcol - grow
            decs.append(jnp.exp(jnp.where(strict, diff, NEG_INF)))
            deci.append(jnp.exp(jnp.where(incl, diff, NEG_INF)))
            eg = jnp.exp(gcol)
            qhs.append(qh)
            khs.append(kh)
            kbs.append(kh * beta)
            vbs.append(vh * beta)
            egs.append(eg)
            kdec.append(kh * jnp.exp(glast - gcol))
            elast.append(jnp.exp(glast))
            ss.append(s_scr[seqs[n], h])
        gram = _mm3_each([jnp.concatenate([kb, qh], axis=0) for kb, qh in zip(kbs, qhs)], khs, NT)
        t = _tri_inverse_each([-(g[:c] * d) for g, d in zip(gram, decs)], c)
        sol = _mm3_each(t, [jnp.concatenate([vb, kb * eg], axis=1) for vb, kb, eg in zip(vbs, kbs, egs)])
        ws = _mm3_each([jnp.concatenate([so[:, HEAD_DIM:], qh * eg], axis=0)
                        for so, qh, eg in zip(sol, qhs, egs)], ss)
        u = [so[:, :HEAD_DIM] - w[:c] for so, w in zip(sol, ws)]
        qku = _mm3_each([g[c:] * d for g, d in zip(gram, deci)], u)
        ds = _mm3_each(kdec, u, TN)
        for (n, h), s, d, el, w, qu in zip(heads, ss, ds, elast, ws, qku):
            sl = slice(h * HEAD_DIM, (h + 1) * HEAD_DIM)
            s_scr[seqs[n], h] = el * s + d
            o = w[c:] + qu
            o = o * lax.rsqrt(jnp.mean(o * o, axis=-1, keepdims=True) + G_NORM_EPS) * ng_ref[...]
            y_ref[seqs[n], :, sl] = o * _silu(z_ref[seqs[n], :, sl])

        @pl.when(ci == nc - 1)
        def _():
            shift = (c - (tv - (G_CONV - 1))) % c
            for i, x in zip(seqs, x_l):
                xs = pltpu.roll(x, shift, 0) if shift else x
                cbout_ref[i] = xs[0:G_CONV - 1, :]

    _for_each_group(bb, one_group)

    @pl.when(ci == nc - 1)
    def _():
        sout_ref[...] = s_scr[...]


def _gdn_mixer(pg, conv_buf, s0, params, *, bb, c, tv):
    b, t, _ = pg.shape
    nc = t // c
    kern = functools.partial(_gdn_kernel, bb=bb, c=c, tv=tv, nc=nc)
    state_spec = pl.BlockSpec((bb, N_HEADS, HEAD_DIM, HEAD_DIM), lambda i, j: (i, 0, 0, 0))
    cb_spec = pl.BlockSpec((bb, G_CONV - 1, G_QKV), lambda i, j: (i, 0, 0))
    return pl.pallas_call(
        kern,
        grid=(b // bb, nc),
        in_specs=[pl.BlockSpec((bb, c, G_QKV), lambda i, j: (i, j, 0)),
                  pl.BlockSpec((bb, c, G_WIDTH), lambda i, j: (i, j, G_QKV // G_WIDTH)),
                  pl.BlockSpec((bb, c, LANES), lambda i, j: (i, j, (G_QKV + G_WIDTH) // LANES)),
                  cb_spec, state_spec] + [_full_spec(p.shape) for p in params],
        out_specs=[pl.BlockSpec((bb, c, G_WIDTH), lambda i, j: (i, j, 0)), state_spec, cb_spec],
        out_shape=[jax.ShapeDtypeStruct((b, t, G_WIDTH), F32),
                   jax.ShapeDtypeStruct((b, N_HEADS, HEAD_DIM, HEAD_DIM), F32),
                   jax.ShapeDtypeStruct((b, G_CONV - 1, G_QKV), F32)],
        scratch_shapes=[pltpu.VMEM((bb, N_HEADS, HEAD_DIM, HEAD_DIM), F32),
                        pltpu.VMEM((bb, SUBLANES, G_QKV), F32)],
        compiler_params=pltpu.CompilerParams(dimension_semantics=("arbitrary", "arbitrary"),
                                             vmem_limit_bytes=VMEM_LIMIT_BYTES),
        name="gdn",
    )(pg, pg, pg, conv_buf, s0, *params)


def _rel_buckets(dist):
    n = np.maximum(dist, 0)
    nf = np.maximum(n, 1).astype(np.float32)
    large = REL_MAX_EXACT + (np.log(nf / REL_MAX_EXACT) / math.log(REL_MAX_DIST / REL_MAX_EXACT)
                             * (REL_BUCKETS - REL_MAX_EXACT)).astype(np.int32)
    return np.where(n < REL_MAX_EXACT, n, np.minimum(large, REL_BUCKETS - 1)).astype(np.int32)


def _bias_kernel(table_ref, bucket_ref, o_ref):
    bucket = bucket_ref[...]
    for h in range(S_Q_HEADS):
        acc = jnp.zeros(bucket.shape, F32)
        for k in range(REL_BUCKETS):
            acc = jnp.where(bucket == k, table_ref[k, h], acc)
        o_ref[h] = acc


def _rel_bias(table, dist):
    bucket = jnp.asarray(_rel_buckets(dist))
    return pl.pallas_call(
        _bias_kernel,
        in_specs=[pl.BlockSpec(memory_space=pltpu.SMEM), _full_spec(bucket.shape)],
        out_specs=_full_spec((S_Q_HEADS,) + bucket.shape),
        out_shape=jax.ShapeDtypeStruct((S_Q_HEADS,) + bucket.shape, F32),
        grid=(1,),
        name="rel_bias",
    )(table, bucket)


def _swa_prompt_kernel(sink_ref, q_ref, kp_ref, kc_ref, vp_ref, vc_ref, bias_ref, o_ref):
    n = pl.program_id(1)
    qi = lax.broadcasted_iota(jnp.int32, (ATTN_BLOCK, 2 * ATTN_BLOCK), 0)
    kj = lax.broadcasted_iota(jnp.int32, (ATTN_BLOCK, 2 * ATTN_BLOCK), 1)
    dist = ATTN_BLOCK + qi - kj
    mask = (dist >= 0) & (dist < WINDOW) & ((kj >= ATTN_BLOCK) | (n > 0))
    q = q_ref[...]
    kband = jnp.concatenate([kp_ref[...], kc_ref[...]], axis=0).astype(BF16)
    vband = jnp.concatenate([vp_ref[...], vc_ref[...]], axis=0).astype(BF16)
    for h in range(S_Q_HEADS):
        kv = h // S_GROUP
        kvs = slice(kv * HEAD_DIM, (kv + 1) * HEAD_DIM)
        qh = q[:, h * HEAD_DIM:(h + 1) * HEAD_DIM].astype(BF16)
        logits = _dg(qh, kband[:, kvs], NT) * (HEAD_DIM ** -0.5) + bias_ref[h]
        logits = jnp.where(mask, logits, NEG_INF)
        sink = sink_ref[h]
        m = jnp.maximum(jnp.max(logits, axis=-1, keepdims=True), sink)
        p = jnp.exp(logits - m)
        probs = p / (jnp.sum(p, axis=-1, keepdims=True) + jnp.exp(sink - m))
        o_ref[:, h * HEAD_DIM:(h + 1) * HEAD_DIM] = _dg(probs.astype(BF16), vband[:, kvs], NN)


def _swa_prompt(ps, sinks, bias):
    b, t, _ = ps.shape
    nb = t // ATTN_BLOCK
    kcol = S_WIDTH // S_KV_WIDTH
    prev = lambda i, j: jnp.maximum(j - 1, 0)
    blk = lambda w: (None, ATTN_BLOCK, w)
    return pl.pallas_call(
        _swa_prompt_kernel,
        grid=(b, nb),
        in_specs=[pl.BlockSpec(memory_space=pltpu.SMEM),
                  pl.BlockSpec(blk(S_WIDTH), lambda i, j: (i, j, 0)),
                  pl.BlockSpec(blk(S_KV_WIDTH), lambda i, j: (i, prev(i, j), kcol)),
                  pl.BlockSpec(blk(S_KV_WIDTH), lambda i, j: (i, j, kcol)),
                  pl.BlockSpec(blk(S_KV_WIDTH), lambda i, j: (i, prev(i, j), kcol + 1)),
                  pl.BlockSpec(blk(S_KV_WIDTH), lambda i, j: (i, j, kcol + 1)),
                  _full_spec(bias.shape)],
        out_specs=pl.BlockSpec(blk(S_WIDTH), lambda i, j: (i, j, 0)),
        out_shape=jax.ShapeDtypeStruct((b, t, S_WIDTH), F32),
        compiler_params=pltpu.CompilerParams(dimension_semantics=("arbitrary", "arbitrary"),
                                             vmem_limit_bytes=VMEM_LIMIT_BYTES),
        name="swa_prompt",
    )(sinks, ps, ps, ps, ps, ps, bias)


def _swa_sample_kernel(sink_ref, q_ref, kn_ref, vn_ref, ck_ref, cv_ref, bias_c_ref, bias_n_ref,
                       o_ref, cko_ref, cvo_ref, *, bb, tp, tv):
    wc = WINDOW
    step = lambda n: jnp.concatenate([lax.broadcasted_iota(jnp.int32, (tp, n), 0)] * S_GROUP, axis=0)
    ti_c = step(wc)
    kj_c = lax.broadcasted_iota(jnp.int32, (S_GROUP * tp, wc), 1)
    dist_c = wc + ti_c - kj_c
    mask_c = (dist_c >= 0) & (dist_c < WINDOW)
    ti_n = step(tp)
    kj_n = lax.broadcasted_iota(jnp.int32, (S_GROUP * tp, tp), 1)
    dist_n = ti_n - kj_n
    mask_n = (dist_n >= 0) & (dist_n < WINDOW) & (kj_n < tv)

    def one_seq(i):
        q = q_ref[i]
        kn = kn_ref[i]
        vn = vn_ref[i]
        ck = ck_ref[i]
        cv = cv_ref[i]
        for kv in range(S_KV_HEADS):
            kvs = slice(kv * HEAD_DIM, (kv + 1) * HEAD_DIM)
            qg = jnp.concatenate([q[:, (kv * S_GROUP + g) * HEAD_DIM:(kv * S_GROUP + g + 1) * HEAD_DIM]
                                  for g in range(S_GROUP)], axis=0).astype(BF16)
            lc = _dg(qg, ck[:, kvs].astype(BF16), NT) * (HEAD_DIM ** -0.5) + bias_c_ref[kv]
            ln = _dg(qg, kn[:, kvs].astype(BF16), NT) * (HEAD_DIM ** -0.5) + bias_n_ref[kv]
            lc = jnp.where(mask_c, lc, NEG_INF)
            ln = jnp.where(mask_n, ln, NEG_INF)
            sink = jnp.concatenate([jnp.full((tp, 1), sink_ref[kv * S_GROUP + g], F32)
                                    for g in range(S_GROUP)], axis=0)
            m = jnp.maximum(jnp.maximum(jnp.max(lc, axis=-1, keepdims=True),
                                        jnp.max(ln, axis=-1, keepdims=True)), sink)
            pc = jnp.exp(lc - m)
            pn = jnp.exp(ln - m)
            den = (jnp.sum(pc, axis=-1, keepdims=True) + jnp.sum(pn, axis=-1, keepdims=True)
                   + jnp.exp(sink - m))
            og = (_dg((pc / den).astype(BF16), cv[:, kvs].astype(BF16), NN)
                  + _dg((pn / den).astype(BF16), vn[:, kvs].astype(BF16), NN))
            for g in range(S_GROUP):
                hq = kv * S_GROUP + g
                o_ref[i, :, hq * HEAD_DIM:(hq + 1) * HEAD_DIM] = og[g * tp:(g + 1) * tp, :]
        cko_ref[i] = jnp.where(
            lax.broadcasted_iota(jnp.int32, (wc, 1), 0) < wc - tv,
            pltpu.roll(ck, wc - tv, 0),
            pltpu.roll(jnp.concatenate([kn, jnp.zeros((wc - tp, S_KV_WIDTH), F32)], axis=0), wc - tv, 0))
        cvo_ref[i] = jnp.where(
            lax.broadcasted_iota(jnp.int32, (wc, 1), 0) < wc - tv,
            pltpu.roll(cv, wc - tv, 0),
            pltpu.roll(jnp.concatenate([vn, jnp.zeros((wc - tp, S_KV_WIDTH), F32)], axis=0), wc - tv, 0))

    def one_group(seqs):
        for i in seqs:
            one_seq(i)

    _for_each_group(bb, one_group)


def _swa_sample(ps, cache_k, cache_v, sinks, bias_c, bias_n, *, bb, tv):
    b, tp, _ = ps.shape
    kcol = S_WIDTH // S_KV_WIDTH
    kern = functools.partial(_swa_sample_kernel, bb=bb, tp=tp, tv=tv)
    cache_spec = pl.BlockSpec((bb, WINDOW, S_KV_WIDTH), lambda i: (i, 0, 0))
    return pl.pallas_call(
        kern,
        grid=(b // bb,),
        in_specs=[pl.BlockSpec(memory_space=pltpu.SMEM),
                  pl.BlockSpec((bb, tp, S_WIDTH), lambda i: (i, 0, 0)),
                  pl.BlockSpec((bb, tp, S_KV_WIDTH), lambda i: (i, 0, kcol)),
                  pl.BlockSpec((bb, tp, S_KV_WIDTH), lambda i: (i, 0, kcol + 1)),
                  cache_spec, cache_spec, _full_spec(bias_c.shape), _full_spec(bias_n.shape)],
        out_specs=[pl.BlockSpec((bb, tp, S_WIDTH), lambda i: (i, 0, 0)), cache_spec, cache_spec],
        out_shape=[jax.ShapeDtypeStruct((b, tp, S_WIDTH), F32),
                   jax.ShapeDtypeStruct(cache_k.shape, F32), jax.ShapeDtypeStruct(cache_v.shape, F32)],
        compiler_params=pltpu.CompilerParams(dimension_semantics=("arbitrary",),
                                             vmem_limit_bytes=VMEM_LIMIT_BYTES),
        name="swa_sample",
    )(sinks, ps, ps, ps, cache_k, cache_v, bias_c, bias_n)


def _prep_layer(l, ffn1_w_in, ffn1_w_out, ln1_g, ln1_b, w_in, rwkv_mu, rwkv_w0, rwkv_w_up, rwkv_a0,
                rwkv_a_up, rwkv_g_up, rwkv_k_k, rwkv_k_a, rwkv_r_k, rwkv_gn_g, rwkv_gn_b, swa_sinks,
                gdn_conv_w, gdn_a_log, gdn_dt_bias, gdn_norm_g, w_out, ln2_g, ln2_b,
                ffn2_w_in, ffn2_w_out, ln3_g, ln3_b):
    row = lambda a: a[l].reshape(1, -1)
    d = w_in.shape[1]
    n_beta = 4 * G_WIDTH
    win = jnp.concatenate([w_in[l], jnp.zeros((d, PROJ_PAD - w_in.shape[2]), F32)], axis=1).astype(BF16)
    lane_tile = lambda a: jnp.zeros((1, LANES), F32).at[0, N_HEADS:2 * N_HEADS].set(a[l])
    del n_beta
    return dict(
        ffn1=(ffn1_w_in[l].astype(BF16), ffn1_w_out[l].astype(BF16), row(ln1_g), row(ln1_b), win),
        rwkv=(row(rwkv_mu), row(rwkv_w0), rwkv_w_up[l], row(rwkv_a0), rwkv_a_up[l], rwkv_g_up[l],
              row(rwkv_k_k), row(rwkv_k_a), row(rwkv_r_k), row(rwkv_gn_g), row(rwkv_gn_b)),
        sinks=swa_sinks[l],
        gdn=(gdn_conv_w[l], lane_tile(gdn_a_log), lane_tile(gdn_dt_bias), row(gdn_norm_g)),
        out=(w_out[l].astype(BF16), row(ln2_g), row(ln2_b), ffn2_w_in[l].astype(BF16),
             ffn2_w_out[l].astype(BF16), row(ln3_g), row(ln3_b)),
    )


def _run_trunk(x, rwkv_s, rwkv_shift, swa_k, swa_v, gdn_s, gdn_conv, layers, biases, *, prompt, tv, alpha,
               tm, n_chunks, bb, c):
    b, t, d = x.shape
    xf = x.reshape(b * t, d)
    new = [[] for _ in range(6)]
    for l, lp in enumerate(layers):
        x1, pr, ps, pg = _ffn_proj(xf, *lp["ffn1"], alpha=alpha, tm=tm, n_chunks=n_chunks)
        pr = pr.reshape(b, t, R_PROJ)
        ps = ps.reshape(b, t, S_PROJ)
        pg = pg.reshape(b, t, G_PROJ_PAD)
        y_r, s_r, sh_r = _rwkv_mixer(pr, rwkv_shift[l], rwkv_s[l], lp["rwkv"], bb=bb, c=c, tv=tv)
        if prompt:
            y_s = _swa_prompt(ps, lp["sinks"], biases[0])
            kc = ps[:, t - WINDOW:, S_WIDTH:S_WIDTH + S_KV_WIDTH]
            vc = ps[:, t - WINDOW:, S_WIDTH + S_KV_WIDTH:]
        else:
            y_s, kc, vc = _swa_sample(ps, swa_k[l], swa_v[l], lp["sinks"], biases[1], biases[2], bb=bb, tv=tv)
        y_g, s_g, cb = _gdn_mixer(pg, gdn_conv[l], gdn_s[l], lp["gdn"], bb=bb, c=c, tv=tv)
        xf = _out_ffn(x1, y_r.reshape(b * t, R_WIDTH), y_s.reshape(b * t, S_WIDTH), y_g.reshape(b * t, G_WIDTH),
                      *lp["out"], alpha=alpha, tm=tm, n_chunks=n_chunks)
        for lst, a in zip(new, (s_r, sh_r, kc.reshape(b, WINDOW, S_KV_HEADS, HEAD_DIM),
                                vc.reshape(b, WINDOW, S_KV_HEADS, HEAD_DIM), s_g, cb)):
            lst.append(a)
    return xf.reshape(b, t, d), [jnp.stack(lst, axis=0) for lst in new]


def kernel(x_prompt, x_sample, state_rwkv, state_rwkv_shift, cache_swa_k, cache_swa_v, state_gdn, state_gdn_conv, ffn1_w_in, ffn1_w_out, ln1_g, ln1_b, w_in, rwkv_mu, rwkv_w0, rwkv_w_up, rwkv_a0, rwkv_a_up, rwkv_g_up, rwkv_k_k, rwkv_k_a, rwkv_r_k, rwkv_gn_g, rwkv_gn_b, swa_sinks, rel_table, gdn_conv_w, gdn_a_log, gdn_dt_bias, gdn_norm_g, w_out, ln2_g, ln2_b, ffn2_w_in, ffn2_w_out, ln3_g, ln3_b):
    depth = ffn1_w_in.shape[0]
    alpha = (2 * depth) ** 0.25
    layers = [_prep_layer(l, ffn1_w_in, ffn1_w_out, ln1_g, ln1_b, w_in, rwkv_mu, rwkv_w0, rwkv_w_up, rwkv_a0,
                          rwkv_a_up, rwkv_g_up, rwkv_k_k, rwkv_k_a, rwkv_r_k, rwkv_gn_g, rwkv_gn_b, swa_sinks,
                          gdn_conv_w, gdn_a_log, gdn_dt_bias, gdn_norm_g, w_out, ln2_g, ln2_b,
                          ffn2_w_in, ffn2_w_out, ln3_g, ln3_b) for l in range(depth)]
    bp, tp_len, d = x_prompt.shape
    bs, ts, _ = x_sample.shape
    ts_pad = -(-ts // SUBLANES) * SUBLANES

    qi = np.arange(ATTN_BLOCK)[:, None]
    bias_p = _rel_bias(rel_table, ATTN_BLOCK + qi - np.arange(2 * ATTN_BLOCK)[None, :])
    ti = (np.arange(S_GROUP * ts_pad) % ts_pad)[:, None]
    wc = cache_swa_k.shape[2]
    bias_c = _rel_bias(rel_table, wc + ti - np.arange(wc)[None, :])
    bias_n = _rel_bias(rel_table, ti - np.arange(ts_pad)[None, :])
    regroup = lambda a: jnp.stack([jnp.concatenate([a[kv * S_GROUP + g, g * ts_pad:(g + 1) * ts_pad]
                                                    for g in range(S_GROUP)], axis=0)
                                   for kv in range(S_KV_HEADS)], axis=0)
    biases = (bias_p, regroup(bias_c), regroup(bias_n))

    zeros = lambda *s: jnp.zeros((depth, bp) + s, F32)
    y_prompt, p_states = _run_trunk(
        x_prompt, zeros(N_HEADS, HEAD_DIM, HEAD_DIM), zeros(R_PROJ), None, None,
        zeros(N_HEADS, HEAD_DIM, HEAD_DIM), zeros(G_CONV - 1, G_QKV), layers, biases,
        prompt=True, tv=64, alpha=alpha, tm=256, n_chunks=2, bb=bp, c=64)

    xs = jnp.concatenate([x_sample, jnp.zeros((bs, ts_pad - ts, d), F32)], axis=1)
    ck = cache_swa_k.reshape(depth, bs, wc, S_KV_WIDTH)
    cv = cache_swa_v.reshape(depth, bs, wc, S_KV_WIDTH)
    y_sample, s_states = _run_trunk(
        xs, state_rwkv, state_rwkv_shift, ck, cv, state_gdn, state_gdn_conv, layers, biases,
        prompt=False, tv=ts, alpha=alpha, tm=256, n_chunks=2, bb=8, c=ts_pad)
    return (y_prompt, y_sample[:, :ts]) + tuple(p_states) + tuple(s_states)
```

```python
import functools
import math

import numpy as np
import jax
import jax.numpy as jnp
from jax import lax
from jax.experimental import pallas as pl
from jax.experimental.pallas import tpu as pltpu

F32 = jnp.float32
BF16 = jnp.bfloat16

HEAD_DIM = 64
N_HEADS = 4
R_WIDTH = N_HEADS * HEAD_DIM
R_PROJ = 896
S_Q_HEADS = 8
S_KV_HEADS = 2
S_GROUP = S_Q_HEADS // S_KV_HEADS
S_WIDTH = S_Q_HEADS * HEAD_DIM
S_KV_WIDTH = S_KV_HEADS * HEAD_DIM
S_PROJ = S_WIDTH + 2 * S_KV_WIDTH
G_WIDTH = N_HEADS * HEAD_DIM
G_CONV = 4
G_QKV = 3 * G_WIDTH
G_PROJ_PAD = 4 * G_WIDTH + 128
WINDOW = 128
ATTN_BLOCK = 128
REL_BUCKETS = 32
REL_MAX_EXACT = 16
REL_MAX_DIST = 128
NEG_INF = -1e30
R_GN_EPS = 64e-5
G_NORM_EPS = 1e-6
LN_EPS = 1e-5
L2_EPS = 1e-6
PROJ_PAD = R_PROJ + S_PROJ + G_PROJ_PAD

LANES = 128
SUBLANES = 8
VMEM_LIMIT_BYTES = 56 * 1024 * 1024

NN = ((1,), (0,))
NT = ((1,), (1,))
TN = ((0,), (0,))


def _dg(a, b, dims):
    return lax.dot_general(a, b, (dims, ((), ())), preferred_element_type=F32)


def _split2(a):
    hi = a.astype(BF16)
    lo = (a - hi.astype(F32)).astype(BF16)
    return hi, lo


def _mm_exact_lhs(a_bf16, b, dims=NN):
    b1 = b.astype(BF16)
    r1 = b - b1.astype(F32)
    b2 = r1.astype(BF16)
    b3 = (r1 - b2.astype(F32)).astype(BF16)
    return _dg(a_bf16, b1, dims) + (_dg(a_bf16, b2, dims) + _dg(a_bf16, b3, dims))


def _sigmoid(x):
    return 1.0 / (1.0 + jnp.exp(-x))


def _silu(x):
    return x * _sigmoid(x)


def _softplus(x):
    return jnp.maximum(x, 0.0) + jnp.log(1.0 + jnp.exp(-jnp.abs(x)))


def _layer_norm(z, g, b):
    mu = jnp.mean(z, axis=-1, keepdims=True)
    zc = z - mu
    var = jnp.mean(zc * zc, axis=-1, keepdims=True)
    return zc * lax.rsqrt(var + LN_EPS) * g + b


def _mm3_each(a_list, b_list, dims=NN):
    sa = [_split2(a) for a in a_list]
    sb = [_split2(b) for b in b_list]
    return [_dg(ah, bh, dims) + (_dg(ah, bl, dims) + _dg(al, bh, dims))
            for (ah, al), (bh, bl) in zip(sa, sb)]


def _mm1_each(a_list, b_list, dims=NN):
    return [_dg(a.astype(BF16), b.astype(BF16), dims) for a, b in zip(a_list, b_list)]


def _bf16_each(a_list):
    return [a.astype(BF16) for a in a_list]


def _tri_inverse_each(m_list, c):
    row = lax.broadcasted_iota(jnp.int32, (c, c), 0)
    col = lax.broadcasted_iota(jnp.int32, (c, c), 1)
    eye = jnp.where(row == col, 1.0, 0.0).astype(F32)
    t = [eye + m for m in m_list]
    mp = _bf16_each(m_list)
    span = 2
    while span < c:
        mp = _bf16_each(_mm1_each(mp, mp))
        t = [a + b for a, b in zip(t, _mm1_each(t, mp))]
        span *= 2
    return t


SEQ_UNROLL = 4


def _for_each_group(bb, one_group):
    unroll = min(SEQ_UNROLL, bb)
    if bb == unroll:
        one_group(list(range(unroll)))
        return

    def group(gi, carry):
        one_group([gi * unroll + u for u in range(unroll)])
        return carry

    lax.fori_loop(0, bb // unroll, group, 0)


def _full_spec(shape):
    nd = len(shape)
    return pl.BlockSpec(shape, lambda *_: (0,) * nd)


def _resident_spec(shape, index_map):
    return pl.BlockSpec(shape, index_map, pipeline_mode=pl.Buffered(1))


SUB_ROWS = 256


def _row_tiles(tm):
    n = max(tm // SUB_ROWS, 1)
    return [slice(i * (tm // n), (i + 1) * (tm // n)) for i in range(n)]


def _swiglu_each(xb_l, wi_ref, wo_ref, d_ff, n_chunks):
    cw = d_ff // n_chunks
    acc = [None] * len(xb_l)
    for j in range(n_chunks):
        gate = [_dg(xb, wi_ref[:, j * cw:(j + 1) * cw], NN) for xb in xb_l]
        up = [_dg(xb, wi_ref[:, d_ff + j * cw:d_ff + (j + 1) * cw], NN) for xb in xb_l]
        act = [(_silu(g) * u).astype(BF16) for g, u in zip(gate, up)]
        part = [_dg(a, wo_ref[j * cw:(j + 1) * cw, :], NN) for a in act]
        acc = [p if a is None else a + p for a, p in zip(acc, part)]
    return acc


def _ffn_proj_kernel(x_ref, wi_ref, wo_ref, g_ref, b_ref, win_ref,
                     x1_ref, pr_ref, ps_ref, pg_ref, *, alpha, d_ff, n_chunks, tm):
    tiles = _row_tiles(tm)
    x_l = [x_ref[r, :] for r in tiles]
    y_l = _swiglu_each([x.astype(BF16) for x in x_l], wi_ref, wo_ref, d_ff, n_chunks)
    x1_l = [_layer_norm(alpha * x + 0.5 * y, g_ref[...], b_ref[...]) for x, y in zip(x_l, y_l)]
    p_l = [_dg(x1.astype(BF16), win_ref[...], NN) for x1 in x1_l]
    for r, x1, p in zip(tiles, x1_l, p_l):
        x1_ref[r, :] = x1
        pr_ref[r, :] = p[:, :R_PROJ]
        ps_ref[r, :] = p[:, R_PROJ:R_PROJ + S_PROJ]
        pg_ref[r, :] = p[:, R_PROJ + S_PROJ:]


def _ffn_proj(x, wi, wo, g, b, win, *, alpha, tm, n_chunks):
    m, d = x.shape
    d_ff = wo.shape[0]
    kern = functools.partial(_ffn_proj_kernel, alpha=alpha, d_ff=d_ff, n_chunks=n_chunks, tm=tm)
    row = lambda w: pl.BlockSpec((tm, w), lambda i: (i, 0))
    const = lambda i: (0, 0)
    return pl.pallas_call(
        kern,
        grid=(m // tm,),
        in_specs=[row(d),
                  _resident_spec(wi.shape, const), _resident_spec(wo.shape, const),
                  _resident_spec(g.shape, const), _resident_spec(b.shape, const),
                  _resident_spec(win.shape, const)],
        out_specs=[row(d), row(R_PROJ), row(S_PROJ), row(G_PROJ_PAD)],
        out_shape=[jax.ShapeDtypeStruct((m, d), F32), jax.ShapeDtypeStruct((m, R_PROJ), F32),
                   jax.ShapeDtypeStruct((m, S_PROJ), F32), jax.ShapeDtypeStruct((m, G_PROJ_PAD), F32)],
        compiler_params=pltpu.CompilerParams(dimension_semantics=("arbitrary",),
                                             vmem_limit_bytes=VMEM_LIMIT_BYTES),
        name="ffn_proj",
    )(x, wi, wo, g, b, win)


def _out_ffn_kernel(x_ref, yr_ref, ys_ref, yg_ref, wout_ref, g2_ref, b2_ref, wi_ref, wo_ref, g3_ref, b3_ref,
                    o_ref, *, alpha, d_ff, n_chunks, tm):
    tiles = _row_tiles(tm)
    x_l = [x_ref[r, :] for r in tiles]
    mix_l = [(_dg(yr_ref[r, :].astype(BF16), wout_ref[0:R_WIDTH, :], NN)
              + _dg(ys_ref[r, :].astype(BF16), wout_ref[R_WIDTH:R_WIDTH + S_WIDTH, :], NN)
              + _dg(yg_ref[r, :].astype(BF16), wout_ref[R_WIDTH + S_WIDTH:, :], NN)) for r in tiles]
    x2_l = [_layer_norm(alpha * x + mix, g2_ref[...], b2_ref[...]) for x, mix in zip(x_l, mix_l)]
    y_l = _swiglu_each([x2.astype(BF16) for x2 in x2_l], wi_ref, wo_ref, d_ff, n_chunks)
    for r, x2, y in zip(tiles, x2_l, y_l):
        o_ref[r, :] = _layer_norm(alpha * x2 + 0.5 * y, g3_ref[...], b3_ref[...])


def _out_ffn(x, yr, ys, yg, wout, g2, b2, wi, wo, g3, b3, *, alpha, tm, n_chunks):
    m, d = x.shape
    d_ff = wo.shape[0]
    kern = functools.partial(_out_ffn_kernel, alpha=alpha, d_ff=d_ff, n_chunks=n_chunks, tm=tm)
    row = lambda w: pl.BlockSpec((tm, w), lambda i: (i, 0))
    const = lambda i: (0, 0)
    res = lambda a: _resident_spec(a.shape, const)
    return pl.pallas_call(
        kern,
        grid=(m // tm,),
        in_specs=[row(d), row(R_WIDTH), row(S_WIDTH), row(G_WIDTH),
                  res(wout), res(g2), res(b2), res(wi), res(wo), res(g3), res(b3)],
        out_specs=row(d),
        out_shape=jax.ShapeDtypeStruct((m, d), F32),
        compiler_params=pltpu.CompilerParams(dimension_semantics=("arbitrary",),
                                             vmem_limit_bytes=VMEM_LIMIT_BYTES),
        name="out_ffn",
    )(x, yr, ys, yg, wout, g2, b2, wi, wo, g3, b3)


def _rwkv_kernel(f_ref, shift_ref, s0_ref, mu_ref, w0_ref, wup_ref, a0_ref, aup_ref, gup_ref,
                 kk_ref, ka_ref, rk_ref, gng_ref, gnb_ref,
                 y_ref, sout_ref, shout_ref, s_scr, prev_scr, *, bb, c, tv, nc):
    ci = pl.program_id(1)

    @pl.when(ci == 0)
    def _():
        s_scr[...] = s0_ref[...]
        prev_scr[...] = shift_ref[...]

    row = lax.broadcasted_iota(jnp.int32, (c, 1), 0)
    ri = lax.broadcasted_iota(jnp.int32, (c, 2 * c), 0)
    cj = lax.broadcasted_iota(jnp.int32, (c, 2 * c), 1)
    cj = jnp.where(cj >= c, cj - c, cj)
    strict = ri > cj
    incl = ri >= cj
    tri = (lax.broadcasted_iota(jnp.int32, (c, c), 0) >= lax.broadcasted_iota(jnp.int32, (c, c), 1))
    tri = jnp.where(tri, 1.0, 0.0).astype(BF16)
    valid = row < tv

    def one_group(seqs):
        n_seq = len(seqs)
        heads = [(n, h) for n in range(n_seq) for h in range(N_HEADS)]
        hs = lambda arr, h: arr[:, h * HEAD_DIM:(h + 1) * HEAD_DIM]
        fl = [f_ref[i] for i in seqs]
        prevs = [jnp.where(row == 0, prev_scr[i], pltpu.roll(f, 1, 0)) for i, f in zip(seqs, fl)]
        for i, f in zip(seqs, fl):
            prev_scr[i] = f[c - 1:c, :]
        fsl = [f + (p - f) * mu_ref[...] for f, p in zip(fl, prevs)]
        w_l = _mm3_each([jnp.tanh(fs[:, 768:800]) for fs in fsl], [wup_ref[...]] * n_seq)
        lr_l = _mm3_each([fs[:, 800:832] for fs in fsl], [aup_ref[...]] * n_seq)
        gate_l = _mm3_each([_sigmoid(fs[:, 832:896]) for fs in fsl], [gup_ref[...]] * n_seq)
        lr_l = [_sigmoid(a0_ref[...] + x) for x in lr_l]
        lw_l = [-jnp.exp(-_softplus(-(w0_ref[...] + x)) - 0.5) for x in w_l]
        r_l = [fs[:, 0:R_WIDTH] for fs in fsl]
        k_l = [fs[:, R_WIDTH:2 * R_WIDTH] for fs in fsl]
        v_l = [fs[:, 2 * R_WIDTH:3 * R_WIDTH] for fs in fsl]
        kkx_l = [k * kk_ref[...] for k in k_l]
        k2_l = [k * (1.0 + (lr - 1.0) * ka_ref[...]) for k, lr in zip(k_l, lr_l)]
        if tv < c:
            lw_l = [jnp.where(valid, x, 0.0) for x in lw_l]
            kkx_l = [jnp.where(valid, x, 0.0) for x in kkx_l]
            k2_l = [jnp.where(valid, x, 0.0) for x in k2_l]
        cum_l = [_mm_exact_lhs(tri, lw) for lw in lw_l]
        p_in = [jnp.exp(x) for x in cum_l]
        p_prev = [jnp.exp(x - lw) for x, lw in zip(cum_l, lw_l)]
        p_inv = [jnp.exp(-x) for x in cum_l]
        xs, y2s, vhs, ss = [], [], [], []
        for n, h in heads:
            kx = hs(kkx_l[n], h)
            kkh = kx * lax.rsqrt(jnp.sum(kx * kx, axis=-1, keepdims=True) + L2_EPS)
            at = -kkh * hs(p_prev[n], h)
            bt = kkh * hs(lr_l[n], h) * hs(p_inv[n], h)
            kt = hs(k2_l[n], h) * hs(p_inv[n], h)
            qt = hs(r_l[n], h) * hs(p_in[n], h)
            xs.append(jnp.concatenate([at, qt], axis=0))
            y2s.append(jnp.concatenate([bt, kt], axis=0))
            vhs.append(hs(v_l[n], h))
            ss.append(s_scr[seqs[n], h])
        xs = _bf16_each(xs)
        y2s = _bf16_each(y2s)
        gram = _mm1_each(xs, y2s, NT)
        xst = _mm1_each(xs, ss, NT)
        xa = [jnp.where(strict, g[:c], 0.0) for g in gram]
        xq = [jnp.where(incl, g[c:], 0.0) for g in gram]
        lakv = _mm1_each([x[:, c:] for x in xa], vhs)
        t = _tri_inverse_each([x[:, :c] for x in xa], c)
        u = _mm1_each(t, [a[:c] + b for a, b in zip(xst, lakv)])
        uv = _bf16_each([jnp.concatenate([a, b], axis=0) for a, b in zip(u, vhs)])
        o_l = [a[c:] + b for a, b in zip(xst, _mm1_each(xq, uv))]
        ds = _mm1_each(uv, y2s, TN)
        for (n, h), s, d, o, vh in zip(heads, ss, ds, o_l, vhs):
            sl = slice(h * HEAD_DIM, (h + 1) * HEAD_DIM)
            s_scr[seqs[n], h] = (s + d) * p_in[n][c - 1:c, sl]
            mean = jnp.mean(o, axis=-1, keepdims=True)
            oc = o - mean
            var = jnp.mean(oc * oc, axis=-1, keepdims=True)
            on = oc * lax.rsqrt(var + R_GN_EPS) * gng_ref[:, sl] + gnb_ref[:, sl]
            bonus = jnp.sum(r_l[n][:, sl] * k2_l[n][:, sl] * rk_ref[:, sl], axis=-1, keepdims=True) * vh
            y_ref[seqs[n], :, sl] = (on + bonus) * gate_l[n][:, sl]

        @pl.when(ci == nc - 1)
        def _():
            for i, f in zip(seqs, fl):
                shout_ref[i] = f[tv - 1:tv, :]

    _for_each_group(bb, one_group)

    @pl.when(ci == nc - 1)
    def _():
        sout_ref[...] = s_scr[...]


def _rwkv_mixer(f, shift_prev, s0, params, *, bb, c, tv):
    b, t, _ = f.shape
    nc = t // c
    kern = functools.partial(_rwkv_kernel, bb=bb, c=c, tv=tv, nc=nc)
    shift3 = shift_prev.reshape(b, 1, R_PROJ)
    y, s_out, sh_out = pl.pallas_call(
        kern,
        grid=(b // bb, nc),
        in_specs=[pl.BlockSpec((bb, c, R_PROJ), lambda i, j: (i, j, 0)),
                  pl.BlockSpec((bb, 1, R_PROJ), lambda i, j: (i, 0, 0)),
                  pl.BlockSpec((bb, N_HEADS, HEAD_DIM, HEAD_DIM), lambda i, j: (i, 0, 0, 0))]
                 + [_full_spec(p.shape) for p in params],
        out_specs=[pl.BlockSpec((bb, c, R_WIDTH), lambda i, j: (i, j, 0)),
                   pl.BlockSpec((bb, N_HEADS, HEAD_DIM, HEAD_DIM), lambda i, j: (i, 0, 0, 0)),
                   pl.BlockSpec((bb, 1, R_PROJ), lambda i, j: (i, 0, 0))],
        out_shape=[jax.ShapeDtypeStruct((b, t, R_WIDTH), F32),
                   jax.ShapeDtypeStruct((b, N_HEADS, HEAD_DIM, HEAD_DIM), F32),
                   jax.ShapeDtypeStruct((b, 1, R_PROJ), F32)],
        scratch_shapes=[pltpu.VMEM((bb, N_HEADS, HEAD_DIM, HEAD_DIM), F32),
                        pltpu.VMEM((bb, 1, R_PROJ), F32)],
        compiler_params=pltpu.CompilerParams(dimension_semantics=("arbitrary", "arbitrary"),
                                             vmem_limit_bytes=VMEM_LIMIT_BYTES),
        name="rwkv7",
    )(f, shift3, s0, *params)
    return y, s_out, sh_out.reshape(b, R_PROJ)


def _gdn_kernel(x_ref, z_ref, gb_ref, cb_ref, s0_ref, cw_ref, alog_ref, dtb_ref, ng_ref,
                y_ref, sout_ref, cbout_ref, s_scr, tail_scr, *, bb, c, tv, nc):
    ci = pl.program_id(1)

    @pl.when(ci == 0)
    def _():
        s_scr[...] = s0_ref[...]
        tail_scr[...] = jnp.zeros_like(tail_scr)
        tail_scr[:, SUBLANES - (G_CONV - 1):SUBLANES, :] = cb_ref[...]

    row = lax.broadcasted_iota(jnp.int32, (c, 1), 0)
    ri = lax.broadcasted_iota(jnp.int32, (c, c), 0)
    cj = lax.broadcasted_iota(jnp.int32, (c, c), 1)
    strict = ri > cj
    incl = ri >= cj
    tri = jnp.where(incl, 1.0, 0.0).astype(BF16)
    valid = row < tv

    def one_group(seqs):
        n_seq = len(seqs)
        heads = [(n, h) for n in range(n_seq) for h in range(N_HEADS)]
        x_l = [x_ref[i] for i in seqs]
        act_l = []
        for i, x in zip(seqs, x_l):
            xe = jnp.concatenate([tail_scr[i], x], axis=0)
            conv = x * cw_ref[G_CONV - 1:G_CONV, :]
            for s in range(1, G_CONV):
                conv = conv + pltpu.roll(xe, s, 0)[SUBLANES:, :] * cw_ref[G_CONV - 1 - s:G_CONV - s, :]
            tail_scr[i] = x[c - SUBLANES:, :]
            act_l.append(_silu(conv))
        gb_l = [gb_ref[i] for i in seqs]
        beta_l = [_sigmoid(g) for g in gb_l]
        g_l = [-jnp.exp(alog_ref[...]) * _softplus(g + dtb_ref[...]) for g in gb_l]
        if tv < c:
            beta_l = [jnp.where(valid, x, 0.0) for x in beta_l]
            g_l = [jnp.where(valid, x, 0.0) for x in g_l]
        gcum_l = [_mm_exact_lhs(tri, g) for g in g_l]
        pad = lambda g: jnp.concatenate([g, jnp.zeros((LANES - c, LANES), F32)], axis=0) if c < LANES else g
        gcum_t = [pad(g).T for g in gcum_l]
        qhs, khs, vbs, kbs, egs, decs, deci, kdec, elast, ss = [], [], [], [], [], [], [], [], [], []
        for n, h in heads:
            act = act_l[n]
            qx = act[:, h * HEAD_DIM:(h + 1) * HEAD_DIM]
            kx = act[:, G_WIDTH + h * HEAD_DIM:G_WIDTH + (h + 1) * HEAD_DIM]
            vh = act[:, 2 * G_WIDTH + h * HEAD_DIM:2 * G_WIDTH + (h + 1) * HEAD_DIM]
            qh = qx * lax.rsqrt(jnp.sum(qx * qx, axis=-1, keepdims=True) + L2_EPS) * (HEAD_DIM ** -0.5)
            kh = kx * lax.rsqrt(jnp.sum(kx * kx, axis=-1, keepdims=True) + L2_EPS)
            if tv < c:
                kh = jnp.where(valid, kh, 0.0)
            beta = beta_l[n][:, h:h + 1]
            gcol = gcum_l[n][:, N_HEADS + h:N_HEADS + h + 1]
            grow = gcum_t[n][N_HEADS + h:N_HEADS + h + 1, 0:c]
            glast = gcum_l[n][c - 1:c, N_HEADS + h:N_HEADS + h + 1]
            diff = gcol - grow
            decs.append(jnp.exp(jnp.where(strict, diff, NEG_INF)))
            deci.append(jnp.exp(jnp.where(incl, diff, NEG_INF)))
            eg = jnp.exp(gcol)
            qhs.append(qh)
            khs.append(kh)
            kbs.append(kh * beta)
            vbs.append(vh * beta)
            egs.append(eg)
            kdec.append(kh * jnp.exp(glast - gcol))
            elast.append(jnp.exp(glast))
            ss.append(s_scr[seqs[n], h])
        gram = _mm1_each([jnp.concatenate([kb, qh], axis=0) for kb, qh in zip(kbs, qhs)], khs, NT)
        t = _tri_inverse_each([-(g[:c] * d) for g, d in zip(gram, decs)], c)
        sol = _mm3_each(t, [jnp.concatenate([vb, kb * eg], axis=1) for vb, kb, eg in zip(vbs, kbs, egs)])
        ws = _mm3_each([jnp.concatenate([so[:, HEAD_DIM:], qh * eg], axis=0)
                        for so, qh, eg in zip(sol, qhs, egs)], ss)
        u = [so[:, :HEAD_DIM] - w[:c] for so, w in zip(sol, ws)]
        qku = _mm1_each([g[c:] * d for g, d in zip(gram, deci)], u)
        ds = _mm3_each(kdec, u, TN)
        for (n, h), s, d, el, w, qu in zip(heads, ss, ds, elast, ws, qku):
            sl = slice(h * HEAD_DIM, (h + 1) * HEAD_DIM)
            s_scr[seqs[n], h] = el * s + d
            o = w[c:] + qu
            o = o * lax.rsqrt(jnp.mean(o * o, axis=-1, keepdims=True) + G_NORM_EPS) * ng_ref[...]
            y_ref[seqs[n], :, sl] = o * _silu(z_ref[seqs[n], :, sl])

        @pl.when(ci == nc - 1)
        def _():
            shift = (c - (tv - (G_CONV - 1))) % c
            for i, x in zip(seqs, x_l):
                xs = pltpu.roll(x, shift, 0) if shift else x
                cbout_ref[i] = xs[0:G_CONV - 1, :]

    _for_each_group(bb, one_group)

    @pl.when(ci == nc - 1)
    def _():
        sout_ref[...] = s_scr[...]


def _gdn_mixer(pg, conv_buf, s0, params, *, bb, c, tv):
    b, t, _ = pg.shape
    nc = t // c
    kern = functools.partial(_gdn_kernel, bb=bb, c=c, tv=tv, nc=nc)
    state_spec = pl.BlockSpec((bb, N_HEADS, HEAD_DIM, HEAD_DIM), lambda i, j: (i, 0, 0, 0))
    cb_spec = pl.BlockSpec((bb, G_CONV - 1, G_QKV), lambda i, j: (i, 0, 0))
    return pl.pallas_call(
        kern,
        grid=(b // bb, nc),
        in_specs=[pl.BlockSpec((bb, c, G_QKV), lambda i, j: (i, j, 0)),
                  pl.BlockSpec((bb, c, G_WIDTH), lambda i, j: (i, j, G_QKV // G_WIDTH)),
                  pl.BlockSpec((bb, c, LANES), lambda i, j: (i, j, (G_QKV + G_WIDTH) // LANES)),
                  cb_spec, state_spec] + [_full_spec(p.shape) for p in params],
        out_specs=[pl.BlockSpec((bb, c, G_WIDTH), lambda i, j: (i, j, 0)), state_spec, cb_spec],
        out_shape=[jax.ShapeDtypeStruct((b, t, G_WIDTH), F32),
                   jax.ShapeDtypeStruct((b, N_HEADS, HEAD_DIM, HEAD_DIM), F32),
                   jax.ShapeDtypeStruct((b, G_CONV - 1, G_QKV), F32)],
        scratch_shapes=[pltpu.VMEM((bb, N_HEADS, HEAD_DIM, HEAD_DIM), F32),
                        pltpu.VMEM((bb, SUBLANES, G_QKV), F32)],
        compiler_params=pltpu.CompilerParams(dimension_semantics=("arbitrary", "arbitrary"),
                                             vmem_limit_bytes=VMEM_LIMIT_BYTES),
        name="gdn",
    )(pg, pg, pg, conv_buf, s0, *params)


def _rel_buckets(dist):
    n = np.maximum(dist, 0)
    nf = np.maximum(n, 1).astype(np.float32)
    large = REL_MAX_EXACT + (np.log(nf / REL_MAX_EXACT) / math.log(REL_MAX_DIST / REL_MAX_EXACT)
                             * (REL_BUCKETS - REL_MAX_EXACT)).astype(np.int32)
    return np.where(n < REL_MAX_EXACT, n, np.minimum(large, REL_BUCKETS - 1)).astype(np.int32)


def _bias_kernel(table_ref, bucket_ref, o_ref):
    bucket = bucket_ref[...]
    for h in range(S_Q_HEADS):
        acc = jnp.zeros(bucket.shape, F32)
        for k in range(REL_BUCKETS):
            acc = jnp.where(bucket == k, table_ref[k, h], acc)
        o_ref[h] = acc


def _rel_bias(table, dist):
    bucket = jnp.asarray(_rel_buckets(dist))
    return pl.pallas_call(
        _bias_kernel,
        in_specs=[pl.BlockSpec(memory_space=pltpu.SMEM), _full_spec(bucket.shape)],
        out_specs=_full_spec((S_Q_HEADS,) + bucket.shape),
        out_shape=jax.ShapeDtypeStruct((S_Q_HEADS,) + bucket.shape, F32),
        grid=(1,),
        name="rel_bias",
    )(table, bucket)


def _swa_prompt_kernel(sink_ref, q_ref, kp_ref, kc_ref, vp_ref, vc_ref, bias_ref, o_ref):
    n = pl.program_id(1)
    qi = lax.broadcasted_iota(jnp.int32, (ATTN_BLOCK, 2 * ATTN_BLOCK), 0)
    kj = lax.broadcasted_iota(jnp.int32, (ATTN_BLOCK, 2 * ATTN_BLOCK), 1)
    dist = ATTN_BLOCK + qi - kj
    mask = (dist >= 0) & (dist < WINDOW) & ((kj >= ATTN_BLOCK) | (n > 0))
    q = q_ref[...]
    kband = jnp.concatenate([kp_ref[...], kc_ref[...]], axis=0).astype(BF16)
    vband = jnp.concatenate([vp_ref[...], vc_ref[...]], axis=0).astype(BF16)
    heads = range(S_Q_HEADS)
    kvs = lambda a, h: a[:, (h // S_GROUP) * HEAD_DIM:(h // S_GROUP + 1) * HEAD_DIM]
    qk = [_dg(q[:, h * HEAD_DIM:(h + 1) * HEAD_DIM].astype(BF16), kvs(kband, h), NT) for h in heads]
    logits = [jnp.where(mask, x * (HEAD_DIM ** -0.5) + bias_ref[h], NEG_INF) for x, h in zip(qk, heads)]
    m = [jnp.maximum(jnp.max(x, axis=-1, keepdims=True), sink_ref[h]) for x, h in zip(logits, heads)]
    p = [jnp.exp(x - mm) for x, mm in zip(logits, m)]
    probs = [x / (jnp.sum(x, axis=-1, keepdims=True) + jnp.exp(sink_ref[h] - mm)) for x, mm, h in zip(p, m, heads)]
    for x, h in zip(probs, heads):
        o_ref[:, h * HEAD_DIM:(h + 1) * HEAD_DIM] = _dg(x.astype(BF16), kvs(vband, h), NN)


def _swa_prompt(ps, sinks, bias):
    b, t, _ = ps.shape
    nb = t // ATTN_BLOCK
    kcol = S_WIDTH // S_KV_WIDTH
    prev = lambda i, j: jnp.maximum(j - 1, 0)
    blk = lambda w: (None, ATTN_BLOCK, w)
    return pl.pallas_call(
        _swa_prompt_kernel,
        grid=(b, nb),
        in_specs=[pl.BlockSpec(memory_space=pltpu.SMEM),
                  pl.BlockSpec(blk(S_WIDTH), lambda i, j: (i, j, 0)),
                  pl.BlockSpec(blk(S_KV_WIDTH), lambda i, j: (i, prev(i, j), kcol)),
                  pl.BlockSpec(blk(S_KV_WIDTH), lambda i, j: (i, j, kcol)),
                  pl.BlockSpec(blk(S_KV_WIDTH), lambda i, j: (i, prev(i, j), kcol + 1)),
                  pl.BlockSpec(blk(S_KV_WIDTH), lambda i, j: (i, j, kcol + 1)),
                  _full_spec(bias.shape)],
        out_specs=pl.BlockSpec(blk(S_WIDTH), lambda i, j: (i, j, 0)),
        out_shape=jax.ShapeDtypeStruct((b, t, S_WIDTH), F32),
        compiler_params=pltpu.CompilerParams(dimension_semantics=("arbitrary", "arbitrary"),
                                             vmem_limit_bytes=VMEM_LIMIT_BYTES),
        name="swa_prompt",
    )(sinks, ps, ps, ps, ps, ps, bias)


def _swa_sample_kernel(sink_ref, q_ref, kn_ref, vn_ref, ck_ref, cv_ref, bias_c_ref, bias_n_ref,
                       o_ref, cko_ref, cvo_ref, *, bb, tp, tv):
    wc = WINDOW
    step = lambda n: jnp.concatenate([lax.broadcasted_iota(jnp.int32, (tp, n), 0)] * S_GROUP, axis=0)
    ti_c = step(wc)
    kj_c = lax.broadcasted_iota(jnp.int32, (S_GROUP * tp, wc), 1)
    dist_c = wc + ti_c - kj_c
    mask_c = (dist_c >= 0) & (dist_c < WINDOW)
    ti_n = step(tp)
    kj_n = lax.broadcasted_iota(jnp.int32, (S_GROUP * tp, tp), 1)
    dist_n = ti_n - kj_n
    mask_n = (dist_n >= 0) & (dist_n < WINDOW) & (kj_n < tv)

    sinks = [jnp.concatenate([jnp.full((tp, 1), sink_ref[kv * S_GROUP + g], F32) for g in range(S_GROUP)], axis=0)
             for kv in range(S_KV_HEADS)]
    old_rows = lax.broadcasted_iota(jnp.int32, (wc, 1), 0) < wc - tv
    scale = HEAD_DIM ** -0.5

    def one_group(seqs):
        chains = [(n, kv) for n in range(len(seqs)) for kv in range(S_KV_HEADS)]
        kvs = lambda a, kv: a[:, kv * HEAD_DIM:(kv + 1) * HEAD_DIM]
        q_l = [q_ref[i] for i in seqs]
        kn_l = [kn_ref[i] for i in seqs]
        vn_l = [vn_ref[i] for i in seqs]
        ck_l = [ck_ref[i] for i in seqs]
        cv_l = [cv_ref[i] for i in seqs]
        qg = [jnp.concatenate([q_l[n][:, (kv * S_GROUP + g) * HEAD_DIM:(kv * S_GROUP + g + 1) * HEAD_DIM]
                               for g in range(S_GROUP)], axis=0).astype(BF16) for n, kv in chains]
        lc = [_dg(x, kvs(ck_l[n], kv).astype(BF16), NT) for x, (n, kv) in zip(qg, chains)]
        ln = [_dg(x, kvs(kn_l[n], kv).astype(BF16), NT) for x, (n, kv) in zip(qg, chains)]
        lc = [jnp.where(mask_c, x * scale + bias_c_ref[kv], NEG_INF) for x, (n, kv) in zip(lc, chains)]
        ln = [jnp.where(mask_n, x * scale + bias_n_ref[kv], NEG_INF) for x, (n, kv) in zip(ln, chains)]
        m = [jnp.maximum(jnp.maximum(jnp.max(a, axis=-1, keepdims=True), jnp.max(b, axis=-1, keepdims=True)),
                         sinks[kv]) for a, b, (n, kv) in zip(lc, ln, chains)]
        pc = [jnp.exp(a - mm) for a, mm in zip(lc, m)]
        pn = [jnp.exp(a - mm) for a, mm in zip(ln, m)]
        den = [jnp.sum(a, axis=-1, keepdims=True) + jnp.sum(b, axis=-1, keepdims=True) + jnp.exp(sinks[kv] - mm)
               for a, b, mm, (n, kv) in zip(pc, pn, m, chains)]
        oc = [_dg((a / d).astype(BF16), kvs(cv_l[n], kv).astype(BF16), NN) for a, d, (n, kv) in zip(pc, den, chains)]
        on = [_dg((a / d).astype(BF16), kvs(vn_l[n], kv).astype(BF16), NN) for a, d, (n, kv) in zip(pn, den, chains)]
        for a, b, (n, kv) in zip(oc, on, chains):
            og = a + b
            for g in range(S_GROUP):
                hq = kv * S_GROUP + g
                o_ref[seqs[n], :, hq * HEAD_DIM:(hq + 1) * HEAD_DIM] = og[g * tp:(g + 1) * tp, :]
        zpad = jnp.zeros((wc - tp, S_KV_WIDTH), F32)
        for n, i in enumerate(seqs):
            cko_ref[i] = jnp.where(old_rows, pltpu.roll(ck_l[n], wc - tv, 0),
                                   pltpu.roll(jnp.concatenate([kn_l[n], zpad], axis=0), wc - tv, 0))
            cvo_ref[i] = jnp.where(old_rows, pltpu.roll(cv_l[n], wc - tv, 0),
                                   pltpu.roll(jnp.concatenate([vn_l[n], zpad], axis=0), wc - tv, 0))

    _for_each_group(bb, one_group)


def _swa_sample(ps, cache_k, cache_v, sinks, bias_c, bias_n, *, bb, tv):
    b, tp, _ = ps.shape
    kcol = S_WIDTH // S_KV_WIDTH
    kern = functools.partial(_swa_sample_kernel, bb=bb, tp=tp, tv=tv)
    cache_spec = pl.BlockSpec((bb, WINDOW, S_KV_WIDTH), lambda i: (i, 0, 0))
    return pl.pallas_call(
        kern,
        grid=(b // bb,),
        in_specs=[pl.BlockSpec(memory_space=pltpu.SMEM),
                  pl.BlockSpec((bb, tp, S_WIDTH), lambda i: (i, 0, 0)),
                  pl.BlockSpec((bb, tp, S_KV_WIDTH), lambda i: (i, 0, kcol)),
                  pl.BlockSpec((bb, tp, S_KV_WIDTH), lambda i: (i, 0, kcol + 1)),
                  cache_spec, cache_spec, _full_spec(bias_c.shape), _full_spec(bias_n.shape)],
        out_specs=[pl.BlockSpec((bb, tp, S_WIDTH), lambda i: (i, 0, 0)), cache_spec, cache_spec],
        out_shape=[jax.ShapeDtypeStruct((b, tp, S_WIDTH), F32),
                   jax.ShapeDtypeStruct(cache_k.shape, F32), jax.ShapeDtypeStruct(cache_v.shape, F32)],
        compiler_params=pltpu.CompilerParams(dimension_semantics=("arbitrary",),
                                             vmem_limit_bytes=VMEM_LIMIT_BYTES),
        name="swa_sample",
    )(sinks, ps, ps, ps, cache_k, cache_v, bias_c, bias_n)


def _prep_layer(l, ffn1_w_in, ffn1_w_out, ln1_g, ln1_b, w_in, rwkv_mu, rwkv_w0, rwkv_w_up, rwkv_a0,
                rwkv_a_up, rwkv_g_up, rwkv_k_k, rwkv_k_a, rwkv_r_k, rwkv_gn_g, rwkv_gn_b, swa_sinks,
                gdn_conv_w, gdn_a_log, gdn_dt_bias, gdn_norm_g, w_out, ln2_g, ln2_b,
                ffn2_w_in, ffn2_w_out, ln3_g, ln3_b):
    row = lambda a: a[l].reshape(1, -1)
    d = w_in.shape[1]
    n_beta = 4 * G_WIDTH
    win = jnp.concatenate([w_in[l], jnp.zeros((d, PROJ_PAD - w_in.shape[2]), F32)], axis=1).astype(BF16)
    lane_tile = lambda a: jnp.zeros((1, LANES), F32).at[0, N_HEADS:2 * N_HEADS].set(a[l])
    del n_beta
    return dict(
        ffn1=(ffn1_w_in[l].astype(BF16), ffn1_w_out[l].astype(BF16), row(ln1_g), row(ln1_b), win),
        rwkv=(row(rwkv_mu), row(rwkv_w0), rwkv_w_up[l], row(rwkv_a0), rwkv_a_up[l], rwkv_g_up[l],
              row(rwkv_k_k), row(rwkv_k_a), row(rwkv_r_k), row(rwkv_gn_g), row(rwkv_gn_b)),
        sinks=swa_sinks[l],
        gdn=(gdn_conv_w[l], lane_tile(gdn_a_log), lane_tile(gdn_dt_bias), row(gdn_norm_g)),
        out=(w_out[l].astype(BF16), row(ln2_g), row(ln2_b), ffn2_w_in[l].astype(BF16),
             ffn2_w_out[l].astype(BF16), row(ln3_g), row(ln3_b)),
    )


def _run_trunk(x, rwkv_s, rwkv_shift, swa_k, swa_v, gdn_s, gdn_conv, layers, biases, *, prompt, tv, alpha,
               tm, n_chunks, bb, c):
    b, t, d = x.shape
    xf = x.reshape(b * t, d)
    new = [[] for _ in range(6)]
    for l, lp in enumerate(layers):
        x1, pr, ps, pg = _ffn_proj(xf, *lp["ffn1"], alpha=alpha, tm=tm, n_chunks=n_chunks)
        pr = pr.reshape(b, t, R_PROJ)
        ps = ps.reshape(b, t, S_PROJ)
        pg = pg.reshape(b, t, G_PROJ_PAD)
        y_r, s_r, sh_r = _rwkv_mixer(pr, rwkv_shift[l], rwkv_s[l], lp["rwkv"], bb=bb, c=c, tv=tv)
        if prompt:
            y_s = _swa_prompt(ps, lp["sinks"], biases[0])
            kc = ps[:, t - WINDOW:, S_WIDTH:S_WIDTH + S_KV_WIDTH]
            vc = ps[:, t - WINDOW:, S_WIDTH + S_KV_WIDTH:]
        else:
            y_s, kc, vc = _swa_sample(ps, swa_k[l], swa_v[l], lp["sinks"], biases[1], biases[2], bb=bb, tv=tv)
        y_g, s_g, cb = _gdn_mixer(pg, gdn_conv[l], gdn_s[l], lp["gdn"], bb=bb, c=c, tv=tv)
        xf = _out_ffn(x1, y_r.reshape(b * t, R_WIDTH), y_s.reshape(b * t, S_WIDTH), y_g.reshape(b * t, G_WIDTH),
                      *lp["out"], alpha=alpha, tm=tm, n_chunks=n_chunks)
        for lst, a in zip(new, (s_r, sh_r, kc.reshape(b, WINDOW, S_KV_HEADS, HEAD_DIM),
                                vc.reshape(b, WINDOW, S_KV_HEADS, HEAD_DIM), s_g, cb)):
            lst.append(a)
    return xf.reshape(b, t, d), [jnp.stack(lst, axis=0) for lst in new]


def kernel(x_prompt, x_sample, state_rwkv, state_rwkv_shift, cache_swa_k, cache_swa_v, state_gdn, state_gdn_conv, ffn1_w_in, ffn1_w_out, ln1_g, ln1_b, w_in, rwkv_mu, rwkv_w0, rwkv_w_up, rwkv_a0, rwkv_a_up, rwkv_g_up, rwkv_k_k, rwkv_k_a, rwkv_r_k, rwkv_gn_g, rwkv_gn_b, swa_sinks, rel_table, gdn_conv_w, gdn_a_log, gdn_dt_bias, gdn_norm_g, w_out, ln2_g, ln2_b, ffn2_w_in, ffn2_w_out, ln3_g, ln3_b):
    depth = ffn1_w_in.shape[0]
    alpha = (2 * depth) ** 0.25
    layers = [_prep_layer(l, ffn1_w_in, ffn1_w_out, ln1_g, ln1_b, w_in, rwkv_mu, rwkv_w0, rwkv_w_up, rwkv_a0,
                          rwkv_a_up, rwkv_g_up, rwkv_k_k, rwkv_k_a, rwkv_r_k, rwkv_gn_g, rwkv_gn_b, swa_sinks,
                          gdn_conv_w, gdn_a_log, gdn_dt_bias, gdn_norm_g, w_out, ln2_g, ln2_b,
                          ffn2_w_in, ffn2_w_out, ln3_g, ln3_b) for l in range(depth)]
    bp, tp_len, d = x_prompt.shape
    bs, ts, _ = x_sample.shape
    ts_pad = -(-ts // SUBLANES) * SUBLANES

    qi = np.arange(ATTN_BLOCK)[:, None]
    bias_p = _rel_bias(rel_table, ATTN_BLOCK + qi - np.arange(2 * ATTN_BLOCK)[None, :])
    ti = (np.arange(S_GROUP * ts_pad) % ts_pad)[:, None]
    wc = cache_swa_k.shape[2]
    bias_c = _rel_bias(rel_table, wc + ti - np.arange(wc)[None, :])
    bias_n = _rel_bias(rel_table, ti - np.arange(ts_pad)[None, :])
    regroup = lambda a: jnp.stack([jnp.concatenate([a[kv * S_GROUP + g, g * ts_pad:(g + 1) * ts_pad]
                                                    for g in range(S_GROUP)], axis=0)
                                   for kv in range(S_KV_HEADS)], axis=0)
    biases = (bias_p, regroup(bias_c), regroup(bias_n))

    zeros = lambda *s: jnp.zeros((depth, bp) + s, F32)
    y_prompt, p_states = _run_trunk(
        x_prompt, zeros(N_HEADS, HEAD_DIM, HEAD_DIM), zeros(R_PROJ), None, None,
        zeros(N_HEADS, HEAD_DIM, HEAD_DIM), zeros(G_CONV - 1, G_QKV), layers, biases,
        prompt=True, tv=64, alpha=alpha, tm=512, n_chunks=2, bb=bp, c=64)

    xs = jnp.concatenate([x_sample, jnp.zeros((bs, ts_pad - ts, d), F32)], axis=1)
    ck = cache_swa_k.reshape(depth, bs, wc, S_KV_WIDTH)
    cv = cache_swa_v.reshape(depth, bs, wc, S_KV_WIDTH)
    y_sample, s_states = _run_trunk(
        xs, state_rwkv, state_rwkv_shift, ck, cv, state_gdn, state_gdn_conv, layers, biases,
        prompt=False, tv=ts, alpha=alpha, tm=512, n_chunks=2, bb=8, c=ts_pad)
    return (y_prompt, y_sample[:, :ts]) + tuple(p_states) + tuple(s_states)
```

```python
import functools
import math

import numpy as np
import jax
import jax.numpy as jnp
from jax import lax
from jax.experimental import pallas as pl
from jax.experimental.pallas import tpu as pltpu

F32 = jnp.float32
BF16 = jnp.bfloat16

HEAD_DIM = 64
N_HEADS = 4
R_WIDTH = N_HEADS * HEAD_DIM
R_PROJ = 896
S_Q_HEADS = 8
S_KV_HEADS = 2
S_GROUP = S_Q_HEADS // S_KV_HEADS
S_WIDTH = S_Q_HEADS * HEAD_DIM
S_KV_WIDTH = S_KV_HEADS * HEAD_DIM
S_PROJ = S_WIDTH + 2 * S_KV_WIDTH
G_WIDTH = N_HEADS * HEAD_DIM
G_CONV = 4
G_QKV = 3 * G_WIDTH
G_PROJ_PAD = 4 * G_WIDTH + 128
WINDOW = 128
ATTN_BLOCK = 128
REL_BUCKETS = 32
REL_MAX_EXACT = 16
REL_MAX_DIST = 128
NEG_INF = -1e30
R_GN_EPS = 64e-5
G_NORM_EPS = 1e-6
LN_EPS = 1e-5
L2_EPS = 1e-6
PROJ_PAD = R_PROJ + S_PROJ + G_PROJ_PAD

LANES = 128
SUBLANES = 8
VMEM_LIMIT_BYTES = 56 * 1024 * 1024

NN = ((1,), (0,))
NT = ((1,), (1,))
TN = ((0,), (0,))


def _dg(a, b, dims):
    return lax.dot_general(a, b, (dims, ((), ())), preferred_element_type=F32)


def _split2(a):
    hi = a.astype(BF16)
    lo = (a - hi.astype(F32)).astype(BF16)
    return hi, lo


def _mm_exact_lhs(a_bf16, b, dims=NN):
    b1 = b.astype(BF16)
    r1 = b - b1.astype(F32)
    b2 = r1.astype(BF16)
    b3 = (r1 - b2.astype(F32)).astype(BF16)
    return _dg(a_bf16, b1, dims) + (_dg(a_bf16, b2, dims) + _dg(a_bf16, b3, dims))


def _sigmoid(x):
    return 1.0 / (1.0 + jnp.exp(-x))


def _silu(x):
    return x * _sigmoid(x)


def _softplus(x):
    return jnp.maximum(x, 0.0) + jnp.log(1.0 + jnp.exp(-jnp.abs(x)))


def _layer_norm(z, g, b):
    mu = jnp.mean(z, axis=-1, keepdims=True)
    zc = z - mu
    var = jnp.mean(zc * zc, axis=-1, keepdims=True)
    return zc * lax.rsqrt(var + LN_EPS) * g + b


def _mm3_each(a_list, b_list, dims=NN):
    sa = [_split2(a) for a in a_list]
    sb = [_split2(b) for b in b_list]
    return [_dg(ah, bh, dims) + (_dg(ah, bl, dims) + _dg(al, bh, dims))
            for (ah, al), (bh, bl) in zip(sa, sb)]


def _mm1_each(a_list, b_list, dims=NN):
    return [_dg(a.astype(BF16), b.astype(BF16), dims) for a, b in zip(a_list, b_list)]


def _bf16_each(a_list):
    return [a.astype(BF16) for a in a_list]


def _tri_inverse_each(m_list, c):
    row = lax.broadcasted_iota(jnp.int32, (c, c), 0)
    col = lax.broadcasted_iota(jnp.int32, (c, c), 1)
    eye = jnp.where(row == col, 1.0, 0.0).astype(F32)
    t = [eye + m for m in m_list]
    mp = _bf16_each(m_list)
    span = 2
    while span < c:
        mp = _bf16_each(_mm1_each(mp, mp))
        t = [a + b for a, b in zip(t, _mm1_each(t, mp))]
        span *= 2
    return t


SEQ_UNROLL = 4


def _for_each_group(bb, one_group):
    unroll = min(SEQ_UNROLL, bb)
    if bb == unroll:
        one_group(list(range(unroll)))
        return

    def group(gi, carry):
        one_group([gi * unroll + u for u in range(unroll)])
        return carry

    lax.fori_loop(0, bb // unroll, group, 0)


def _full_spec(shape):
    nd = len(shape)
    return pl.BlockSpec(shape, lambda *_: (0,) * nd)


def _resident_spec(shape, index_map):
    return pl.BlockSpec(shape, index_map, pipeline_mode=pl.Buffered(1))


SUB_ROWS = 256


def _row_tiles(tm):
    n = max(tm // SUB_ROWS, 1)
    return [slice(i * (tm // n), (i + 1) * (tm // n)) for i in range(n)]


def _swiglu_each(xb_l, wi_ref, wo_ref, d_ff, n_chunks):
    cw = d_ff // n_chunks
    acc = [None] * len(xb_l)
    for j in range(n_chunks):
        gate = [_dg(xb, wi_ref[:, j * cw:(j + 1) * cw], NN) for xb in xb_l]
        up = [_dg(xb, wi_ref[:, d_ff + j * cw:d_ff + (j + 1) * cw], NN) for xb in xb_l]
        act = [(_silu(g) * u).astype(BF16) for g, u in zip(gate, up)]
        part = [_dg(a, wo_ref[j * cw:(j + 1) * cw, :], NN) for a in act]
        acc = [p if a is None else a + p for a, p in zip(acc, part)]
    return acc


def _ffn_proj_kernel(x_ref, wi_ref, wo_ref, g_ref, b_ref, win_ref,
                     x1_ref, pr_ref, ps_ref, pg_ref, *, alpha, d_ff, n_chunks, tm):
    tiles = _row_tiles(tm)
    x_l = [x_ref[r, :] for r in tiles]
    y_l = _swiglu_each([x.astype(BF16) for x in x_l], wi_ref, wo_ref, d_ff, n_chunks)
    x1_l = [_layer_norm(alpha * x + 0.5 * y, g_ref[...], b_ref[...]) for x, y in zip(x_l, y_l)]
    p_l = [_dg(x1.astype(BF16), win_ref[...], NN) for x1 in x1_l]
    for r, x1, p in zip(tiles, x1_l, p_l):
        x1_ref[r, :] = x1
        pr_ref[r, :] = p[:, :R_PROJ]
        ps_ref[r, :] = p[:, R_PROJ:R_PROJ + S_PROJ]
        pg_ref[r, :] = p[:, R_PROJ + S_PROJ:]


def _ffn_proj(x, wi, wo, g, b, win, *, alpha, tm, n_chunks):
    m, d = x.shape
    tm = min(tm, m)
    assert m % tm == 0
    d_ff = wo.shape[0]
    kern = functools.partial(_ffn_proj_kernel, alpha=alpha, d_ff=d_ff, n_chunks=n_chunks, tm=tm)
    row = lambda w: pl.BlockSpec((tm, w), lambda i: (i, 0))
    const = lambda i: (0, 0)
    return pl.pallas_call(
        kern,
        grid=(m // tm,),
        in_specs=[row(d),
                  _resident_spec(wi.shape, const), _resident_spec(wo.shape, const),
                  _resident_spec(g.shape, const), _resident_spec(b.shape, const),
                  _resident_spec(win.shape, const)],
        out_specs=[row(d), row(R_PROJ), row(S_PROJ), row(G_PROJ_PAD)],
        out_shape=[jax.ShapeDtypeStruct((m, d), F32), jax.ShapeDtypeStruct((m, R_PROJ), F32),
                   jax.ShapeDtypeStruct((m, S_PROJ), F32), jax.ShapeDtypeStruct((m, G_PROJ_PAD), F32)],
        compiler_params=pltpu.CompilerParams(dimension_semantics=("arbitrary",),
                                             vmem_limit_bytes=VMEM_LIMIT_BYTES),
        name="ffn_proj",
    )(x, wi, wo, g, b, win)


def _out_ffn_kernel(x_ref, yr_ref, ys_ref, yg_ref, wout_ref, g2_ref, b2_ref, wi_ref, wo_ref, g3_ref, b3_ref,
                    o_ref, *, alpha, d_ff, n_chunks, tm):
    tiles = _row_tiles(tm)
    x_l = [x_ref[r, :] for r in tiles]
    mix_l = [(_dg(yr_ref[r, :].astype(BF16), wout_ref[0:R_WIDTH, :], NN)
              + _dg(ys_ref[r, :].astype(BF16), wout_ref[R_WIDTH:R_WIDTH + S_WIDTH, :], NN)
              + _dg(yg_ref[r, :].astype(BF16), wout_ref[R_WIDTH + S_WIDTH:, :], NN)) for r in tiles]
    x2_l = [_layer_norm(alpha * x + mix, g2_ref[...], b2_ref[...]) for x, mix in zip(x_l, mix_l)]
    y_l = _swiglu_each([x2.astype(BF16) for x2 in x2_l], wi_ref, wo_ref, d_ff, n_chunks)
    for r, x2, y in zip(tiles, x2_l, y_l):
        o_ref[r, :] = _layer_norm(alpha * x2 + 0.5 * y, g3_ref[...], b3_ref[...])


def _out_ffn(x, yr, ys, yg, wout, g2, b2, wi, wo, g3, b3, *, alpha, tm, n_chunks):
    m, d = x.shape
    tm = min(tm, m)
    assert m % tm == 0
    d_ff = wo.shape[0]
    kern = functools.partial(_out_ffn_kernel, alpha=alpha, d_ff=d_ff, n_chunks=n_chunks, tm=tm)
    row = lambda w: pl.BlockSpec((tm, w), lambda i: (i, 0))
    const = lambda i: (0, 0)
    res = lambda a: _resident_spec(a.shape, const)
    return pl.pallas_call(
        kern,
        grid=(m // tm,),
        in_specs=[row(d), row(R_WIDTH), row(S_WIDTH), row(G_WIDTH),
                  res(wout), res(g2), res(b2), res(wi), res(wo), res(g3), res(b3)],
        out_specs=row(d),
        out_shape=jax.ShapeDtypeStruct((m, d), F32),
        compiler_params=pltpu.CompilerParams(dimension_semantics=("arbitrary",),
                                             vmem_limit_bytes=VMEM_LIMIT_BYTES),
        name="out_ffn",
    )(x, yr, ys, yg, wout, g2, b2, wi, wo, g3, b3)


def _rwkv_kernel(f_ref, shift_ref, s0_ref, mu_ref, w0_ref, wup_ref, a0_ref, aup_ref, gup_ref,
                 kk_ref, ka_ref, rk_ref, gng_ref, gnb_ref, s_acc_ref, sh_acc_ref,
                 y_ref, sout_ref, shout_ref, s_scr, prev_scr, *, bb, c, nsub, tv, nc):
    del s_acc_ref, sh_acc_ref
    ci = pl.program_id(1)

    @pl.when(ci == 0)
    def _():
        s_scr[...] = s0_ref[...]
        prev_scr[...] = shift_ref[...]

    row = lax.broadcasted_iota(jnp.int32, (c, 1), 0)
    ri = lax.broadcasted_iota(jnp.int32, (c, 2 * c), 0)
    cj = lax.broadcasted_iota(jnp.int32, (c, 2 * c), 1)
    cj = jnp.where(cj >= c, cj - c, cj)
    strict = ri > cj
    incl = ri >= cj
    tri = (lax.broadcasted_iota(jnp.int32, (c, c), 0) >= lax.broadcasted_iota(jnp.int32, (c, c), 1))
    tri = jnp.where(tri, 1.0, 0.0).astype(BF16)
    valid = row < tv

    rows = nsub * c
    first_row = lax.broadcasted_iota(jnp.int32, (rows, 1), 0) == 0

    def one_group(seqs):
        n_seq = len(seqs)
        heads = [(n, h) for n in range(n_seq) for h in range(N_HEADS)]
        hs = lambda arr, h: arr[:, h * HEAD_DIM:(h + 1) * HEAD_DIM]
        fl = [f_ref[i] for i in seqs]
        prevs = [jnp.where(first_row, prev_scr[i], pltpu.roll(f, 1, 0)) for i, f in zip(seqs, fl)]
        for i, f in zip(seqs, fl):
            prev_scr[i] = f[rows - 1:rows, :]

        fsl = [f + (p - f) * mu_ref[...] for f, p in zip(fl, prevs)]
        w_l = _mm3_each([jnp.tanh(fs[:, 768:800]) for fs in fsl], [wup_ref[...]] * n_seq)
        lr_l = _mm3_each([fs[:, 800:832] for fs in fsl], [aup_ref[...]] * n_seq)
        gate_l = _mm3_each([_sigmoid(fs[:, 832:896]) for fs in fsl], [gup_ref[...]] * n_seq)
        lr_l = [_sigmoid(a0_ref[...] + x) for x in lr_l]
        lw_l = [-jnp.exp(-_softplus(-(w0_ref[...] + x)) - 0.5) for x in w_l]
        r_l = [fs[:, 0:R_WIDTH] for fs in fsl]
        k_l = [fs[:, R_WIDTH:2 * R_WIDTH] for fs in fsl]
        v_l = [fs[:, 2 * R_WIDTH:3 * R_WIDTH] for fs in fsl]
        kkx_l = [k * kk_ref[...] for k in k_l]
        k2_l = [k * (1.0 + (lr - 1.0) * ka_ref[...]) for k, lr in zip(k_l, lr_l)]
        if tv < c:
            lw_l = [jnp.where(valid, x, 0.0) for x in lw_l]
            kkx_l = [jnp.where(valid, x, 0.0) for x in kkx_l]
            k2_l = [jnp.where(valid, x, 0.0) for x in k2_l]
        cum_l = [jnp.concatenate([_mm_exact_lhs(tri, lw[s * c:(s + 1) * c]) for s in range(nsub)], axis=0)
                 if nsub > 1 else _mm_exact_lhs(tri, lw) for lw in lw_l]
        p_in_l = [jnp.exp(x) for x in cum_l]
        p_prev_l = [jnp.exp(x - lw) for x, lw in zip(cum_l, lw_l)]
        p_inv_l = [jnp.exp(-x) for x in cum_l]

        def intra(s):
            hs = lambda arr, h: arr[s * c:(s + 1) * c, h * HEAD_DIM:(h + 1) * HEAD_DIM]
            xs, y2s, vhs, plast, rk2 = [], [], [], [], []
            for n, h in heads:
                kx = hs(kkx_l[n], h)
                kkh = kx * lax.rsqrt(jnp.sum(kx * kx, axis=-1, keepdims=True) + L2_EPS)
                at = -kkh * hs(p_prev_l[n], h)
                bt = kkh * hs(lr_l[n], h) * hs(p_inv_l[n], h)
                kt = hs(k2_l[n], h) * hs(p_inv_l[n], h)
                qt = hs(r_l[n], h) * hs(p_in_l[n], h)
                xs.append(jnp.concatenate([at, qt], axis=0))
                y2s.append(jnp.concatenate([bt, kt], axis=0))
                vhs.append(hs(v_l[n], h))
                plast.append(hs(p_in_l[n], h)[c - 1:c])
                rk2.append(hs(r_l[n], h) * hs(k2_l[n], h))
            xs = _bf16_each(xs)
            y2s = _bf16_each(y2s)
            gram = _mm1_each(xs, y2s, NT)
            xa = [jnp.where(strict, g[:c], 0.0) for g in gram]
            xq = [jnp.where(incl, g[c:], 0.0) for g in gram]
            lakv = _mm1_each([x[:, c:] for x in xa], vhs)
            t = _tri_inverse_each([x[:, :c] for x in xa], c)
            gate = [hs(gate_l[n], h) for n, h in heads]
            return dict(xs=xs, y2s=y2s, vhs=vhs, xq=xq, lakv=lakv, t=t, plast=plast, rk2=rk2, gate=gate)

        parts = [intra(s) for s in range(nsub)]
        ss = [s_scr[seqs[n], h] for n, h in heads]
        for s, p in enumerate(parts):
            xst = _mm1_each(p["xs"], ss, NT)
            u = _mm1_each(p["t"], [a[:c] + b for a, b in zip(xst, p["lakv"])])
            uv = _bf16_each([jnp.concatenate([a, b], axis=0) for a, b in zip(u, p["vhs"])])
            o_l = [a[c:] + b for a, b in zip(xst, _mm1_each(p["xq"], uv))]
            ds = _mm1_each(uv, p["y2s"], TN)
            ss = [(st + d) * pl_ for st, d, pl_ in zip(ss, ds, p["plast"])]
            for (n, h), o, vh, rk2, gate in zip(heads, o_l, p["vhs"], p["rk2"], p["gate"]):
                sl = slice(h * HEAD_DIM, (h + 1) * HEAD_DIM)
                mean = jnp.mean(o, axis=-1, keepdims=True)
                oc = o - mean
                var = jnp.mean(oc * oc, axis=-1, keepdims=True)
                on = oc * lax.rsqrt(var + R_GN_EPS) * gng_ref[:, sl] + gnb_ref[:, sl]
                bonus = jnp.sum(rk2 * rk_ref[:, sl], axis=-1, keepdims=True) * vh
                y_ref[seqs[n], s * c:(s + 1) * c, sl] = (on + bonus) * gate
        for (n, h), st in zip(heads, ss):
            s_scr[seqs[n], h] = st

        @pl.when(ci == nc - 1)
        def _():
            last = rows - c + tv
            for i, f in zip(seqs, fl):
                shout_ref[i] = f[last - 1:last, :]

    _for_each_group(bb, one_group)

    @pl.when(ci == nc - 1)
    def _():
        sout_ref[...] = s_scr[...]


def _layer_block(shape, l):
    nd = len(shape)
    return pl.BlockSpec((None,) + tuple(shape), lambda i, j=0: (l, i) + (0,) * (nd - 1))


_ANY_SPEC = pl.BlockSpec(memory_space=pl.ANY)


def _rwkv_mixer(f, shift_all, s_all, l, s_acc, sh_acc, params, *, bb, c, tv, nsub=1):
    b, t, _ = f.shape
    rows = nsub * c
    nc = t // rows
    assert nsub == 1 or tv == c
    kern = functools.partial(_rwkv_kernel, bb=bb, c=c, nsub=nsub, tv=tv, nc=nc)
    state_spec = _layer_block((bb, N_HEADS, HEAD_DIM, HEAD_DIM), l)
    shift_spec = _layer_block((bb, 1, R_PROJ), l)
    n_in = 3 + len(params)
    return pl.pallas_call(
        kern,
        grid=(b // bb, nc),
        in_specs=[pl.BlockSpec((bb, rows, R_PROJ), lambda i, j: (i, j, 0)), shift_spec, state_spec]
                 + [_full_spec(p.shape) for p in params] + [_ANY_SPEC, _ANY_SPEC],
        out_specs=[pl.BlockSpec((bb, rows, R_WIDTH), lambda i, j: (i, j, 0)), state_spec, shift_spec],
        out_shape=[jax.ShapeDtypeStruct((b, t, R_WIDTH), F32),
                   jax.ShapeDtypeStruct(s_acc.shape, F32), jax.ShapeDtypeStruct(sh_acc.shape, F32)],
        input_output_aliases={n_in: 1, n_in + 1: 2},
        scratch_shapes=[pltpu.VMEM((bb, N_HEADS, HEAD_DIM, HEAD_DIM), F32),
                        pltpu.VMEM((bb, 1, R_PROJ), F32)],
        compiler_params=pltpu.CompilerParams(dimension_semantics=("arbitrary", "arbitrary"),
                                             vmem_limit_bytes=VMEM_LIMIT_BYTES),
        name="rwkv7",
    )(f, shift_all, s_all, *params, s_acc, sh_acc)


def _gdn_kernel(x_ref, z_ref, gb_ref, cb_ref, s0_ref, cw_ref, alog_ref, dtb_ref, ng_ref, s_acc_ref, cb_acc_ref,
                y_ref, sout_ref, cbout_ref, s_scr, tail_scr, *, bb, c, nsub, tv, nc):
    del s_acc_ref, cb_acc_ref
    ci = pl.program_id(1)

    @pl.when(ci == 0)
    def _():
        s_scr[...] = s0_ref[...]
        tail_scr[...] = jnp.zeros_like(tail_scr)
        tail_scr[:, SUBLANES - (G_CONV - 1):SUBLANES, :] = cb_ref[...]

    row = lax.broadcasted_iota(jnp.int32, (c, 1), 0)
    ri = lax.broadcasted_iota(jnp.int32, (c, c), 0)
    cj = lax.broadcasted_iota(jnp.int32, (c, c), 1)
    strict = ri > cj
    incl = ri >= cj
    tri = jnp.where(incl, 1.0, 0.0).astype(BF16)
    valid = row < tv

    rows = nsub * c

    def one_group(seqs):
        n_seq = len(seqs)
        heads = [(n, h) for n in range(n_seq) for h in range(N_HEADS)]
        chains = [(n, s, h) for s in range(nsub) for n, h in heads]
        sub = lambda a, s: a[s * c:(s + 1) * c]
        x_l = [x_ref[i] for i in seqs]
        act_l = []
        for i, x in zip(seqs, x_l):
            xe = jnp.concatenate([tail_scr[i], x], axis=0)
            conv = x * cw_ref[G_CONV - 1:G_CONV, :]
            for s in range(1, G_CONV):
                conv = conv + pltpu.roll(xe, s, 0)[SUBLANES:, :] * cw_ref[G_CONV - 1 - s:G_CONV - s, :]
            tail_scr[i] = x[rows - SUBLANES:, :]
            act_l.append(_silu(conv))
        gb_l = [gb_ref[i] for i in seqs]
        beta_l = [_sigmoid(g) for g in gb_l]
        g_l = [-jnp.exp(alog_ref[...]) * _softplus(g + dtb_ref[...]) for g in gb_l]
        if tv < c:
            beta_l = [jnp.where(valid, x, 0.0) for x in beta_l]
            g_l = [jnp.where(valid, x, 0.0) for x in g_l]
        pad = lambda g: jnp.concatenate([g, jnp.zeros((LANES - c, LANES), F32)], axis=0) if c < LANES else g
        gcum = {(n, s): _mm_exact_lhs(tri, sub(g_l[n], s)) for s in range(nsub) for n in range(n_seq)}
        gcum_t = {k: pad(g).T for k, g in gcum.items()}
        qegs, khs, vbs, kbs, egs, decs, kdec, elast = [], [], [], [], [], [], [], []
        for n, s, h in chains:
            act = sub(act_l[n], s)
            qx = act[:, h * HEAD_DIM:(h + 1) * HEAD_DIM]
            kx = act[:, G_WIDTH + h * HEAD_DIM:G_WIDTH + (h + 1) * HEAD_DIM]
            vh = act[:, 2 * G_WIDTH + h * HEAD_DIM:2 * G_WIDTH + (h + 1) * HEAD_DIM]
            qh = qx * lax.rsqrt(jnp.sum(qx * qx, axis=-1, keepdims=True) + L2_EPS) * (HEAD_DIM ** -0.5)
            kh = kx * lax.rsqrt(jnp.sum(kx * kx, axis=-1, keepdims=True) + L2_EPS)
            if tv < c:
                kh = jnp.where(valid, kh, 0.0)
            beta = sub(beta_l[n], s)[:, h:h + 1]
            gc = gcum[n, s]
            gcol = gc[:, N_HEADS + h:N_HEADS + h + 1]
            grow = gcum_t[n, s][N_HEADS + h:N_HEADS + h + 1, 0:c]
            glast = gc[c - 1:c, N_HEADS + h:N_HEADS + h + 1]
            decs.append(jnp.exp(jnp.where(incl, gcol - grow, NEG_INF)))
            eg = jnp.exp(gcol)
            qegs.append((qh, eg))
            khs.append(kh)
            kbs.append(kh * beta)
            vbs.append(vh * beta)
            egs.append(eg)
            kdec.append(kh * jnp.exp(glast - gcol))
            elast.append(jnp.exp(glast))
        sol, qk = [], []
        for s in range(nsub):
            pick = lambda lst: lst[s * len(heads):(s + 1) * len(heads)]
            gram = _mm1_each([jnp.concatenate([kb, qh], axis=0) for kb, (qh, _) in zip(pick(kbs), pick(qegs))],
                             pick(khs), NT)
            t = _tri_inverse_each([-jnp.where(strict, g[:c] * d, 0.0) for g, d in zip(gram, pick(decs))], c)
            sol += _mm3_each(t, [jnp.concatenate([vb, kb * eg], axis=1)
                                 for vb, kb, eg in zip(pick(vbs), pick(kbs), pick(egs))])
            qk += [g[c:] * d for g, d in zip(gram, pick(decs))]
        wq = [jnp.concatenate([so[:, HEAD_DIM:], qh * eg], axis=0) for so, (qh, eg) in zip(sol, qegs)]
        ss = [s_scr[seqs[n], h] for n, h in heads]
        for s in range(nsub):
            pick = lambda lst: lst[s * len(heads):(s + 1) * len(heads)]
            ws = _mm3_each(pick(wq), ss)
            u = [so[:, :HEAD_DIM] - w[:c] for so, w in zip(pick(sol), ws)]
            qku = _mm1_each(pick(qk), u)
            ds = _mm3_each(pick(kdec), u, TN)
            ss = [el * st + d for el, st, d in zip(pick(elast), ss, ds)]
            for (n, h), w, qu in zip(heads, ws, qku):
                sl = slice(h * HEAD_DIM, (h + 1) * HEAD_DIM)
                o = w[c:] + qu
                o = o * lax.rsqrt(jnp.mean(o * o, axis=-1, keepdims=True) + G_NORM_EPS) * ng_ref[...]
                y_ref[seqs[n], s * c:(s + 1) * c, sl] = o * _silu(z_ref[seqs[n], s * c:(s + 1) * c, sl])
        for (n, h), st in zip(heads, ss):
            s_scr[seqs[n], h] = st

        @pl.when(ci == nc - 1)
        def _():
            shift = (rows - (rows - c + tv - (G_CONV - 1))) % rows
            for i, x in zip(seqs, x_l):
                xs = pltpu.roll(x, shift, 0) if shift else x
                cbout_ref[i] = xs[0:G_CONV - 1, :]

    _for_each_group(bb, one_group)

    @pl.when(ci == nc - 1)
    def _():
        sout_ref[...] = s_scr[...]


def _gdn_mixer(pg, cb_all, s_all, l, s_acc, cb_acc, params, *, bb, c, tv, nsub=1):
    b, t, _ = pg.shape
    rows = nsub * c
    nc = t // rows
    assert nsub == 1 or tv == c
    kern = functools.partial(_gdn_kernel, bb=bb, c=c, nsub=nsub, tv=tv, nc=nc)
    state_spec = _layer_block((bb, N_HEADS, HEAD_DIM, HEAD_DIM), l)
    cb_spec = _layer_block((bb, G_CONV - 1, G_QKV), l)
    n_in = 5 + len(params)
    return pl.pallas_call(
        kern,
        grid=(b // bb, nc),
        in_specs=[pl.BlockSpec((bb, rows, G_QKV), lambda i, j: (i, j, 0)),
                  pl.BlockSpec((bb, rows, G_WIDTH), lambda i, j: (i, j, G_QKV // G_WIDTH)),
                  pl.BlockSpec((bb, rows, LANES), lambda i, j: (i, j, (G_QKV + G_WIDTH) // LANES)),
                  cb_spec, state_spec] + [_full_spec(p.shape) for p in params] + [_ANY_SPEC, _ANY_SPEC],
        out_specs=[pl.BlockSpec((bb, rows, G_WIDTH), lambda i, j: (i, j, 0)), state_spec, cb_spec],
        out_shape=[jax.ShapeDtypeStruct((b, t, G_WIDTH), F32),
                   jax.ShapeDtypeStruct(s_acc.shape, F32), jax.ShapeDtypeStruct(cb_acc.shape, F32)],
        input_output_aliases={n_in: 1, n_in + 1: 2},
        scratch_shapes=[pltpu.VMEM((bb, N_HEADS, HEAD_DIM, HEAD_DIM), F32),
                        pltpu.VMEM((bb, SUBLANES, G_QKV), F32)],
        compiler_params=pltpu.CompilerParams(dimension_semantics=("arbitrary", "arbitrary"),
                                             vmem_limit_bytes=VMEM_LIMIT_BYTES),
        name="gdn",
    )(pg, pg, pg, cb_all, s_all, *params, s_acc, cb_acc)


def _rel_buckets(dist):
    n = np.maximum(dist, 0)
    nf = np.maximum(n, 1).astype(np.float32)
    large = REL_MAX_EXACT + (np.log(nf / REL_MAX_EXACT) / math.log(REL_MAX_DIST / REL_MAX_EXACT)
                             * (REL_BUCKETS - REL_MAX_EXACT)).astype(np.int32)
    return np.where(n < REL_MAX_EXACT, n, np.minimum(large, REL_BUCKETS - 1)).astype(np.int32)


def _bias_kernel(table_ref, bucket_ref, o_ref):
    bucket = bucket_ref[...]
    for h in range(S_Q_HEADS):
        acc = jnp.zeros(bucket.shape, F32)
        for k in range(REL_BUCKETS):
            acc = jnp.where(bucket == k, table_ref[k, h], acc)
        o_ref[h] = acc


def _rel_bias(table, dist):
    bucket = jnp.asarray(_rel_buckets(dist))
    return pl.pallas_call(
        _bias_kernel,
        in_specs=[pl.BlockSpec(memory_space=pltpu.SMEM), _full_spec(bucket.shape)],
        out_specs=_full_spec((S_Q_HEADS,) + bucket.shape),
        out_shape=jax.ShapeDtypeStruct((S_Q_HEADS,) + bucket.shape, F32),
        grid=(1,),
        name="rel_bias",
    )(table, bucket)


def _swa_prompt_kernel(sink_ref, q_ref, kp_ref, kc_ref, vp_ref, vc_ref, bias_ref, o_ref):
    n = pl.program_id(1)
    qi = lax.broadcasted_iota(jnp.int32, (ATTN_BLOCK, 2 * ATTN_BLOCK), 0)
    kj = lax.broadcasted_iota(jnp.int32, (ATTN_BLOCK, 2 * ATTN_BLOCK), 1)
    dist = ATTN_BLOCK + qi - kj
    mask = (dist >= 0) & (dist < WINDOW) & ((kj >= ATTN_BLOCK) | (n > 0))
    q = q_ref[...]
    kband = jnp.concatenate([kp_ref[...], kc_ref[...]], axis=0).astype(BF16)
    vband = jnp.concatenate([vp_ref[...], vc_ref[...]], axis=0).astype(BF16)
    heads = range(S_Q_HEADS)
    kvs = lambda a, h: a[:, (h // S_GROUP) * HEAD_DIM:(h // S_GROUP + 1) * HEAD_DIM]
    qk = [_dg(q[:, h * HEAD_DIM:(h + 1) * HEAD_DIM].astype(BF16), kvs(kband, h), NT) for h in heads]
    logits = [jnp.where(mask, x * (HEAD_DIM ** -0.5) + bias_ref[h], NEG_INF) for x, h in zip(qk, heads)]
    m = [jnp.maximum(jnp.max(x, axis=-1, keepdims=True), sink_ref[h]) for x, h in zip(logits, heads)]
    p = [jnp.exp(x - mm) for x, mm in zip(logits, m)]
    probs = [x / (jnp.sum(x, axis=-1, keepdims=True) + jnp.exp(sink_ref[h] - mm)) for x, mm, h in zip(p, m, heads)]
    for x, h in zip(probs, heads):
        o_ref[:, h * HEAD_DIM:(h + 1) * HEAD_DIM] = _dg(x.astype(BF16), kvs(vband, h), NN)


def _swa_prompt(ps, sinks, bias):
    b, t, _ = ps.shape
    nb = t // ATTN_BLOCK
    kcol = S_WIDTH // S_KV_WIDTH
    prev = lambda i, j: jnp.maximum(j - 1, 0)
    blk = lambda w: (None, ATTN_BLOCK, w)
    return pl.pallas_call(
        _swa_prompt_kernel,
        grid=(b, nb),
        in_specs=[pl.BlockSpec(memory_space=pltpu.SMEM),
                  pl.BlockSpec(blk(S_WIDTH), lambda i, j: (i, j, 0)),
                  pl.BlockSpec(blk(S_KV_WIDTH), lambda i, j: (i, prev(i, j), kcol)),
                  pl.BlockSpec(blk(S_KV_WIDTH), lambda i, j: (i, j, kcol)),
                  pl.BlockSpec(blk(S_KV_WIDTH), lambda i, j: (i, prev(i, j), kcol + 1)),
                  pl.BlockSpec(blk(S_KV_WIDTH), lambda i, j: (i, j, kcol + 1)),
                  _full_spec(bias.shape)],
        out_specs=pl.BlockSpec(blk(S_WIDTH), lambda i, j: (i, j, 0)),
        out_shape=jax.ShapeDtypeStruct((b, t, S_WIDTH), F32),
        compiler_params=pltpu.CompilerParams(dimension_semantics=("arbitrary", "arbitrary"),
                                             vmem_limit_bytes=VMEM_LIMIT_BYTES),
        name="swa_prompt",
    )(sinks, ps, ps, ps, ps, ps, bias)


def _swa_sample_kernel(sink_ref, q_ref, kn_ref, vn_ref, ck_ref, cv_ref, bias_c_ref, bias_n_ref,
                       k_acc_ref, v_acc_ref, o_ref, cko_ref, cvo_ref, *, bb, tp, tv):
    del k_acc_ref, v_acc_ref
    wc = WINDOW
    step = lambda n: jnp.concatenate([lax.broadcasted_iota(jnp.int32, (tp, n), 0)] * S_GROUP, axis=0)
    ti_c = step(wc)
    kj_c = lax.broadcasted_iota(jnp.int32, (S_GROUP * tp, wc), 1)
    dist_c = wc + ti_c - kj_c
    mask_c = (dist_c >= 0) & (dist_c < WINDOW)
    ti_n = step(tp)
    kj_n = lax.broadcasted_iota(jnp.int32, (S_GROUP * tp, tp), 1)
    dist_n = ti_n - kj_n
    mask_n = (dist_n >= 0) & (dist_n < WINDOW) & (kj_n < tv)

    sinks = [jnp.concatenate([jnp.full((tp, 1), sink_ref[kv * S_GROUP + g], F32) for g in range(S_GROUP)], axis=0)
             for kv in range(S_KV_HEADS)]
    old_rows = lax.broadcasted_iota(jnp.int32, (wc, 1), 0) < wc - tv
    scale = HEAD_DIM ** -0.5

    def one_group(seqs):
        chains = [(n, kv) for n in range(len(seqs)) for kv in range(S_KV_HEADS)]
        kvs = lambda a, kv: a[:, kv * HEAD_DIM:(kv + 1) * HEAD_DIM]
        q_l = [q_ref[i] for i in seqs]
        kn_l = [kn_ref[i] for i in seqs]
        vn_l = [vn_ref[i] for i in seqs]
        ck_l = [ck_ref[i] for i in seqs]
        cv_l = [cv_ref[i] for i in seqs]
        qg = [jnp.concatenate([q_l[n][:, (kv * S_GROUP + g) * HEAD_DIM:(kv * S_GROUP + g + 1) * HEAD_DIM]
                               for g in range(S_GROUP)], axis=0).astype(BF16) for n, kv in chains]
        lc = [_dg(x, kvs(ck_l[n], kv).astype(BF16), NT) for x, (n, kv) in zip(qg, chains)]
        ln = [_dg(x, kvs(kn_l[n], kv).astype(BF16), NT) for x, (n, kv) in zip(qg, chains)]
        lc = [jnp.where(mask_c, x * scale + bias_c_ref[kv], NEG_INF) for x, (n, kv) in zip(lc, chains)]
        ln = [jnp.where(mask_n, x * scale + bias_n_ref[kv], NEG_INF) for x, (n, kv) in zip(ln, chains)]
        m = [jnp.maximum(jnp.maximum(jnp.max(a, axis=-1, keepdims=True), jnp.max(b, axis=-1, keepdims=True)),
                         sinks[kv]) for a, b, (n, kv) in zip(lc, ln, chains)]
        pc = [jnp.exp(a - mm) for a, mm in zip(lc, m)]
        pn = [jnp.exp(a - mm) for a, mm in zip(ln, m)]
        den = [jnp.sum(a, axis=-1, keepdims=True) + jnp.sum(b, axis=-1, keepdims=True) + jnp.exp(sinks[kv] - mm)
               for a, b, mm, (n, kv) in zip(pc, pn, m, chains)]
        oc = [_dg((a / d).astype(BF16), kvs(cv_l[n], kv).astype(BF16), NN) for a, d, (n, kv) in zip(pc, den, chains)]
        on = [_dg((a / d).astype(BF16), kvs(vn_l[n], kv).astype(BF16), NN) for a, d, (n, kv) in zip(pn, den, chains)]
        for a, b, (n, kv) in zip(oc, on, chains):
            og = a + b
            for g in range(S_GROUP):
                hq = kv * S_GROUP + g
                o_ref[seqs[n], :, hq * HEAD_DIM:(hq + 1) * HEAD_DIM] = og[g * tp:(g + 1) * tp, :]
        zpad = jnp.zeros((wc - tp, S_KV_WIDTH), F32)
        for n, i in enumerate(seqs):
            cko_ref[i] = jnp.where(old_rows, pltpu.roll(ck_l[n], wc - tv, 0),
                                   pltpu.roll(jnp.concatenate([kn_l[n], zpad], axis=0), wc - tv, 0))
            cvo_ref[i] = jnp.where(old_rows, pltpu.roll(cv_l[n], wc - tv, 0),
                                   pltpu.roll(jnp.concatenate([vn_l[n], zpad], axis=0), wc - tv, 0))

    _for_each_group(bb, one_group)


def _swa_sample(ps, ck_all, cv_all, l, k_acc, v_acc, sinks, bias_c, bias_n, *, bb, tv):
    b, tp, _ = ps.shape
    kcol = S_WIDTH // S_KV_WIDTH
    kern = functools.partial(_swa_sample_kernel, bb=bb, tp=tp, tv=tv)
    cache_spec = _layer_block((bb, WINDOW, S_KV_WIDTH), l)
    return pl.pallas_call(
        kern,
        grid=(b // bb,),
        in_specs=[pl.BlockSpec(memory_space=pltpu.SMEM),
                  pl.BlockSpec((bb, tp, S_WIDTH), lambda i: (i, 0, 0)),
                  pl.BlockSpec((bb, tp, S_KV_WIDTH), lambda i: (i, 0, kcol)),
                  pl.BlockSpec((bb, tp, S_KV_WIDTH), lambda i: (i, 0, kcol + 1)),
                  cache_spec, cache_spec, _full_spec(bias_c.shape), _full_spec(bias_n.shape),
                  _ANY_SPEC, _ANY_SPEC],
        out_specs=[pl.BlockSpec((bb, tp, S_WIDTH), lambda i: (i, 0, 0)), cache_spec, cache_spec],
        out_shape=[jax.ShapeDtypeStruct((b, tp, S_WIDTH), F32),
                   jax.ShapeDtypeStruct(k_acc.shape, F32), jax.ShapeDtypeStruct(v_acc.shape, F32)],
        input_output_aliases={8: 1, 9: 2},
        compiler_params=pltpu.CompilerParams(dimension_semantics=("arbitrary",),
                                             vmem_limit_bytes=VMEM_LIMIT_BYTES),
        name="swa_sample",
    )(sinks, ps, ps, ps, ck_all, cv_all, bias_c, bias_n, k_acc, v_acc)


def _prep_layer(l, ffn1_w_in, ffn1_w_out, ln1_g, ln1_b, w_in, rwkv_mu, rwkv_w0, rwkv_w_up, rwkv_a0,
                rwkv_a_up, rwkv_g_up, rwkv_k_k, rwkv_k_a, rwkv_r_k, rwkv_gn_g, rwkv_gn_b, swa_sinks,
                gdn_conv_w, gdn_a_log, gdn_dt_bias, gdn_norm_g, w_out, ln2_g, ln2_b,
                ffn2_w_in, ffn2_w_out, ln3_g, ln3_b):
    row = lambda a: a[l].reshape(1, -1)
    d = w_in.shape[1]
    n_beta = 4 * G_WIDTH
    win = jnp.concatenate([w_in[l], jnp.zeros((d, PROJ_PAD - w_in.shape[2]), F32)], axis=1).astype(BF16)
    lane_tile = lambda a: jnp.zeros((1, LANES), F32).at[0, N_HEADS:2 * N_HEADS].set(a[l])
    del n_beta
    return dict(
        ffn1=(ffn1_w_in[l].astype(BF16), ffn1_w_out[l].astype(BF16), row(ln1_g), row(ln1_b), win),
        rwkv=(row(rwkv_mu), row(rwkv_w0), rwkv_w_up[l], row(rwkv_a0), rwkv_a_up[l], rwkv_g_up[l],
              row(rwkv_k_k), row(rwkv_k_a), row(rwkv_r_k), row(rwkv_gn_g), row(rwkv_gn_b)),
        sinks=swa_sinks[l],
        gdn=(gdn_conv_w[l], lane_tile(gdn_a_log), lane_tile(gdn_dt_bias), row(gdn_norm_g)),
        out=(w_out[l].astype(BF16), row(ln2_g), row(ln2_b), ffn2_w_in[l].astype(BF16),
             ffn2_w_out[l].astype(BF16), row(ln3_g), row(ln3_b)),
    )


def _run_trunk(x, rwkv_s, rwkv_shift, swa_k, swa_v, gdn_s, gdn_conv, layers, biases, *, prompt, tv, alpha,
               tm, n_chunks, bb, c, nsub):
    b, t, d = x.shape
    depth = len(layers)
    xf = x.reshape(b * t, d)
    rwkv_shift = rwkv_shift.reshape(depth, b, 1, R_PROJ)
    s_r = jnp.zeros_like(rwkv_s)
    sh_r = jnp.zeros_like(rwkv_shift)
    s_g = jnp.zeros_like(gdn_s)
    cb = jnp.zeros_like(gdn_conv)
    if prompt:
        kc, vc = [], []
    else:
        kc, vc = jnp.zeros_like(swa_k), jnp.zeros_like(swa_v)
    for l, lp in enumerate(layers):
        x1, pr, ps, pg = _ffn_proj(xf, *lp["ffn1"], alpha=alpha, tm=tm, n_chunks=n_chunks)
        pr = pr.reshape(b, t, R_PROJ)
        ps = ps.reshape(b, t, S_PROJ)
        pg = pg.reshape(b, t, G_PROJ_PAD)
        y_r, s_r, sh_r = _rwkv_mixer(pr, rwkv_shift, rwkv_s, l, s_r, sh_r, lp["rwkv"], bb=bb, c=c, tv=tv, nsub=nsub)
        if prompt:
            y_s = _swa_prompt(ps, lp["sinks"], biases[0])
            kc.append(ps[:, t - WINDOW:, S_WIDTH:S_WIDTH + S_KV_WIDTH])
            vc.append(ps[:, t - WINDOW:, S_WIDTH + S_KV_WIDTH:])
        else:
            y_s, kc, vc = _swa_sample(ps, swa_k, swa_v, l, kc, vc, lp["sinks"], biases[1], biases[2], bb=bb, tv=tv)
        y_g, s_g, cb = _gdn_mixer(pg, gdn_conv, gdn_s, l, s_g, cb, lp["gdn"], bb=bb, c=c, tv=tv, nsub=nsub)
        xf = _out_ffn(x1, y_r.reshape(b * t, R_WIDTH), y_s.reshape(b * t, S_WIDTH), y_g.reshape(b * t, G_WIDTH),
                      *lp["out"], alpha=alpha, tm=tm, n_chunks=n_chunks)
    if prompt:
        kc, vc = jnp.stack(kc, axis=0), jnp.stack(vc, axis=0)
    cache_shape = (depth, b, WINDOW, S_KV_HEADS, HEAD_DIM)
    return xf.reshape(b, t, d), [s_r, sh_r.reshape(depth, b, R_PROJ), kc.reshape(cache_shape),
                                 vc.reshape(cache_shape), s_g, cb]


def kernel(x_prompt, x_sample, state_rwkv, state_rwkv_shift, cache_swa_k, cache_swa_v, state_gdn, state_gdn_conv, ffn1_w_in, ffn1_w_out, ln1_g, ln1_b, w_in, rwkv_mu, rwkv_w0, rwkv_w_up, rwkv_a0, rwkv_a_up, rwkv_g_up, rwkv_k_k, rwkv_k_a, rwkv_r_k, rwkv_gn_g, rwkv_gn_b, swa_sinks, rel_table, gdn_conv_w, gdn_a_log, gdn_dt_bias, gdn_norm_g, w_out, ln2_g, ln2_b, ffn2_w_in, ffn2_w_out, ln3_g, ln3_b):
    depth = ffn1_w_in.shape[0]
    alpha = (2 * depth) ** 0.25
    layers = [_prep_layer(l, ffn1_w_in, ffn1_w_out, ln1_g, ln1_b, w_in, rwkv_mu, rwkv_w0, rwkv_w_up, rwkv_a0,
                          rwkv_a_up, rwkv_g_up, rwkv_k_k, rwkv_k_a, rwkv_r_k, rwkv_gn_g, rwkv_gn_b, swa_sinks,
                          gdn_conv_w, gdn_a_log, gdn_dt_bias, gdn_norm_g, w_out, ln2_g, ln2_b,
                          ffn2_w_in, ffn2_w_out, ln3_g, ln3_b) for l in range(depth)]
    bp, tp_len, d = x_prompt.shape
    bs, ts, _ = x_sample.shape
    ts_pad = -(-ts // SUBLANES) * SUBLANES

    qi = np.arange(ATTN_BLOCK)[:, None]
    bias_p = _rel_bias(rel_table, ATTN_BLOCK + qi - np.arange(2 * ATTN_BLOCK)[None, :])
    ti = (np.arange(S_GROUP * ts_pad) % ts_pad)[:, None]
    wc = cache_swa_k.shape[2]
    bias_c = _rel_bias(rel_table, wc + ti - np.arange(wc)[None, :])
    bias_n = _rel_bias(rel_table, ti - np.arange(ts_pad)[None, :])
    regroup = lambda a: jnp.stack([jnp.concatenate([a[kv * S_GROUP + g, g * ts_pad:(g + 1) * ts_pad]
                                                    for g in range(S_GROUP)], axis=0)
                                   for kv in range(S_KV_HEADS)], axis=0)
    biases = (bias_p, regroup(bias_c), regroup(bias_n))

    zeros = lambda *s: jnp.zeros((depth, bp) + s, F32)
    y_prompt, p_states = _run_trunk(
        x_prompt, zeros(N_HEADS, HEAD_DIM, HEAD_DIM), zeros(R_PROJ), None, None,
        zeros(N_HEADS, HEAD_DIM, HEAD_DIM), zeros(G_CONV - 1, G_QKV), layers, biases,
        prompt=True, tv=64, alpha=alpha, tm=512, n_chunks=2, bb=bp, c=64, nsub=2)

    xs = jnp.concatenate([x_sample, jnp.zeros((bs, ts_pad - ts, d), F32)], axis=1)
    ck = cache_swa_k.reshape(depth, bs, wc, S_KV_WIDTH)
    cv = cache_swa_v.reshape(depth, bs, wc, S_KV_WIDTH)
    y_sample, s_states = _run_trunk(
        xs, state_rwkv, state_rwkv_shift, ck, cv, state_gdn, state_gdn_conv, layers, biases,
        prompt=False, tv=ts, alpha=alpha, tm=512, n_chunks=2, bb=8, c=ts_pad, nsub=1)
    return (y_prompt, y_sample[:, :ts]) + tuple(p_states) + tuple(s_states)
```

```python
import functools
import math

import numpy as np
import jax
import jax.numpy as jnp
from jax import lax
from jax.experimental import pallas as pl
from jax.experimental.pallas import tpu as pltpu

F32 = jnp.float32
BF16 = jnp.bfloat16

HEAD_DIM = 64
N_HEADS = 4
R_WIDTH = N_HEADS * HEAD_DIM
R_PROJ = 896
S_Q_HEADS = 8
S_KV_HEADS = 2
S_GROUP = S_Q_HEADS // S_KV_HEADS
S_WIDTH = S_Q_HEADS * HEAD_DIM
S_KV_WIDTH = S_KV_HEADS * HEAD_DIM
S_PROJ = S_WIDTH + 2 * S_KV_WIDTH
G_WIDTH = N_HEADS * HEAD_DIM
G_CONV = 4
G_QKV = 3 * G_WIDTH
G_PROJ_PAD = 4 * G_WIDTH + 128
WINDOW = 128
ATTN_BLOCK = 128
REL_BUCKETS = 32
REL_MAX_EXACT = 16
REL_MAX_DIST = 128
NEG_INF = -1e30
R_GN_EPS = 64e-5
G_NORM_EPS = 1e-6
LN_EPS = 1e-5
L2_EPS = 1e-6
PROJ_PAD = R_PROJ + S_PROJ + G_PROJ_PAD

LANES = 128
SUBLANES = 8
VMEM_LIMIT_BYTES = 56 * 1024 * 1024

NN = ((1,), (0,))
NT = ((1,), (1,))
TN = ((0,), (0,))


def _dg(a, b, dims):
    return lax.dot_general(a, b, (dims, ((), ())), preferred_element_type=F32)


def _split2(a):
    hi = a.astype(BF16)
    lo = (a - hi.astype(F32)).astype(BF16)
    return hi, lo


def _mm_exact_lhs(a_bf16, b, dims=NN):
    b1 = b.astype(BF16)
    r1 = b - b1.astype(F32)
    b2 = r1.astype(BF16)
    b3 = (r1 - b2.astype(F32)).astype(BF16)
    return _dg(a_bf16, b1, dims) + (_dg(a_bf16, b2, dims) + _dg(a_bf16, b3, dims))


def _sigmoid(x):
    return 1.0 / (1.0 + jnp.exp(-x))


def _silu(x):
    return x * _sigmoid(x)


def _softplus(x):
    return jnp.maximum(x, 0.0) + jnp.log(1.0 + jnp.exp(-jnp.abs(x)))


def _layer_norm(z, g, b):
    mu = jnp.mean(z, axis=-1, keepdims=True)
    zc = z - mu
    var = jnp.mean(zc * zc, axis=-1, keepdims=True)
    return zc * lax.rsqrt(var + LN_EPS) * g + b


def _mm3_each(a_list, b_list, dims=NN):
    sa = [_split2(a) for a in a_list]
    sb = [_split2(b) for b in b_list]
    return [_dg(ah, bh, dims) + (_dg(ah, bl, dims) + _dg(al, bh, dims))
            for (ah, al), (bh, bl) in zip(sa, sb)]


def _mm1_each(a_list, b_list, dims=NN):
    return [_dg(a.astype(BF16), b.astype(BF16), dims) for a, b in zip(a_list, b_list)]


def _bf16_each(a_list):
    return [a.astype(BF16) for a in a_list]


def _tri_inverse_each(m_list, c):
    row = lax.broadcasted_iota(jnp.int32, (c, c), 0)
    col = lax.broadcasted_iota(jnp.int32, (c, c), 1)
    eye = jnp.where(row == col, 1.0, 0.0).astype(F32)
    t = [eye + m for m in m_list]
    mp = _bf16_each(m_list)
    span = 2
    while span < c:
        mp = _bf16_each(_mm1_each(mp, mp))
        t = [a + b for a, b in zip(t, _mm1_each(t, mp))]
        span *= 2
    return t


SEQ_UNROLL = 8


def _for_each_group(bb, one_group):
    unroll = min(SEQ_UNROLL, bb)
    if bb == unroll:
        one_group(list(range(unroll)))
        return

    def group(gi, carry):
        one_group([gi * unroll + u for u in range(unroll)])
        return carry

    lax.fori_loop(0, bb // unroll, group, 0)


def _full_spec(shape):
    nd = len(shape)
    return pl.BlockSpec(shape, lambda *_: (0,) * nd)


def _resident_spec(shape, index_map):
    return pl.BlockSpec(shape, index_map, pipeline_mode=pl.Buffered(1))


SUB_ROWS = 256


def _row_tiles(tm):
    n = max(tm // SUB_ROWS, 1)
    return [slice(i * (tm // n), (i + 1) * (tm // n)) for i in range(n)]


def _swiglu_each(xb_l, wi_ref, wo_ref, d_ff, n_chunks):
    cw = d_ff // n_chunks
    acc = [None] * len(xb_l)
    for j in range(n_chunks):
        gate = [_dg(xb, wi_ref[:, j * cw:(j + 1) * cw], NN) for xb in xb_l]
        up = [_dg(xb, wi_ref[:, d_ff + j * cw:d_ff + (j + 1) * cw], NN) for xb in xb_l]
        act = [(_silu(g) * u).astype(BF16) for g, u in zip(gate, up)]
        part = [_dg(a, wo_ref[j * cw:(j + 1) * cw, :], NN) for a in act]
        acc = [p if a is None else a + p for a, p in zip(acc, part)]
    return acc


def _ffn_proj_kernel(x_ref, wi_ref, wo_ref, g_ref, b_ref, win_ref,
                     x1_ref, pr_ref, ps_ref, pg_ref, *, alpha, d_ff, n_chunks, tm):
    tiles = _row_tiles(tm)
    x_l = [x_ref[r, :] for r in tiles]
    y_l = _swiglu_each([x.astype(BF16) for x in x_l], wi_ref, wo_ref, d_ff, n_chunks)
    x1_l = [_layer_norm(alpha * x + 0.5 * y, g_ref[...], b_ref[...]) for x, y in zip(x_l, y_l)]
    p_l = [_dg(x1.astype(BF16), win_ref[...], NN) for x1 in x1_l]
    for r, x1, p in zip(tiles, x1_l, p_l):
        x1_ref[r, :] = x1
        pr_ref[r, :] = p[:, :R_PROJ]
        ps_ref[r, :] = p[:, R_PROJ:R_PROJ + S_PROJ]
        pg_ref[r, :] = p[:, R_PROJ + S_PROJ:]


def _ffn_proj(x, wi, wo, g, b, win, *, alpha, tm, n_chunks):
    m, d = x.shape
    tm = min(tm, m)
    assert m % tm == 0
    d_ff = wo.shape[0]
    kern = functools.partial(_ffn_proj_kernel, alpha=alpha, d_ff=d_ff, n_chunks=n_chunks, tm=tm)
    row = lambda w: pl.BlockSpec((tm, w), lambda i: (i, 0))
    const = lambda i: (0, 0)
    return pl.pallas_call(
        kern,
        grid=(m // tm,),
        in_specs=[row(d),
                  _resident_spec(wi.shape, const), _resident_spec(wo.shape, const),
                  _resident_spec(g.shape, const), _resident_spec(b.shape, const),
                  _resident_spec(win.shape, const)],
        out_specs=[row(d), row(R_PROJ), row(S_PROJ), row(G_PROJ_PAD)],
        out_shape=[jax.ShapeDtypeStruct((m, d), F32), jax.ShapeDtypeStruct((m, R_PROJ), F32),
                   jax.ShapeDtypeStruct((m, S_PROJ), F32), jax.ShapeDtypeStruct((m, G_PROJ_PAD), F32)],
        compiler_params=pltpu.CompilerParams(dimension_semantics=("arbitrary",),
                                             vmem_limit_bytes=VMEM_LIMIT_BYTES),
        name="ffn_proj",
    )(x, wi, wo, g, b, win)


def _out_ffn_kernel(x_ref, yr_ref, ys_ref, yg_ref, wout_ref, g2_ref, b2_ref, wi_ref, wo_ref, g3_ref, b3_ref,
                    o_ref, *, alpha, d_ff, n_chunks, tm):
    tiles = _row_tiles(tm)
    x_l = [x_ref[r, :] for r in tiles]
    mix_l = [(_dg(yr_ref[r, :].astype(BF16), wout_ref[0:R_WIDTH, :], NN)
              + _dg(ys_ref[r, :].astype(BF16), wout_ref[R_WIDTH:R_WIDTH + S_WIDTH, :], NN)
              + _dg(yg_ref[r, :].astype(BF16), wout_ref[R_WIDTH + S_WIDTH:, :], NN)) for r in tiles]
    x2_l = [_layer_norm(alpha * x + mix, g2_ref[...], b2_ref[...]) for x, mix in zip(x_l, mix_l)]
    y_l = _swiglu_each([x2.astype(BF16) for x2 in x2_l], wi_ref, wo_ref, d_ff, n_chunks)
    for r, x2, y in zip(tiles, x2_l, y_l):
        o_ref[r, :] = _layer_norm(alpha * x2 + 0.5 * y, g3_ref[...], b3_ref[...])


def _out_ffn(x, yr, ys, yg, wout, g2, b2, wi, wo, g3, b3, *, alpha, tm, n_chunks):
    m, d = x.shape
    tm = min(tm, m)
    assert m % tm == 0
    d_ff = wo.shape[0]
    kern = functools.partial(_out_ffn_kernel, alpha=alpha, d_ff=d_ff, n_chunks=n_chunks, tm=tm)
    row = lambda w: pl.BlockSpec((tm, w), lambda i: (i, 0))
    const = lambda i: (0, 0)
    res = lambda a: _resident_spec(a.shape, const)
    return pl.pallas_call(
        kern,
        grid=(m // tm,),
        in_specs=[row(d), row(R_WIDTH), row(S_WIDTH), row(G_WIDTH),
                  res(wout), res(g2), res(b2), res(wi), res(wo), res(g3), res(b3)],
        out_specs=row(d),
        out_shape=jax.ShapeDtypeStruct((m, d), F32),
        compiler_params=pltpu.CompilerParams(dimension_semantics=("arbitrary",),
                                             vmem_limit_bytes=VMEM_LIMIT_BYTES),
        name="out_ffn",
    )(x, yr, ys, yg, wout, g2, b2, wi, wo, g3, b3)


def _rwkv_kernel(f_ref, shift_ref, s0_ref, mu_ref, w0_ref, wup_ref, a0_ref, aup_ref, gup_ref,
                 kk_ref, ka_ref, rk_ref, gng_ref, gnb_ref, s_acc_ref, sh_acc_ref,
                 y_ref, sout_ref, shout_ref, s_scr, prev_scr, *, bb, c, nsub, tv, nc):
    del s_acc_ref, sh_acc_ref
    ci = pl.program_id(1)

    @pl.when(ci == 0)
    def _():
        s_scr[...] = s0_ref[...]
        prev_scr[...] = shift_ref[...]

    row = lax.broadcasted_iota(jnp.int32, (c, 1), 0)
    ri = lax.broadcasted_iota(jnp.int32, (c, 2 * c), 0)
    cj = lax.broadcasted_iota(jnp.int32, (c, 2 * c), 1)
    cj = jnp.where(cj >= c, cj - c, cj)
    strict = ri > cj
    incl = ri >= cj
    tri = (lax.broadcasted_iota(jnp.int32, (c, c), 0) >= lax.broadcasted_iota(jnp.int32, (c, c), 1))
    tri = jnp.where(tri, 1.0, 0.0).astype(BF16)
    valid = row < tv

    rows = nsub * c
    first_row = lax.broadcasted_iota(jnp.int32, (rows, 1), 0) == 0

    def one_group(seqs):
        n_seq = len(seqs)
        heads = [(n, h) for n in range(n_seq) for h in range(N_HEADS)]
        hs = lambda arr, h: arr[:, h * HEAD_DIM:(h + 1) * HEAD_DIM]
        fl = [f_ref[i] for i in seqs]
        prevs = [jnp.where(first_row, prev_scr[i], pltpu.roll(f, 1, 0)) for i, f in zip(seqs, fl)]
        for i, f in zip(seqs, fl):
            prev_scr[i] = f[rows - 1:rows, :]

        fsl = [f + (p - f) * mu_ref[...] for f, p in zip(fl, prevs)]
        w_l = _mm3_each([jnp.tanh(fs[:, 768:800]) for fs in fsl], [wup_ref[...]] * n_seq)
        lr_l = _mm3_each([fs[:, 800:832] for fs in fsl], [aup_ref[...]] * n_seq)
        gate_l = _mm3_each([_sigmoid(fs[:, 832:896]) for fs in fsl], [gup_ref[...]] * n_seq)
        lr_l = [_sigmoid(a0_ref[...] + x) for x in lr_l]
        lw_l = [-jnp.exp(-_softplus(-(w0_ref[...] + x)) - 0.5) for x in w_l]
        r_l = [fs[:, 0:R_WIDTH] for fs in fsl]
        k_l = [fs[:, R_WIDTH:2 * R_WIDTH] for fs in fsl]
        v_l = [fs[:, 2 * R_WIDTH:3 * R_WIDTH] for fs in fsl]
        kkx_l = [k * kk_ref[...] for k in k_l]
        k2_l = [k * (1.0 + (lr - 1.0) * ka_ref[...]) for k, lr in zip(k_l, lr_l)]
        if tv < c:
            lw_l = [jnp.where(valid, x, 0.0) for x in lw_l]
            kkx_l = [jnp.where(valid, x, 0.0) for x in kkx_l]
            k2_l = [jnp.where(valid, x, 0.0) for x in k2_l]
        cum_l = [jnp.concatenate([_mm_exact_lhs(tri, lw[s * c:(s + 1) * c]) for s in range(nsub)], axis=0)
                 if nsub > 1 else _mm_exact_lhs(tri, lw) for lw in lw_l]
        p_in_l = [jnp.exp(x) for x in cum_l]
        p_prev_l = [jnp.exp(x - lw) for x, lw in zip(cum_l, lw_l)]
        p_inv_l = [jnp.exp(-x) for x in cum_l]

        def intra(s):
            hs = lambda arr, h: arr[s * c:(s + 1) * c, h * HEAD_DIM:(h + 1) * HEAD_DIM]
            xs, y2s, vhs, plast, rk2 = [], [], [], [], []
            for n, h in heads:
                kx = hs(kkx_l[n], h)
                kkh = kx * lax.rsqrt(jnp.sum(kx * kx, axis=-1, keepdims=True) + L2_EPS)
                at = -kkh * hs(p_prev_l[n], h)
                bt = kkh * hs(lr_l[n], h) * hs(p_inv_l[n], h)
                kt = hs(k2_l[n], h) * hs(p_inv_l[n], h)
                qt = hs(r_l[n], h) * hs(p_in_l[n], h)
                xs.append(jnp.concatenate([at, qt], axis=0))
                y2s.append(jnp.concatenate([bt, kt], axis=0))
                vhs.append(hs(v_l[n], h))
                plast.append(hs(p_in_l[n], h)[c - 1:c])
                rk2.append(hs(r_l[n], h) * hs(k2_l[n], h))
            xs = _bf16_each(xs)
            y2s = _bf16_each(y2s)
            gram = _mm1_each(xs, y2s, NT)
            xa = [jnp.where(strict, g[:c], 0.0) for g in gram]
            xq = [jnp.where(incl, g[c:], 0.0) for g in gram]
            lakv = _mm1_each([x[:, c:] for x in xa], vhs)
            t = _tri_inverse_each([x[:, :c] for x in xa], c)
            gate = [hs(gate_l[n], h) for n, h in heads]
            return dict(xs=xs, y2s=y2s, vhs=vhs, xq=xq, lakv=lakv, t=t, plast=plast, rk2=rk2, gate=gate)

        parts = [intra(s) for s in range(nsub)]
        ss = [s_scr[seqs[n], h] for n, h in heads]
        for s, p in enumerate(parts):
            xst = _mm1_each(p["xs"], ss, NT)
            u = _mm1_each(p["t"], [a[:c] + b for a, b in zip(xst, p["lakv"])])
            uv = _bf16_each([jnp.concatenate([a, b], axis=0) for a, b in zip(u, p["vhs"])])
            o_l = [a[c:] + b for a, b in zip(xst, _mm1_each(p["xq"], uv))]
            ds = _mm1_each(uv, p["y2s"], TN)
            ss = [(st + d) * pl_ for st, d, pl_ in zip(ss, ds, p["plast"])]
            for (n, h), o, vh, rk2, gate in zip(heads, o_l, p["vhs"], p["rk2"], p["gate"]):
                sl = slice(h * HEAD_DIM, (h + 1) * HEAD_DIM)
                mean = jnp.mean(o, axis=-1, keepdims=True)
                oc = o - mean
                var = jnp.mean(oc * oc, axis=-1, keepdims=True)
                on = oc * lax.rsqrt(var + R_GN_EPS) * gng_ref[:, sl] + gnb_ref[:, sl]
                bonus = jnp.sum(rk2 * rk_ref[:, sl], axis=-1, keepdims=True) * vh
                y_ref[seqs[n], s * c:(s + 1) * c, sl] = (on + bonus) * gate
        for (n, h), st in zip(heads, ss):
            s_scr[seqs[n], h] = st

        @pl.when(ci == nc - 1)
        def _():
            last = rows - c + tv
            for i, f in zip(seqs, fl):
                shout_ref[i] = f[last - 1:last, :]

    _for_each_group(bb, one_group)

    @pl.when(ci == nc - 1)
    def _():
        sout_ref[...] = s_scr[...]


def _layer_block(shape, l):
    nd = len(shape)
    return pl.BlockSpec((None,) + tuple(shape), lambda i, j=0: (l, i) + (0,) * (nd - 1))


_ANY_SPEC = pl.BlockSpec(memory_space=pl.ANY)


def _stacked_outputs(kern, n_in, accs):
    if accs[0] is None:
        def first(*refs):
            return kern(*refs[:n_in], None, None, *refs[n_in:])
        return first, [], [], {}
    return kern, [_ANY_SPEC, _ANY_SPEC], list(accs), {n_in: 1, n_in + 1: 2}


def _rwkv_mixer(f, shift_all, s_all, l, s_acc, sh_acc, params, *, bb, c, tv, nsub=1):
    b, t, _ = f.shape
    rows = nsub * c
    nc = t // rows
    assert nsub == 1 or tv == c
    kern = functools.partial(_rwkv_kernel, bb=bb, c=c, nsub=nsub, tv=tv, nc=nc)
    state_spec = _layer_block((bb, N_HEADS, HEAD_DIM, HEAD_DIM), l)
    shift_spec = _layer_block((bb, 1, R_PROJ), l)
    kern, acc_specs, acc_args, aliases = _stacked_outputs(kern, 3 + len(params), (s_acc, sh_acc))
    return pl.pallas_call(
        kern,
        grid=(b // bb, nc),
        in_specs=[pl.BlockSpec((bb, rows, R_PROJ), lambda i, j: (i, j, 0)), shift_spec, state_spec]
                 + [_full_spec(p.shape) for p in params] + acc_specs,
        out_specs=[pl.BlockSpec((bb, rows, R_WIDTH), lambda i, j: (i, j, 0)), state_spec, shift_spec],
        out_shape=[jax.ShapeDtypeStruct((b, t, R_WIDTH), F32),
                   jax.ShapeDtypeStruct(s_all.shape, F32), jax.ShapeDtypeStruct(shift_all.shape, F32)],
        input_output_aliases=aliases,
        scratch_shapes=[pltpu.VMEM((bb, N_HEADS, HEAD_DIM, HEAD_DIM), F32),
                        pltpu.VMEM((bb, 1, R_PROJ), F32)],
        compiler_params=pltpu.CompilerParams(dimension_semantics=("arbitrary", "arbitrary"),
                                             vmem_limit_bytes=VMEM_LIMIT_BYTES),
        name="rwkv7",
    )(f, shift_all, s_all, *params, *acc_args)


def _gdn_kernel(x_ref, z_ref, gb_ref, cb_ref, s0_ref, cw_ref, alog_ref, dtb_ref, ng_ref, s_acc_ref, cb_acc_ref,
                y_ref, sout_ref, cbout_ref, s_scr, tail_scr, *, bb, c, nsub, tv, nc):
    del s_acc_ref, cb_acc_ref
    ci = pl.program_id(1)

    @pl.when(ci == 0)
    def _():
        s_scr[...] = s0_ref[...]
        tail_scr[...] = jnp.zeros_like(tail_scr)
        tail_scr[:, SUBLANES - (G_CONV - 1):SUBLANES, :] = cb_ref[...]

    row = lax.broadcasted_iota(jnp.int32, (c, 1), 0)
    ri = lax.broadcasted_iota(jnp.int32, (c, c), 0)
    cj = lax.broadcasted_iota(jnp.int32, (c, c), 1)
    strict = ri > cj
    incl = ri >= cj
    tri = jnp.where(incl, 1.0, 0.0).astype(BF16)
    valid = row < tv

    rows = nsub * c

    def one_group(seqs):
        n_seq = len(seqs)
        heads = [(n, h) for n in range(n_seq) for h in range(N_HEADS)]
        chains = [(n, s, h) for s in range(nsub) for n, h in heads]
        sub = lambda a, s: a[s * c:(s + 1) * c]
        x_l = [x_ref[i] for i in seqs]
        act_l = []
        for i, x in zip(seqs, x_l):
            xe = jnp.concatenate([tail_scr[i], x], axis=0)
            conv = x * cw_ref[G_CONV - 1:G_CONV, :]
            for s in range(1, G_CONV):
                conv = conv + pltpu.roll(xe, s, 0)[SUBLANES:, :] * cw_ref[G_CONV - 1 - s:G_CONV - s, :]
            tail_scr[i] = x[rows - SUBLANES:, :]
            act_l.append(_silu(conv))
        gb_l = [gb_ref[i] for i in seqs]
        beta_l = [_sigmoid(g) for g in gb_l]
        g_l = [-jnp.exp(alog_ref[...]) * _softplus(g + dtb_ref[...]) for g in gb_l]
        if tv < c:
            beta_l = [jnp.where(valid, x, 0.0) for x in beta_l]
            g_l = [jnp.where(valid, x, 0.0) for x in g_l]
        pad = lambda g: jnp.concatenate([g, jnp.zeros((LANES - c, LANES), F32)], axis=0) if c < LANES else g
        gcum = {(n, s): _mm_exact_lhs(tri, sub(g_l[n], s)) for s in range(nsub) for n in range(n_seq)}
        gcum_t = {k: pad(g).T for k, g in gcum.items()}
        qegs, khs, vbs, kbs, egs, decs, kdec, elast = [], [], [], [], [], [], [], []
        for n, s, h in chains:
            act = sub(act_l[n], s)
            qx = act[:, h * HEAD_DIM:(h + 1) * HEAD_DIM]
            kx = act[:, G_WIDTH + h * HEAD_DIM:G_WIDTH + (h + 1) * HEAD_DIM]
            vh = act[:, 2 * G_WIDTH + h * HEAD_DIM:2 * G_WIDTH + (h + 1) * HEAD_DIM]
            qh = qx * lax.rsqrt(jnp.sum(qx * qx, axis=-1, keepdims=True) + L2_EPS) * (HEAD_DIM ** -0.5)
            kh = kx * lax.rsqrt(jnp.sum(kx * kx, axis=-1, keepdims=True) + L2_EPS)
            if tv < c:
                kh = jnp.where(valid, kh, 0.0)
            beta = sub(beta_l[n], s)[:, h:h + 1]
            gc = gcum[n, s]
            gcol = gc[:, N_HEADS + h:N_HEADS + h + 1]
            grow = gcum_t[n, s][N_HEADS + h:N_HEADS + h + 1, 0:c]
            glast = gc[c - 1:c, N_HEADS + h:N_HEADS + h + 1]
            decs.append(jnp.exp(jnp.where(incl, gcol - grow, NEG_INF)))
            eg = jnp.exp(gcol)
            qegs.append((qh, eg))
            khs.append(kh)
            kbs.append(kh * beta)
            vbs.append(vh * beta)
            egs.append(eg)
            kdec.append(kh * jnp.exp(glast - gcol))
            elast.append(jnp.exp(glast))
        sol, qk = [], []
        for s in range(nsub):
            pick = lambda lst: lst[s * len(heads):(s + 1) * len(heads)]
            gram = _mm1_each([jnp.concatenate([kb, qh], axis=0) for kb, (qh, _) in zip(pick(kbs), pick(qegs))],
                             pick(khs), NT)
            t = _tri_inverse_each([-jnp.where(strict, g[:c] * d, 0.0) for g, d in zip(gram, pick(decs))], c)
            sol += _mm3_each(t, [jnp.concatenate([vb, kb * eg], axis=1)
                                 for vb, kb, eg in zip(pick(vbs), pick(kbs), pick(egs))])
            qk += [g[c:] * d for g, d in zip(gram, pick(decs))]
        wq = [jnp.concatenate([so[:, HEAD_DIM:], qh * eg], axis=0) for so, (qh, eg) in zip(sol, qegs)]
        ss = [s_scr[seqs[n], h] for n, h in heads]
        for s in range(nsub):
            pick = lambda lst: lst[s * len(heads):(s + 1) * len(heads)]
            ws = _mm3_each(pick(wq), ss)
            u = [so[:, :HEAD_DIM] - w[:c] for so, w in zip(pick(sol), ws)]
            qku = _mm1_each(pick(qk), u)
            ds = _mm3_each(pick(kdec), u, TN)
            ss = [el * st + d for el, st, d in zip(pick(elast), ss, ds)]
            for (n, h), w, qu in zip(heads, ws, qku):
                sl = slice(h * HEAD_DIM, (h + 1) * HEAD_DIM)
                o = w[c:] + qu
                o = o * lax.rsqrt(jnp.mean(o * o, axis=-1, keepdims=True) + G_NORM_EPS) * ng_ref[...]
                y_ref[seqs[n], s * c:(s + 1) * c, sl] = o * _silu(z_ref[seqs[n], s * c:(s + 1) * c, sl])
        for (n, h), st in zip(heads, ss):
            s_scr[seqs[n], h] = st

        @pl.when(ci == nc - 1)
        def _():
            shift = (rows - (rows - c + tv - (G_CONV - 1))) % rows
            for i, x in zip(seqs, x_l):
                xs = pltpu.roll(x, shift, 0) if shift else x
                cbout_ref[i] = xs[0:G_CONV - 1, :]

    _for_each_group(bb, one_group)

    @pl.when(ci == nc - 1)
    def _():
        sout_ref[...] = s_scr[...]


def _gdn_mixer(pg, cb_all, s_all, l, s_acc, cb_acc, params, *, bb, c, tv, nsub=1):
    b, t, _ = pg.shape
    rows = nsub * c
    nc = t // rows
    assert nsub == 1 or tv == c
    kern = functools.partial(_gdn_kernel, bb=bb, c=c, nsub=nsub, tv=tv, nc=nc)
    state_spec = _layer_block((bb, N_HEADS, HEAD_DIM, HEAD_DIM), l)
    cb_spec = _layer_block((bb, G_CONV - 1, G_QKV), l)
    kern, acc_specs, acc_args, aliases = _stacked_outputs(kern, 5 + len(params), (s_acc, cb_acc))
    return pl.pallas_call(
        kern,
        grid=(b // bb, nc),
        in_specs=[pl.BlockSpec((bb, rows, G_QKV), lambda i, j: (i, j, 0)),
                  pl.BlockSpec((bb, rows, G_WIDTH), lambda i, j: (i, j, G_QKV // G_WIDTH)),
                  pl.BlockSpec((bb, rows, LANES), lambda i, j: (i, j, (G_QKV + G_WIDTH) // LANES)),
                  cb_spec, state_spec] + [_full_spec(p.shape) for p in params] + acc_specs,
        out_specs=[pl.BlockSpec((bb, rows, G_WIDTH), lambda i, j: (i, j, 0)), state_spec, cb_spec],
        out_shape=[jax.ShapeDtypeStruct((b, t, G_WIDTH), F32),
                   jax.ShapeDtypeStruct(s_all.shape, F32), jax.ShapeDtypeStruct(cb_all.shape, F32)],
        input_output_aliases=aliases,
        scratch_shapes=[pltpu.VMEM((bb, N_HEADS, HEAD_DIM, HEAD_DIM), F32),
                        pltpu.VMEM((bb, SUBLANES, G_QKV), F32)],
        compiler_params=pltpu.CompilerParams(dimension_semantics=("arbitrary", "arbitrary"),
                                             vmem_limit_bytes=VMEM_LIMIT_BYTES),
        name="gdn",
    )(pg, pg, pg, cb_all, s_all, *params, *acc_args)


def _rel_buckets(dist):
    n = np.maximum(dist, 0)
    nf = np.maximum(n, 1).astype(np.float32)
    large = REL_MAX_EXACT + (np.log(nf / REL_MAX_EXACT) / math.log(REL_MAX_DIST / REL_MAX_EXACT)
                             * (REL_BUCKETS - REL_MAX_EXACT)).astype(np.int32)
    return np.where(n < REL_MAX_EXACT, n, np.minimum(large, REL_BUCKETS - 1)).astype(np.int32)


def _bias_kernel(table_ref, bucket_ref, o_ref):
    bucket = bucket_ref[...]
    for h in range(S_Q_HEADS):
        acc = jnp.full(bucket.shape, NEG_INF, F32)
        for k in range(REL_BUCKETS):
            acc = jnp.where(bucket == k, table_ref[k, h], acc)
        o_ref[h] = acc


def _rel_bias(table, dist, visible=None):
    bucket = _rel_buckets(dist)
    if visible is not None:
        bucket = np.where(visible, bucket, -1).astype(np.int32)
    bucket = jnp.asarray(bucket)
    return pl.pallas_call(
        _bias_kernel,
        in_specs=[pl.BlockSpec(memory_space=pltpu.SMEM), _full_spec(bucket.shape)],
        out_specs=_full_spec((S_Q_HEADS,) + bucket.shape),
        out_shape=jax.ShapeDtypeStruct((S_Q_HEADS,) + bucket.shape, F32),
        grid=(1,),
        name="rel_bias",
    )(table, bucket)


Q_BLOCKS_PER_STEP = 2


def _swa_prompt_kernel(sink_ref, q_ref, kp_ref, kc_ref, vp_ref, vc_ref, bias_ref, o_ref):
    n = pl.program_id(1)
    q = q_ref[...]
    kall = jnp.concatenate([kp_ref[...], kc_ref[...]], axis=0).astype(BF16)
    vall = jnp.concatenate([vp_ref[...], vc_ref[...]], axis=0).astype(BF16)
    chains = [(s, h) for s in range(Q_BLOCKS_PER_STEP) for h in range(S_Q_HEADS)]
    band = lambda a, s, h: a[s * ATTN_BLOCK:(s + 2) * ATTN_BLOCK,
                             (h // S_GROUP) * HEAD_DIM:(h // S_GROUP + 1) * HEAD_DIM]
    has_prev = [jnp.where(n == 0, 0, 1) if s == 0 else 1 for s in range(Q_BLOCKS_PER_STEP)]
    qk = [_dg(q[s * ATTN_BLOCK:(s + 1) * ATTN_BLOCK, h * HEAD_DIM:(h + 1) * HEAD_DIM].astype(BF16),
              band(kall, s, h), NT) for s, h in chains]
    logits = [x * (HEAD_DIM ** -0.5) + bias_ref[has_prev[s], h] for x, (s, h) in zip(qk, chains)]
    m = [jnp.maximum(jnp.max(x, axis=-1, keepdims=True), sink_ref[h]) for x, (s, h) in zip(logits, chains)]
    p = [jnp.exp(x - mm) for x, mm in zip(logits, m)]
    probs = [x / (jnp.sum(x, axis=-1, keepdims=True) + jnp.exp(sink_ref[h] - mm))
             for x, mm, (s, h) in zip(p, m, chains)]
    for x, (s, h) in zip(probs, chains):
        o_ref[s * ATTN_BLOCK:(s + 1) * ATTN_BLOCK, h * HEAD_DIM:(h + 1) * HEAD_DIM] = _dg(
            x.astype(BF16), band(vall, s, h), NN)


def _swa_prompt(ps, sinks, bias):
    b, t, _ = ps.shape
    rows = Q_BLOCKS_PER_STEP * ATTN_BLOCK
    kcol = S_WIDTH // S_KV_WIDTH
    prev = lambda j: jnp.maximum(Q_BLOCKS_PER_STEP * j - 1, 0)
    return pl.pallas_call(
        _swa_prompt_kernel,
        grid=(b, t // rows),
        in_specs=[pl.BlockSpec(memory_space=pltpu.SMEM),
                  pl.BlockSpec((None, rows, S_WIDTH), lambda i, j: (i, j, 0)),
                  pl.BlockSpec((None, ATTN_BLOCK, S_KV_WIDTH), lambda i, j: (i, prev(j), kcol)),
                  pl.BlockSpec((None, rows, S_KV_WIDTH), lambda i, j: (i, j, kcol)),
                  pl.BlockSpec((None, ATTN_BLOCK, S_KV_WIDTH), lambda i, j: (i, prev(j), kcol + 1)),
                  pl.BlockSpec((None, rows, S_KV_WIDTH), lambda i, j: (i, j, kcol + 1)),
                  _full_spec(bias.shape)],
        out_specs=pl.BlockSpec((None, rows, S_WIDTH), lambda i, j: (i, j, 0)),
        out_shape=jax.ShapeDtypeStruct((b, t, S_WIDTH), F32),
        compiler_params=pltpu.CompilerParams(dimension_semantics=("arbitrary", "arbitrary"),
                                             vmem_limit_bytes=VMEM_LIMIT_BYTES),
        name="swa_prompt",
    )(sinks, ps, ps, ps, ps, ps, bias)


def _swa_sample_kernel(sink_ref, q_ref, kn_ref, vn_ref, ck_ref, cv_ref, bias_c_ref, bias_n_ref,
                       k_acc_ref, v_acc_ref, o_ref, cko_ref, cvo_ref, *, bb, tp, tv):
    del k_acc_ref, v_acc_ref
    wc = WINDOW
    step = lambda n: jnp.concatenate([lax.broadcasted_iota(jnp.int32, (tp, n), 0)] * S_GROUP, axis=0)
    ti_c = step(wc)
    kj_c = lax.broadcasted_iota(jnp.int32, (S_GROUP * tp, wc), 1)
    dist_c = wc + ti_c - kj_c
    mask_c = (dist_c >= 0) & (dist_c < WINDOW)
    ti_n = step(tp)
    kj_n = lax.broadcasted_iota(jnp.int32, (S_GROUP * tp, tp), 1)
    dist_n = ti_n - kj_n
    mask_n = (dist_n >= 0) & (dist_n < WINDOW) & (kj_n < tv)

    sinks = [jnp.concatenate([jnp.full((tp, 1), sink_ref[kv * S_GROUP + g], F32) for g in range(S_GROUP)], axis=0)
             for kv in range(S_KV_HEADS)]
    old_rows = lax.broadcasted_iota(jnp.int32, (wc, 1), 0) < wc - tv
    scale = HEAD_DIM ** -0.5

    def one_group(seqs):
        chains = [(n, kv) for n in range(len(seqs)) for kv in range(S_KV_HEADS)]
        kvs = lambda a, kv: a[:, kv * HEAD_DIM:(kv + 1) * HEAD_DIM]
        q_l = [q_ref[i] for i in seqs]
        kn_l = [kn_ref[i] for i in seqs]
        vn_l = [vn_ref[i] for i in seqs]
        ck_l = [ck_ref[i] for i in seqs]
        cv_l = [cv_ref[i] for i in seqs]
        qg = [jnp.concatenate([q_l[n][:, (kv * S_GROUP + g) * HEAD_DIM:(kv * S_GROUP + g + 1) * HEAD_DIM]
                               for g in range(S_GROUP)], axis=0).astype(BF16) for n, kv in chains]
        lc = [_dg(x, kvs(ck_l[n], kv).astype(BF16), NT) for x, (n, kv) in zip(qg, chains)]
        ln = [_dg(x, kvs(kn_l[n], kv).astype(BF16), NT) for x, (n, kv) in zip(qg, chains)]
        lc = [jnp.where(mask_c, x * scale + bias_c_ref[kv], NEG_INF) for x, (n, kv) in zip(lc, chains)]
        ln = [jnp.where(mask_n, x * scale + bias_n_ref[kv], NEG_INF) for x, (n, kv) in zip(ln, chains)]
        m = [jnp.maximum(jnp.maximum(jnp.max(a, axis=-1, keepdims=True), jnp.max(b, axis=-1, keepdims=True)),
                         sinks[kv]) for a, b, (n, kv) in zip(lc, ln, chains)]
        pc = [jnp.exp(a - mm) for a, mm in zip(lc, m)]
        pn = [jnp.exp(a - mm) for a, mm in zip(ln, m)]
        den = [jnp.sum(a, axis=-1, keepdims=True) + jnp.sum(b, axis=-1, keepdims=True) + jnp.exp(sinks[kv] - mm)
               for a, b, mm, (n, kv) in zip(pc, pn, m, chains)]
        oc = [_dg((a / d).astype(BF16), kvs(cv_l[n], kv).astype(BF16), NN) for a, d, (n, kv) in zip(pc, den, chains)]
        on = [_dg((a / d).astype(BF16), kvs(vn_l[n], kv).astype(BF16), NN) for a, d, (n, kv) in zip(pn, den, chains)]
        for a, b, (n, kv) in zip(oc, on, chains):
            og = a + b
            for g in range(S_GROUP):
                hq = kv * S_GROUP + g
                o_ref[seqs[n], :, hq * HEAD_DIM:(hq + 1) * HEAD_DIM] = og[g * tp:(g + 1) * tp, :]
        zpad = jnp.zeros((wc - tp, S_KV_WIDTH), F32)
        for n, i in enumerate(seqs):
            cko_ref[i] = jnp.where(old_rows, pltpu.roll(ck_l[n], wc - tv, 0),
                                   pltpu.roll(jnp.concatenate([kn_l[n], zpad], axis=0), wc - tv, 0))
            cvo_ref[i] = jnp.where(old_rows, pltpu.roll(cv_l[n], wc - tv, 0),
                                   pltpu.roll(jnp.concatenate([vn_l[n], zpad], axis=0), wc - tv, 0))

    _for_each_group(bb, one_group)


def _swa_sample(ps, ck_all, cv_all, l, k_acc, v_acc, sinks, bias_c, bias_n, *, bb, tv):
    b, tp, _ = ps.shape
    kcol = S_WIDTH // S_KV_WIDTH
    kern = functools.partial(_swa_sample_kernel, bb=bb, tp=tp, tv=tv)
    kern, acc_specs, acc_args, aliases = _stacked_outputs(kern, 8, (k_acc, v_acc))
    cache_spec = _layer_block((bb, WINDOW, S_KV_WIDTH), l)
    return pl.pallas_call(
        kern,
        grid=(b // bb,),
        in_specs=[pl.BlockSpec(memory_space=pltpu.SMEM),
                  pl.BlockSpec((bb, tp, S_WIDTH), lambda i: (i, 0, 0)),
                  pl.BlockSpec((bb, tp, S_KV_WIDTH), lambda i: (i, 0, kcol)),
                  pl.BlockSpec((bb, tp, S_KV_WIDTH), lambda i: (i, 0, kcol + 1)),
                  cache_spec, cache_spec, _full_spec(bias_c.shape), _full_spec(bias_n.shape)] + acc_specs,
        out_specs=[pl.BlockSpec((bb, tp, S_WIDTH), lambda i: (i, 0, 0)), cache_spec, cache_spec],
        out_shape=[jax.ShapeDtypeStruct((b, tp, S_WIDTH), F32),
                   jax.ShapeDtypeStruct(ck_all.shape, F32), jax.ShapeDtypeStruct(cv_all.shape, F32)],
        input_output_aliases=aliases,
        compiler_params=pltpu.CompilerParams(dimension_semantics=("arbitrary",),
                                             vmem_limit_bytes=VMEM_LIMIT_BYTES),
        name="swa_sample",
    )(sinks, ps, ps, ps, ck_all, cv_all, bias_c, bias_n, *acc_args)


def _prep_layer(l, ffn1_w_in, ffn1_w_out, ln1_g, ln1_b, w_in, rwkv_mu, rwkv_w0, rwkv_w_up, rwkv_a0,
                rwkv_a_up, rwkv_g_up, rwkv_k_k, rwkv_k_a, rwkv_r_k, rwkv_gn_g, rwkv_gn_b, swa_sinks,
                gdn_conv_w, gdn_a_log, gdn_dt_bias, gdn_norm_g, w_out, ln2_g, ln2_b,
                ffn2_w_in, ffn2_w_out, ln3_g, ln3_b):
    row = lambda a: a[l].reshape(1, -1)
    d = w_in.shape[1]
    n_beta = 4 * G_WIDTH
    win = jnp.concatenate([w_in[l], jnp.zeros((d, PROJ_PAD - w_in.shape[2]), F32)], axis=1).astype(BF16)
    lane_tile = lambda a: jnp.zeros((1, LANES), F32).at[0, N_HEADS:2 * N_HEADS].set(a[l])
    del n_beta
    return dict(
        ffn1=(ffn1_w_in[l].astype(BF16), ffn1_w_out[l].astype(BF16), row(ln1_g), row(ln1_b), win),
        rwkv=(row(rwkv_mu), row(rwkv_w0), rwkv_w_up[l], row(rwkv_a0), rwkv_a_up[l], rwkv_g_up[l],
              row(rwkv_k_k), row(rwkv_k_a), row(rwkv_r_k), row(rwkv_gn_g), row(rwkv_gn_b)),
        sinks=swa_sinks[l],
        gdn=(gdn_conv_w[l], lane_tile(gdn_a_log), lane_tile(gdn_dt_bias), row(gdn_norm_g)),
        out=(w_out[l].astype(BF16), row(ln2_g), row(ln2_b), ffn2_w_in[l].astype(BF16),
             ffn2_w_out[l].astype(BF16), row(ln3_g), row(ln3_b)),
    )


def _run_trunk(x, rwkv_s, rwkv_shift, swa_k, swa_v, gdn_s, gdn_conv, layers, biases, *, prompt, tv, alpha,
               tm, n_chunks, bb, c, nsub):
    b, t, d = x.shape
    depth = len(layers)
    xf = x.reshape(b * t, d)
    rwkv_shift = rwkv_shift.reshape(depth, b, 1, R_PROJ)
    s_r = sh_r = s_g = cb = None
    kc, vc = ([], []) if prompt else (None, None)
    for l, lp in enumerate(layers):
        x1, pr, ps, pg = _ffn_proj(xf, *lp["ffn1"], alpha=alpha, tm=tm, n_chunks=n_chunks)
        pr = pr.reshape(b, t, R_PROJ)
        ps = ps.reshape(b, t, S_PROJ)
        pg = pg.reshape(b, t, G_PROJ_PAD)
        y_r, s_r, sh_r = _rwkv_mixer(pr, rwkv_shift, rwkv_s, l, s_r, sh_r, lp["rwkv"], bb=bb, c=c, tv=tv, nsub=nsub)
        if prompt:
            y_s = _swa_prompt(ps, lp["sinks"], biases[0])
            kc.append(ps[:, t - WINDOW:, S_WIDTH:S_WIDTH + S_KV_WIDTH])
            vc.append(ps[:, t - WINDOW:, S_WIDTH + S_KV_WIDTH:])
        else:
            y_s, kc, vc = _swa_sample(ps, swa_k, swa_v, l, kc, vc, lp["sinks"], biases[1], biases[2], bb=bb, tv=tv)
        y_g, s_g, cb = _gdn_mixer(pg, gdn_conv, gdn_s, l, s_g, cb, lp["gdn"], bb=bb, c=c, tv=tv, nsub=nsub)
        xf = _out_ffn(x1, y_r.reshape(b * t, R_WIDTH), y_s.reshape(b * t, S_WIDTH), y_g.reshape(b * t, G_WIDTH),
                      *lp["out"], alpha=alpha, tm=tm, n_chunks=n_chunks)
    if prompt:
        kc, vc = jnp.stack(kc, axis=0), jnp.stack(vc, axis=0)
    cache_shape = (depth, b, WINDOW, S_KV_HEADS, HEAD_DIM)
    return xf.reshape(b, t, d), [s_r, sh_r.reshape(depth, b, R_PROJ), kc.reshape(cache_shape),
                                 vc.reshape(cache_shape), s_g, cb]


def kernel(x_prompt, x_sample, state_rwkv, state_rwkv_shift, cache_swa_k, cache_swa_v, state_gdn, state_gdn_conv, ffn1_w_in, ffn1_w_out, ln1_g, ln1_b, w_in, rwkv_mu, rwkv_w0, rwkv_w_up, rwkv_a0, rwkv_a_up, rwkv_g_up, rwkv_k_k, rwkv_k_a, rwkv_r_k, rwkv_gn_g, rwkv_gn_b, swa_sinks, rel_table, gdn_conv_w, gdn_a_log, gdn_dt_bias, gdn_norm_g, w_out, ln2_g, ln2_b, ffn2_w_in, ffn2_w_out, ln3_g, ln3_b):
    depth = ffn1_w_in.shape[0]
    alpha = (2 * depth) ** 0.25
    layers = [_prep_layer(l, ffn1_w_in, ffn1_w_out, ln1_g, ln1_b, w_in, rwkv_mu, rwkv_w0, rwkv_w_up, rwkv_a0,
                          rwkv_a_up, rwkv_g_up, rwkv_k_k, rwkv_k_a, rwkv_r_k, rwkv_gn_g, rwkv_gn_b, swa_sinks,
                          gdn_conv_w, gdn_a_log, gdn_dt_bias, gdn_norm_g, w_out, ln2_g, ln2_b,
                          ffn2_w_in, ffn2_w_out, ln3_g, ln3_b) for l in range(depth)]
    bp, tp_len, d = x_prompt.shape
    bs, ts, _ = x_sample.shape
    ts_pad = -(-ts // SUBLANES) * SUBLANES

    qi = np.arange(ATTN_BLOCK)[:, None]
    kj = np.arange(2 * ATTN_BLOCK)[None, :]
    dist_p = ATTN_BLOCK + qi - kj
    in_window = (dist_p >= 0) & (dist_p < WINDOW)
    bias_p = jnp.stack([_rel_bias(rel_table, dist_p, in_window & (kj >= ATTN_BLOCK)),
                        _rel_bias(rel_table, dist_p, in_window)], axis=0)
    ti = (np.arange(S_GROUP * ts_pad) % ts_pad)[:, None]
    wc = cache_swa_k.shape[2]
    bias_c = _rel_bias(rel_table, wc + ti - np.arange(wc)[None, :])
    bias_n = _rel_bias(rel_table, ti - np.arange(ts_pad)[None, :])
    regroup = lambda a: jnp.stack([jnp.concatenate([a[kv * S_GROUP + g, g * ts_pad:(g + 1) * ts_pad]
                                                    for g in range(S_GROUP)], axis=0)
                                   for kv in range(S_KV_HEADS)], axis=0)
    biases = (bias_p, regroup(bias_c), regroup(bias_n))

    zeros = lambda *s: jnp.zeros((depth, bp) + s, F32)
    y_prompt, p_states = _run_trunk(
        x_prompt, zeros(N_HEADS, HEAD_DIM, HEAD_DIM), zeros(R_PROJ), None, None,
        zeros(N_HEADS, HEAD_DIM, HEAD_DIM), zeros(G_CONV - 1, G_QKV), layers, biases,
        prompt=True, tv=64, alpha=alpha, tm=512, n_chunks=2, bb=bp, c=64, nsub=2)

    xs = jnp.concatenate([x_sample, jnp.zeros((bs, ts_pad - ts, d), F32)], axis=1)
    ck = cache_swa_k.reshape(depth, bs, wc, S_KV_WIDTH)
    cv = cache_swa_v.reshape(depth, bs, wc, S_KV_WIDTH)
    y_sample, s_states = _run_trunk(
        xs, state_rwkv, state_rwkv_shift, ck, cv, state_gdn, state_gdn_conv, layers, biases,
        prompt=False, tv=ts, alpha=alpha, tm=512, n_chunks=2, bb=8, c=ts_pad, nsub=1)
    return (y_prompt, y_sample[:, :ts]) + tuple(p_states) + tuple(s_states)
```

```python
import functools
import math

import numpy as np
import jax
import jax.numpy as jnp
from jax import lax
from jax.experimental import pallas as pl
from jax.experimental.pallas import tpu as pltpu

F32 = jnp.float32
BF16 = jnp.bfloat16

HEAD_DIM = 64
N_HEADS = 4
R_WIDTH = N_HEADS * HEAD_DIM
R_PROJ = 896
S_Q_HEADS = 8
S_KV_HEADS = 2
S_GROUP = S_Q_HEADS // S_KV_HEADS
S_WIDTH = S_Q_HEADS * HEAD_DIM
S_KV_WIDTH = S_KV_HEADS * HEAD_DIM
S_PROJ = S_WIDTH + 2 * S_KV_WIDTH
G_WIDTH = N_HEADS * HEAD_DIM
G_CONV = 4
G_QKV = 3 * G_WIDTH
G_PROJ_PAD = 4 * G_WIDTH + 128
WINDOW = 128
ATTN_BLOCK = 128
REL_BUCKETS = 32
REL_MAX_EXACT = 16
REL_MAX_DIST = 128
NEG_INF = -1e30
R_GN_EPS = 64e-5
G_NORM_EPS = 1e-6
LN_EPS = 1e-5
L2_EPS = 1e-6
PROJ_PAD = R_PROJ + S_PROJ + G_PROJ_PAD

LANES = 128
SUBLANES = 8
VMEM_LIMIT_BYTES = 56 * 1024 * 1024

NN = ((1,), (0,))
NT = ((1,), (1,))
TN = ((0,), (0,))


def _dg(a, b, dims):
    return lax.dot_general(a, b, (dims, ((), ())), preferred_element_type=F32)


def _split2(a):
    hi = a.astype(BF16)
    lo = (a - hi.astype(F32)).astype(BF16)
    return hi, lo


def _mm_exact_lhs(a_bf16, b, dims=NN):
    b1 = b.astype(BF16)
    r1 = b - b1.astype(F32)
    b2 = r1.astype(BF16)
    b3 = (r1 - b2.astype(F32)).astype(BF16)
    return _dg(a_bf16, b1, dims) + (_dg(a_bf16, b2, dims) + _dg(a_bf16, b3, dims))


def _sigmoid(x):
    return 1.0 / (1.0 + jnp.exp(-x))


def _silu(x):
    return x * _sigmoid(x)


def _softplus(x):
    return jnp.maximum(x, 0.0) + jnp.log(1.0 + jnp.exp(-jnp.abs(x)))


def _layer_norm(z, g, b):
    mu = jnp.mean(z, axis=-1, keepdims=True)
    zc = z - mu
    var = jnp.mean(zc * zc, axis=-1, keepdims=True)
    return zc * lax.rsqrt(var + LN_EPS) * g + b


def _mm3_each(a_list, b_list, dims=NN):
    sa = [_split2(a) for a in a_list]
    sb = [_split2(b) for b in b_list]
    return [_dg(ah, bh, dims) + (_dg(ah, bl, dims) + _dg(al, bh, dims))
            for (ah, al), (bh, bl) in zip(sa, sb)]


def _mm1_each(a_list, b_list, dims=NN):
    return [_dg(a.astype(BF16), b.astype(BF16), dims) for a, b in zip(a_list, b_list)]


def _bf16_each(a_list):
    return [a.astype(BF16) for a in a_list]


def _tri_inverse_each(m_list, c):
    row = lax.broadcasted_iota(jnp.int32, (c, c), 0)
    col = lax.broadcasted_iota(jnp.int32, (c, c), 1)
    eye = jnp.where(row == col, 1.0, 0.0).astype(F32)
    t = [eye + m for m in m_list]
    mp = _bf16_each(m_list)
    span = 2
    while span < c:
        mp = _bf16_each(_mm1_each(mp, mp))
        t = [a + b for a, b in zip(t, _mm1_each(t, mp))]
        span *= 2
    return t


def _for_each_group(bb, one_group):
    one_group(list(range(bb)))


def _full_spec(shape):
    nd = len(shape)
    return pl.BlockSpec(shape, lambda *_: (0,) * nd)


def _resident_spec(shape, index_map):
    return pl.BlockSpec(shape, index_map, pipeline_mode=pl.Buffered(1))


SUB_ROWS = 256


def _row_tiles(tm):
    n = max(tm // SUB_ROWS, 1)
    return [slice(i * (tm // n), (i + 1) * (tm // n)) for i in range(n)]


def _swiglu_each(xb_l, wi_ref, wo_ref, d_ff, n_chunks):
    cw = d_ff // n_chunks
    acc = [None] * len(xb_l)
    for j in range(n_chunks):
        gate = [_dg(xb, wi_ref[:, j * cw:(j + 1) * cw], NN) for xb in xb_l]
        up = [_dg(xb, wi_ref[:, d_ff + j * cw:d_ff + (j + 1) * cw], NN) for xb in xb_l]
        act = [(_silu(g) * u).astype(BF16) for g, u in zip(gate, up)]
        part = [_dg(a, wo_ref[j * cw:(j + 1) * cw, :], NN) for a in act]
        acc = [p if a is None else a + p for a, p in zip(acc, part)]
    return acc


def _ffn_proj_kernel(x_ref, wi_ref, wo_ref, g_ref, b_ref, win_ref,
                     x1_ref, pr_ref, ps_ref, pg_ref, *, alpha, d_ff, n_chunks, tm):
    tiles = _row_tiles(tm)
    x_l = [x_ref[r, :] for r in tiles]
    y_l = _swiglu_each([x.astype(BF16) for x in x_l], wi_ref, wo_ref, d_ff, n_chunks)
    x1_l = [_layer_norm(alpha * x + 0.5 * y, g_ref[...], b_ref[...]) for x, y in zip(x_l, y_l)]
    p_l = [_dg(x1.astype(BF16), win_ref[...], NN) for x1 in x1_l]
    for r, x1, p in zip(tiles, x1_l, p_l):
        x1_ref[r, :] = x1
        pr_ref[r, :] = p[:, :R_PROJ]
        ps_ref[r, :] = p[:, R_PROJ:R_PROJ + S_PROJ]
        pg_ref[r, :] = p[:, R_PROJ + S_PROJ:]


def _ffn_proj(x, wi, wo, g, b, win, *, alpha, tm, n_chunks):
    m, d = x.shape
    tm = min(tm, m)
    assert m % tm == 0
    d_ff = wo.shape[0]
    kern = functools.partial(_ffn_proj_kernel, alpha=alpha, d_ff=d_ff, n_chunks=n_chunks, tm=tm)
    row = lambda w: pl.BlockSpec((tm, w), lambda i: (i, 0))
    const = lambda i: (0, 0)
    return pl.pallas_call(
        kern,
        grid=(m // tm,),
        in_specs=[row(d),
                  _resident_spec(wi.shape, const), _resident_spec(wo.shape, const),
                  _resident_spec(g.shape, const), _resident_spec(b.shape, const),
                  _resident_spec(win.shape, const)],
        out_specs=[row(d), row(R_PROJ), row(S_PROJ), row(G_PROJ_PAD)],
        out_shape=[jax.ShapeDtypeStruct((m, d), F32), jax.ShapeDtypeStruct((m, R_PROJ), F32),
                   jax.ShapeDtypeStruct((m, S_PROJ), F32), jax.ShapeDtypeStruct((m, G_PROJ_PAD), F32)],
        compiler_params=pltpu.CompilerParams(dimension_semantics=("arbitrary",),
                                             vmem_limit_bytes=VMEM_LIMIT_BYTES),
        name="ffn_proj",
    )(x, wi, wo, g, b, win)


def _out_ffn_kernel(x_ref, yr_ref, ys_ref, yg_ref, wout_ref, g2_ref, b2_ref, wi_ref, wo_ref, g3_ref, b3_ref,
                    o_ref, *, alpha, d_ff, n_chunks, tm):
    tiles = _row_tiles(tm)
    x_l = [x_ref[r, :] for r in tiles]
    mix_l = [(_dg(yr_ref[r, :].astype(BF16), wout_ref[0:R_WIDTH, :], NN)
              + _dg(ys_ref[r, :].astype(BF16), wout_ref[R_WIDTH:R_WIDTH + S_WIDTH, :], NN)
              + _dg(yg_ref[r, :].astype(BF16), wout_ref[R_WIDTH + S_WIDTH:, :], NN)) for r in tiles]
    x2_l = [_layer_norm(alpha * x + mix, g2_ref[...], b2_ref[...]) for x, mix in zip(x_l, mix_l)]
    y_l = _swiglu_each([x2.astype(BF16) for x2 in x2_l], wi_ref, wo_ref, d_ff, n_chunks)
    for r, x2, y in zip(tiles, x2_l, y_l):
        o_ref[r, :] = _layer_norm(alpha * x2 + 0.5 * y, g3_ref[...], b3_ref[...])


def _out_ffn(x, yr, ys, yg, wout, g2, b2, wi, wo, g3, b3, *, alpha, tm, n_chunks):
    m, d = x.shape
    tm = min(tm, m)
    assert m % tm == 0
    d_ff = wo.shape[0]
    kern = functools.partial(_out_ffn_kernel, alpha=alpha, d_ff=d_ff, n_chunks=n_chunks, tm=tm)
    row = lambda w: pl.BlockSpec((tm, w), lambda i: (i, 0))
    const = lambda i: (0, 0)
    res = lambda a: _resident_spec(a.shape, const)
    return pl.pallas_call(
        kern,
        grid=(m // tm,),
        in_specs=[row(d), row(R_WIDTH), row(S_WIDTH), row(G_WIDTH),
                  res(wout), res(g2), res(b2), res(wi), res(wo), res(g3), res(b3)],
        out_specs=row(d),
        out_shape=jax.ShapeDtypeStruct((m, d), F32),
        compiler_params=pltpu.CompilerParams(dimension_semantics=("arbitrary",),
                                             vmem_limit_bytes=VMEM_LIMIT_BYTES),
        name="out_ffn",
    )(x, yr, ys, yg, wout, g2, b2, wi, wo, g3, b3)


def _rwkv_kernel(f_ref, shift_ref, s0_ref, mu_ref, w0_ref, wup_ref, a0_ref, aup_ref, gup_ref,
                 kk_ref, ka_ref, rk_ref, gng_ref, gnb_ref, s_acc_ref, sh_acc_ref,
                 y_ref, sout_ref, shout_ref, s_scr, prev_scr, *, bb, c, nsub, tv, nc):
    del s_acc_ref, sh_acc_ref
    ci = pl.program_id(1)

    @pl.when(ci == 0)
    def _():
        s_scr[...] = s0_ref[...]
        prev_scr[...] = shift_ref[...]

    row = lax.broadcasted_iota(jnp.int32, (c, 1), 0)
    ri = lax.broadcasted_iota(jnp.int32, (c, 2 * c), 0)
    cj = lax.broadcasted_iota(jnp.int32, (c, 2 * c), 1)
    cj = jnp.where(cj >= c, cj - c, cj)
    strict = ri > cj
    incl = ri >= cj
    tri = (lax.broadcasted_iota(jnp.int32, (c, c), 0) >= lax.broadcasted_iota(jnp.int32, (c, c), 1))
    tri = jnp.where(tri, 1.0, 0.0).astype(BF16)
    valid = row < tv

    rows = nsub * c
    first_row = lax.broadcasted_iota(jnp.int32, (rows, 1), 0) == 0

    def one_group(seqs):
        n_seq = len(seqs)
        heads = [(n, h) for n in range(n_seq) for h in range(N_HEADS)]
        hs = lambda arr, h: arr[:, h * HEAD_DIM:(h + 1) * HEAD_DIM]
        fl = [f_ref[i] for i in seqs]
        prevs = [jnp.where(first_row, prev_scr[i], pltpu.roll(f, 1, 0)) for i, f in zip(seqs, fl)]
        for i, f in zip(seqs, fl):
            prev_scr[i] = f[rows - 1:rows, :]

        fsl = [f + (p - f) * mu_ref[...] for f, p in zip(fl, prevs)]
        w_l = _mm3_each([jnp.tanh(fs[:, 768:800]) for fs in fsl], [wup_ref[...]] * n_seq)
        lr_l = _mm3_each([fs[:, 800:832] for fs in fsl], [aup_ref[...]] * n_seq)
        gate_l = _mm3_each([_sigmoid(fs[:, 832:896]) for fs in fsl], [gup_ref[...]] * n_seq)
        lr_l = [_sigmoid(a0_ref[...] + x) for x in lr_l]
        lw_l = [-jnp.exp(-_softplus(-(w0_ref[...] + x)) - 0.5) for x in w_l]
        r_l = [fs[:, 0:R_WIDTH] for fs in fsl]
        k_l = [fs[:, R_WIDTH:2 * R_WIDTH] for fs in fsl]
        v_l = [fs[:, 2 * R_WIDTH:3 * R_WIDTH] for fs in fsl]
        kkx_l = [k * kk_ref[...] for k in k_l]
        k2_l = [k * (1.0 + (lr - 1.0) * ka_ref[...]) for k, lr in zip(k_l, lr_l)]
        if tv < c:
            lw_l = [jnp.where(valid, x, 0.0) for x in lw_l]
            kkx_l = [jnp.where(valid, x, 0.0) for x in kkx_l]
            k2_l = [jnp.where(valid, x, 0.0) for x in k2_l]
        cum_l = [jnp.concatenate([_mm_exact_lhs(tri, lw[s * c:(s + 1) * c]) for s in range(nsub)], axis=0)
                 if nsub > 1 else _mm_exact_lhs(tri, lw) for lw in lw_l]
        p_in_l = [jnp.exp(x) for x in cum_l]
        p_prev_l = [jnp.exp(x - lw) for x, lw in zip(cum_l, lw_l)]
        p_inv_l = [jnp.exp(-x) for x in cum_l]

        def intra(s):
            hs = lambda arr, h: arr[s * c:(s + 1) * c, h * HEAD_DIM:(h + 1) * HEAD_DIM]
            xs, y2s, vhs, plast, rk2 = [], [], [], [], []
            for n, h in heads:
                kx = hs(kkx_l[n], h)
                kkh = kx * lax.rsqrt(jnp.sum(kx * kx, axis=-1, keepdims=True) + L2_EPS)
                at = -kkh * hs(p_prev_l[n], h)
                bt = kkh * hs(lr_l[n], h) * hs(p_inv_l[n], h)
                kt = hs(k2_l[n], h) * hs(p_inv_l[n], h)
                qt = hs(r_l[n], h) * hs(p_in_l[n], h)
                xs.append(jnp.concatenate([at, qt], axis=0))
                y2s.append(jnp.concatenate([bt, kt], axis=0))
                vhs.append(hs(v_l[n], h))
                plast.append(hs(p_in_l[n], h)[c - 1:c])
                rk2.append(hs(r_l[n], h) * hs(k2_l[n], h))
            xs = _bf16_each(xs)
            y2s = _bf16_each(y2s)
            gram = _mm1_each(xs, y2s, NT)
            xa = [jnp.where(strict, g[:c], 0.0) for g in gram]
            xq = [jnp.where(incl, g[c:], 0.0) for g in gram]
            lakv = _mm1_each([x[:, c:] for x in xa], vhs)
            t = _tri_inverse_each([x[:, :c] for x in xa], c)
            gate = [hs(gate_l[n], h) for n, h in heads]
            return dict(xs=xs, y2s=y2s, vhs=vhs, xq=xq, lakv=lakv, t=t, plast=plast, rk2=rk2, gate=gate)

        parts = [intra(s) for s in range(nsub)]
        ss = [s_scr[seqs[n], h] for n, h in heads]
        for s, p in enumerate(parts):
            xst = _mm1_each(p["xs"], ss, NT)
            u = _mm1_each(p["t"], [a[:c] + b for a, b in zip(xst, p["lakv"])])
            uv = _bf16_each([jnp.concatenate([a, b], axis=0) for a, b in zip(u, p["vhs"])])
            o_l = [a[c:] + b for a, b in zip(xst, _mm1_each(p["xq"], uv))]
            ds = _mm1_each(uv, p["y2s"], TN)
            ss = [(st + d) * pl_ for st, d, pl_ in zip(ss, ds, p["plast"])]
            for (n, h), o, vh, rk2, gate in zip(heads, o_l, p["vhs"], p["rk2"], p["gate"]):
                sl = slice(h * HEAD_DIM, (h + 1) * HEAD_DIM)
                mean = jnp.mean(o, axis=-1, keepdims=True)
                oc = o - mean
                var = jnp.mean(oc * oc, axis=-1, keepdims=True)
                on = oc * lax.rsqrt(var + R_GN_EPS) * gng_ref[:, sl] + gnb_ref[:, sl]
                bonus = jnp.sum(rk2 * rk_ref[:, sl], axis=-1, keepdims=True) * vh
                y_ref[seqs[n], s * c:(s + 1) * c, sl] = (on + bonus) * gate
        for (n, h), st in zip(heads, ss):
            s_scr[seqs[n], h] = st

        @pl.when(ci == nc - 1)
        def _():
            last = rows - c + tv
            for i, f in zip(seqs, fl):
                shout_ref[i] = f[last - 1:last, :]

    _for_each_group(bb, one_group)

    @pl.when(ci == nc - 1)
    def _():
        sout_ref[...] = s_scr[...]


def _layer_block(shape, l):
    nd = len(shape)
    return pl.BlockSpec((None,) + tuple(shape), lambda i, j=0: (l, i) + (0,) * (nd - 1))


_ANY_SPEC = pl.BlockSpec(memory_space=pl.ANY)


def _stacked_outputs(kern, n_in, accs):
    if accs[0] is None:
        def first(*refs):
            return kern(*refs[:n_in], None, None, *refs[n_in:])
        return first, [], [], {}
    return kern, [_ANY_SPEC, _ANY_SPEC], list(accs), {n_in: 1, n_in + 1: 2}


def _rwkv_mixer(f, shift_all, s_all, l, s_acc, sh_acc, params, *, bb, c, tv, nsub=1):
    b, t, _ = f.shape
    bb = min(bb, b)
    assert b % bb == 0
    rows = nsub * c
    nc = t // rows
    assert nsub == 1 or tv == c
    kern = functools.partial(_rwkv_kernel, bb=bb, c=c, nsub=nsub, tv=tv, nc=nc)
    state_spec = _layer_block((bb, N_HEADS, HEAD_DIM, HEAD_DIM), l)
    shift_spec = _layer_block((bb, 1, R_PROJ), l)
    kern, acc_specs, acc_args, aliases = _stacked_outputs(kern, 3 + len(params), (s_acc, sh_acc))
    return pl.pallas_call(
        kern,
        grid=(b // bb, nc),
        in_specs=[pl.BlockSpec((bb, rows, R_PROJ), lambda i, j: (i, j, 0)), shift_spec, state_spec]
                 + [_full_spec(p.shape) for p in params] + acc_specs,
        out_specs=[pl.BlockSpec((bb, rows, R_WIDTH), lambda i, j: (i, j, 0)), state_spec, shift_spec],
        out_shape=[jax.ShapeDtypeStruct((b, t, R_WIDTH), F32),
                   jax.ShapeDtypeStruct(s_all.shape, F32), jax.ShapeDtypeStruct(shift_all.shape, F32)],
        input_output_aliases=aliases,
        scratch_shapes=[pltpu.VMEM((bb, N_HEADS, HEAD_DIM, HEAD_DIM), F32),
                        pltpu.VMEM((bb, 1, R_PROJ), F32)],
        compiler_params=pltpu.CompilerParams(dimension_semantics=("arbitrary", "arbitrary"),
                                             vmem_limit_bytes=VMEM_LIMIT_BYTES),
        name="rwkv7",
    )(f, shift_all, s_all, *params, *acc_args)


def _gdn_kernel(x_ref, z_ref, gb_ref, cb_ref, s0_ref, cw_ref, alog_ref, dtb_ref, ng_ref, s_acc_ref, cb_acc_ref,
                y_ref, sout_ref, cbout_ref, s_scr, tail_scr, *, bb, c, nsub, tv, nc):
    del s_acc_ref, cb_acc_ref
    ci = pl.program_id(1)

    @pl.when(ci == 0)
    def _():
        s_scr[...] = s0_ref[...]
        tail_scr[...] = jnp.zeros_like(tail_scr)
        tail_scr[:, SUBLANES - (G_CONV - 1):SUBLANES, :] = cb_ref[...]

    row = lax.broadcasted_iota(jnp.int32, (c, 1), 0)
    ri = lax.broadcasted_iota(jnp.int32, (c, c), 0)
    cj = lax.broadcasted_iota(jnp.int32, (c, c), 1)
    strict = ri > cj
    incl = ri >= cj
    tri = jnp.where(incl, 1.0, 0.0).astype(BF16)
    valid = row < tv

    rows = nsub * c

    def one_group(seqs):
        n_seq = len(seqs)
        heads = [(n, h) for n in range(n_seq) for h in range(N_HEADS)]
        chains = [(n, s, h) for s in range(nsub) for n, h in heads]
        sub = lambda a, s: a[s * c:(s + 1) * c]
        x_l = [x_ref[i] for i in seqs]
        act_l = []
        for i, x in zip(seqs, x_l):
            xe = jnp.concatenate([tail_scr[i], x], axis=0)
            conv = x * cw_ref[G_CONV - 1:G_CONV, :]
            for s in range(1, G_CONV):
                conv = conv + pltpu.roll(xe, s, 0)[SUBLANES:, :] * cw_ref[G_CONV - 1 - s:G_CONV - s, :]
            tail_scr[i] = x[rows - SUBLANES:, :]
            act_l.append(_silu(conv))
        gb_l = [gb_ref[i] for i in seqs]
        beta_l = [_sigmoid(g) for g in gb_l]
        g_l = [-jnp.exp(alog_ref[...]) * _softplus(g + dtb_ref[...]) for g in gb_l]
        if tv < c:
            beta_l = [jnp.where(valid, x, 0.0) for x in beta_l]
            g_l = [jnp.where(valid, x, 0.0) for x in g_l]
        pad = lambda g: jnp.concatenate([g, jnp.zeros((LANES - c, LANES), F32)], axis=0) if c < LANES else g
        gcum = {(n, s): _mm_exact_lhs(tri, sub(g_l[n], s)) for s in range(nsub) for n in range(n_seq)}
        gcum_t = {k: pad(g).T for k, g in gcum.items()}
        qegs, khs, vbs, kbs, egs, decs, kdec, elast = [], [], [], [], [], [], [], []
        for n, s, h in chains:
            act = sub(act_l[n], s)
            qx = act[:, h * HEAD_DIM:(h + 1) * HEAD_DIM]
            kx = act[:, G_WIDTH + h * HEAD_DIM:G_WIDTH + (h + 1) * HEAD_DIM]
            vh = act[:, 2 * G_WIDTH + h * HEAD_DIM:2 * G_WIDTH + (h + 1) * HEAD_DIM]
            qh = qx * lax.rsqrt(jnp.sum(qx * qx, axis=-1, keepdims=True) + L2_EPS) * (HEAD_DIM ** -0.5)
            kh = kx * lax.rsqrt(jnp.sum(kx * kx, axis=-1, keepdims=True) + L2_EPS)
            if tv < c:
                kh = jnp.where(valid, kh, 0.0)
            beta = sub(beta_l[n], s)[:, h:h + 1]
            gc = gcum[n, s]
            gcol = gc[:, N_HEADS + h:N_HEADS + h + 1]
            grow = gcum_t[n, s][N_HEADS + h:N_HEADS + h + 1, 0:c]
            glast = gc[c - 1:c, N_HEADS + h:N_HEADS + h + 1]
            decs.append(jnp.exp(jnp.where(incl, gcol - grow, NEG_INF)))
            eg = jnp.exp(gcol)
            qegs.append((qh, eg))
            khs.append(kh)
            kbs.append(kh * beta)
            vbs.append(vh * beta)
            egs.append(eg)
            kdec.append(kh * jnp.exp(glast - gcol))
            elast.append(jnp.exp(glast))
        sol, qk = [], []
        for s in range(nsub):
            pick = lambda lst: lst[s * len(heads):(s + 1) * len(heads)]
            gram = _mm1_each([jnp.concatenate([kb, qh], axis=0) for kb, (qh, _) in zip(pick(kbs), pick(qegs))],
                             pick(khs), NT)
            t = _tri_inverse_each([-jnp.where(strict, g[:c] * d, 0.0) for g, d in zip(gram, pick(decs))], c)
            sol += _mm3_each(t, [jnp.concatenate([vb, kb * eg], axis=1)
                                 for vb, kb, eg in zip(pick(vbs), pick(kbs), pick(egs))])
            qk += [g[c:] * d for g, d in zip(gram, pick(decs))]
        wq = [jnp.concatenate([so[:, HEAD_DIM:], qh * eg], axis=0) for so, (qh, eg) in zip(sol, qegs)]
        ss = [s_scr[seqs[n], h] for n, h in heads]
        for s in range(nsub):
            pick = lambda lst: lst[s * len(heads):(s + 1) * len(heads)]
            ws = _mm3_each(pick(wq), ss)
            u = [so[:, :HEAD_DIM] - w[:c] for so, w in zip(pick(sol), ws)]
            qku = _mm1_each(pick(qk), u)
            ds = _mm3_each(pick(kdec), u, TN)
            ss = [el * st + d for el, st, d in zip(pick(elast), ss, ds)]
            for (n, h), w, qu in zip(heads, ws, qku):
                sl = slice(h * HEAD_DIM, (h + 1) * HEAD_DIM)
                o = w[c:] + qu
                o = o * lax.rsqrt(jnp.mean(o * o, axis=-1, keepdims=True) + G_NORM_EPS) * ng_ref[...]
                y_ref[seqs[n], s * c:(s + 1) * c, sl] = o * _silu(z_ref[seqs[n], s * c:(s + 1) * c, sl])
        for (n, h), st in zip(heads, ss):
            s_scr[seqs[n], h] = st

        @pl.when(ci == nc - 1)
        def _():
            shift = (rows - (rows - c + tv - (G_CONV - 1))) % rows
            for i, x in zip(seqs, x_l):
                xs = pltpu.roll(x, shift, 0) if shift else x
                cbout_ref[i] = xs[0:G_CONV - 1, :]

    _for_each_group(bb, one_group)

    @pl.when(ci == nc - 1)
    def _():
        sout_ref[...] = s_scr[...]


def _gdn_mixer(pg, cb_all, s_all, l, s_acc, cb_acc, params, *, bb, c, tv, nsub=1):
    b, t, _ = pg.shape
    bb = min(bb, b)
    assert b % bb == 0
    rows = nsub * c
    nc = t // rows
    assert nsub == 1 or tv == c
    kern = functools.partial(_gdn_kernel, bb=bb, c=c, nsub=nsub, tv=tv, nc=nc)
    state_spec = _layer_block((bb, N_HEADS, HEAD_DIM, HEAD_DIM), l)
    cb_spec = _layer_block((bb, G_CONV - 1, G_QKV), l)
    kern, acc_specs, acc_args, aliases = _stacked_outputs(kern, 5 + len(params), (s_acc, cb_acc))
    return pl.pallas_call(
        kern,
        grid=(b // bb, nc),
        in_specs=[pl.BlockSpec((bb, rows, G_QKV), lambda i, j: (i, j, 0)),
                  pl.BlockSpec((bb, rows, G_WIDTH), lambda i, j: (i, j, G_QKV // G_WIDTH)),
                  pl.BlockSpec((bb, rows, LANES), lambda i, j: (i, j, (G_QKV + G_WIDTH) // LANES)),
                  cb_spec, state_spec] + [_full_spec(p.shape) for p in params] + acc_specs,
        out_specs=[pl.BlockSpec((bb, rows, G_WIDTH), lambda i, j: (i, j, 0)), state_spec, cb_spec],
        out_shape=[jax.ShapeDtypeStruct((b, t, G_WIDTH), F32),
                   jax.ShapeDtypeStruct(s_all.shape, F32), jax.ShapeDtypeStruct(cb_all.shape, F32)],
        input_output_aliases=aliases,
        scratch_shapes=[pltpu.VMEM((bb, N_HEADS, HEAD_DIM, HEAD_DIM), F32),
                        pltpu.VMEM((bb, SUBLANES, G_QKV), F32)],
        compiler_params=pltpu.CompilerParams(dimension_semantics=("arbitrary", "arbitrary"),
                                             vmem_limit_bytes=VMEM_LIMIT_BYTES),
        name="gdn",
    )(pg, pg, pg, cb_all, s_all, *params, *acc_args)


def _rel_buckets(dist):
    n = np.maximum(dist, 0)
    nf = np.maximum(n, 1).astype(np.float32)
    large = REL_MAX_EXACT + (np.log(nf / REL_MAX_EXACT) / math.log(REL_MAX_DIST / REL_MAX_EXACT)
                             * (REL_BUCKETS - REL_MAX_EXACT)).astype(np.int32)
    return np.where(n < REL_MAX_EXACT, n, np.minimum(large, REL_BUCKETS - 1)).astype(np.int32)


def _bias_kernel(table_ref, bucket_ref, o_ref):
    bucket = bucket_ref[...]
    for h in range(S_Q_HEADS):
        acc = jnp.full(bucket.shape, NEG_INF, F32)
        for k in range(REL_BUCKETS):
            acc = jnp.where(bucket == k, table_ref[k, h], acc)
        o_ref[h] = acc


def _rel_bias(table, dist, visible=None):
    bucket = _rel_buckets(dist)
    if visible is not None:
        bucket = np.where(visible, bucket, -1).astype(np.int32)
    bucket = jnp.asarray(bucket)
    return pl.pallas_call(
        _bias_kernel,
        in_specs=[pl.BlockSpec(memory_space=pltpu.SMEM), _full_spec(bucket.shape)],
        out_specs=_full_spec((S_Q_HEADS,) + bucket.shape),
        out_shape=jax.ShapeDtypeStruct((S_Q_HEADS,) + bucket.shape, F32),
        grid=(1,),
        name="rel_bias",
    )(table, bucket)


Q_BLOCKS_PER_STEP = 2


def _swa_prompt_kernel(sink_ref, q_ref, kp_ref, kc_ref, vp_ref, vc_ref, bias_ref, o_ref):
    n = pl.program_id(1)
    q = q_ref[...]
    kall = jnp.concatenate([kp_ref[...], kc_ref[...]], axis=0).astype(BF16)
    vall = jnp.concatenate([vp_ref[...], vc_ref[...]], axis=0).astype(BF16)
    chains = [(s, h) for s in range(Q_BLOCKS_PER_STEP) for h in range(S_Q_HEADS)]
    band = lambda a, s, h: a[s * ATTN_BLOCK:(s + 2) * ATTN_BLOCK,
                             (h // S_GROUP) * HEAD_DIM:(h // S_GROUP + 1) * HEAD_DIM]
    from_prev = (lax.broadcasted_iota(jnp.int32, (ATTN_BLOCK, ATTN_BLOCK), 1)
                 > lax.broadcasted_iota(jnp.int32, (ATTN_BLOCK, ATTN_BLOCK), 0))
    has_prev = [jnp.where(n == 0, 0, 1) if s == 0 else 1 for s in range(Q_BLOCKS_PER_STEP)]
    qk = [_dg(q[s * ATTN_BLOCK:(s + 1) * ATTN_BLOCK, h * HEAD_DIM:(h + 1) * HEAD_DIM].astype(BF16),
              band(kall, s, h), NT) for s, h in chains]
    logits = [jnp.where(from_prev, x[:, :ATTN_BLOCK], x[:, ATTN_BLOCK:]) * (HEAD_DIM ** -0.5)
              + bias_ref[has_prev[s], h] for x, (s, h) in zip(qk, chains)]
    m = [jnp.maximum(jnp.max(x, axis=-1, keepdims=True), sink_ref[h]) for x, (s, h) in zip(logits, chains)]
    p = [jnp.exp(x - mm) for x, mm in zip(logits, m)]
    probs = [x / (jnp.sum(x, axis=-1, keepdims=True) + jnp.exp(sink_ref[h] - mm))
             for x, mm, (s, h) in zip(p, m, chains)]
    for x, (s, h) in zip(probs, chains):
        pband = jnp.concatenate([jnp.where(from_prev, x, 0.0), jnp.where(from_prev, 0.0, x)], axis=1)
        o_ref[s * ATTN_BLOCK:(s + 1) * ATTN_BLOCK, h * HEAD_DIM:(h + 1) * HEAD_DIM] = _dg(
            pband.astype(BF16), band(vall, s, h), NN)


def _swa_prompt(ps, sinks, bias):
    b, t, _ = ps.shape
    rows = Q_BLOCKS_PER_STEP * ATTN_BLOCK
    kcol = S_WIDTH // S_KV_WIDTH
    prev = lambda j: jnp.maximum(Q_BLOCKS_PER_STEP * j - 1, 0)
    return pl.pallas_call(
        _swa_prompt_kernel,
        grid=(b, t // rows),
        in_specs=[pl.BlockSpec(memory_space=pltpu.SMEM),
                  pl.BlockSpec((None, rows, S_WIDTH), lambda i, j: (i, j, 0)),
                  pl.BlockSpec((None, ATTN_BLOCK, S_KV_WIDTH), lambda i, j: (i, prev(j), kcol)),
                  pl.BlockSpec((None, rows, S_KV_WIDTH), lambda i, j: (i, j, kcol)),
                  pl.BlockSpec((None, ATTN_BLOCK, S_KV_WIDTH), lambda i, j: (i, prev(j), kcol + 1)),
                  pl.BlockSpec((None, rows, S_KV_WIDTH), lambda i, j: (i, j, kcol + 1)),
                  _full_spec(bias.shape)],
        out_specs=pl.BlockSpec((None, rows, S_WIDTH), lambda i, j: (i, j, 0)),
        out_shape=jax.ShapeDtypeStruct((b, t, S_WIDTH), F32),
        compiler_params=pltpu.CompilerParams(dimension_semantics=("arbitrary", "arbitrary"),
                                             vmem_limit_bytes=VMEM_LIMIT_BYTES),
        name="swa_prompt",
    )(sinks, ps, ps, ps, ps, ps, bias)


def _swa_sample_kernel(sink_ref, q_ref, kn_ref, vn_ref, ck_ref, cv_ref, bias_c_ref, bias_n_ref,
                       k_acc_ref, v_acc_ref, o_ref, cko_ref, cvo_ref, *, bb, tp, tv):
    del k_acc_ref, v_acc_ref
    wc = WINDOW
    step = lambda n: jnp.concatenate([lax.broadcasted_iota(jnp.int32, (tp, n), 0)] * S_GROUP, axis=0)
    ti_c = step(wc)
    kj_c = lax.broadcasted_iota(jnp.int32, (S_GROUP * tp, wc), 1)
    dist_c = wc + ti_c - kj_c
    mask_c = (dist_c >= 0) & (dist_c < WINDOW)
    ti_n = step(tp)
    kj_n = lax.broadcasted_iota(jnp.int32, (S_GROUP * tp, tp), 1)
    dist_n = ti_n - kj_n
    mask_n = (dist_n >= 0) & (dist_n < WINDOW) & (kj_n < tv)

    sinks = [jnp.concatenate([jnp.full((tp, 1), sink_ref[kv * S_GROUP + g], F32) for g in range(S_GROUP)], axis=0)
             for kv in range(S_KV_HEADS)]
    old_rows = lax.broadcasted_iota(jnp.int32, (wc, 1), 0) < wc - tv
    scale = HEAD_DIM ** -0.5

    def one_group(seqs):
        chains = [(n, kv) for n in range(len(seqs)) for kv in range(S_KV_HEADS)]
        kvs = lambda a, kv: a[:, kv * HEAD_DIM:(kv + 1) * HEAD_DIM]
        q_l = [q_ref[i] for i in seqs]
        kn_l = [kn_ref[i] for i in seqs]
        vn_l = [vn_ref[i] for i in seqs]
        ck_l = [ck_ref[i] for i in seqs]
        cv_l = [cv_ref[i] for i in seqs]
        qg = [jnp.concatenate([q_l[n][:, (kv * S_GROUP + g) * HEAD_DIM:(kv * S_GROUP + g + 1) * HEAD_DIM]
                               for g in range(S_GROUP)], axis=0).astype(BF16) for n, kv in chains]
        lc = [_dg(x, kvs(ck_l[n], kv).astype(BF16), NT) for x, (n, kv) in zip(qg, chains)]
        ln = [_dg(x, kvs(kn_l[n], kv).astype(BF16), NT) for x, (n, kv) in zip(qg, chains)]
        lc = [jnp.where(mask_c, x * scale + bias_c_ref[kv], NEG_INF) for x, (n, kv) in zip(lc, chains)]
        ln = [jnp.where(mask_n, x * scale + bias_n_ref[kv], NEG_INF) for x, (n, kv) in zip(ln, chains)]
        m = [jnp.maximum(jnp.maximum(jnp.max(a, axis=-1, keepdims=True), jnp.max(b, axis=-1, keepdims=True)),
                         sinks[kv]) for a, b, (n, kv) in zip(lc, ln, chains)]
        pc = [jnp.exp(a - mm) for a, mm in zip(lc, m)]
        pn = [jnp.exp(a - mm) for a, mm in zip(ln, m)]
        den = [jnp.sum(a, axis=-1, keepdims=True) + jnp.sum(b, axis=-1, keepdims=True) + jnp.exp(sinks[kv] - mm)
               for a, b, mm, (n, kv) in zip(pc, pn, m, chains)]
        oc = [_dg((a / d).astype(BF16), kvs(cv_l[n], kv).astype(BF16), NN) for a, d, (n, kv) in zip(pc, den, chains)]
        on = [_dg((a / d).astype(BF16), kvs(vn_l[n], kv).astype(BF16), NN) for a, d, (n, kv) in zip(pn, den, chains)]
        for a, b, (n, kv) in zip(oc, on, chains):
            og = a + b
            for g in range(S_GROUP):
                hq = kv * S_GROUP + g
                o_ref[seqs[n], :, hq * HEAD_DIM:(hq + 1) * HEAD_DIM] = og[g * tp:(g + 1) * tp, :]
        zpad = jnp.zeros((wc - tp, S_KV_WIDTH), F32)
        for n, i in enumerate(seqs):
            cko_ref[i] = jnp.where(old_rows, pltpu.roll(ck_l[n], wc - tv, 0),
                                   pltpu.roll(jnp.concatenate([kn_l[n], zpad], axis=0), wc - tv, 0))
            cvo_ref[i] = jnp.where(old_rows, pltpu.roll(cv_l[n], wc - tv, 0),
                                   pltpu.roll(jnp.concatenate([vn_l[n], zpad], axis=0), wc - tv, 0))

    _for_each_group(bb, one_group)


def _swa_sample(ps, ck_all, cv_all, l, k_acc, v_acc, sinks, bias_c, bias_n, *, bb, tv):
    b, tp, _ = ps.shape
    bb = min(bb, b)
    assert b % bb == 0
    kcol = S_WIDTH // S_KV_WIDTH
    kern = functools.partial(_swa_sample_kernel, bb=bb, tp=tp, tv=tv)
    kern, acc_specs, acc_args, aliases = _stacked_outputs(kern, 8, (k_acc, v_acc))
    cache_spec = _layer_block((bb, WINDOW, S_KV_WIDTH), l)
    return pl.pallas_call(
        kern,
        grid=(b // bb,),
        in_specs=[pl.BlockSpec(memory_space=pltpu.SMEM),
                  pl.BlockSpec((bb, tp, S_WIDTH), lambda i: (i, 0, 0)),
                  pl.BlockSpec((bb, tp, S_KV_WIDTH), lambda i: (i, 0, kcol)),
                  pl.BlockSpec((bb, tp, S_KV_WIDTH), lambda i: (i, 0, kcol + 1)),
                  cache_spec, cache_spec, _full_spec(bias_c.shape), _full_spec(bias_n.shape)] + acc_specs,
        out_specs=[pl.BlockSpec((bb, tp, S_WIDTH), lambda i: (i, 0, 0)), cache_spec, cache_spec],
        out_shape=[jax.ShapeDtypeStruct((b, tp, S_WIDTH), F32),
                   jax.ShapeDtypeStruct(ck_all.shape, F32), jax.ShapeDtypeStruct(cv_all.shape, F32)],
        input_output_aliases=aliases,
        compiler_params=pltpu.CompilerParams(dimension_semantics=("arbitrary",),
                                             vmem_limit_bytes=VMEM_LIMIT_BYTES),
        name="swa_sample",
    )(sinks, ps, ps, ps, ck_all, cv_all, bias_c, bias_n, *acc_args)


def _prep_layer(l, ffn1_w_in, ffn1_w_out, ln1_g, ln1_b, w_in, rwkv_mu, rwkv_w0, rwkv_w_up, rwkv_a0,
                rwkv_a_up, rwkv_g_up, rwkv_k_k, rwkv_k_a, rwkv_r_k, rwkv_gn_g, rwkv_gn_b, swa_sinks,
                gdn_conv_w, gdn_a_log, gdn_dt_bias, gdn_norm_g, w_out, ln2_g, ln2_b,
                ffn2_w_in, ffn2_w_out, ln3_g, ln3_b):
    row = lambda a: a[l].reshape(1, -1)
    d = w_in.shape[1]
    n_beta = 4 * G_WIDTH
    win = jnp.concatenate([w_in[l], jnp.zeros((d, PROJ_PAD - w_in.shape[2]), F32)], axis=1).astype(BF16)
    lane_tile = lambda a: jnp.zeros((1, LANES), F32).at[0, N_HEADS:2 * N_HEADS].set(a[l])
    del n_beta
    return dict(
        ffn1=(ffn1_w_in[l].astype(BF16), ffn1_w_out[l].astype(BF16), row(ln1_g), row(ln1_b), win),
        rwkv=(row(rwkv_mu), row(rwkv_w0), rwkv_w_up[l], row(rwkv_a0), rwkv_a_up[l], rwkv_g_up[l],
              row(rwkv_k_k), row(rwkv_k_a), row(rwkv_r_k), row(rwkv_gn_g), row(rwkv_gn_b)),
        sinks=swa_sinks[l],
        gdn=(gdn_conv_w[l], lane_tile(gdn_a_log), lane_tile(gdn_dt_bias), row(gdn_norm_g)),
        out=(w_out[l].astype(BF16), row(ln2_g), row(ln2_b), ffn2_w_in[l].astype(BF16),
             ffn2_w_out[l].astype(BF16), row(ln3_g), row(ln3_b)),
    )


def _run_trunk(x, rwkv_s, rwkv_shift, swa_k, swa_v, gdn_s, gdn_conv, layers, biases, *, prompt, tv, alpha,
               tm, n_chunks, bb, bb_rwkv, c, nsub):
    b, t, d = x.shape
    depth = len(layers)
    xf = x.reshape(b * t, d)
    rwkv_shift = rwkv_shift.reshape(depth, b, 1, R_PROJ)
    s_r = sh_r = s_g = cb = None
    kc, vc = ([], []) if prompt else (None, None)
    for l, lp in enumerate(layers):
        x1, pr, ps, pg = _ffn_proj(xf, *lp["ffn1"], alpha=alpha, tm=tm, n_chunks=n_chunks)
        pr = pr.reshape(b, t, R_PROJ)
        ps = ps.reshape(b, t, S_PROJ)
        pg = pg.reshape(b, t, G_PROJ_PAD)
        y_r, s_r, sh_r = _rwkv_mixer(pr, rwkv_shift, rwkv_s, l, s_r, sh_r, lp["rwkv"], bb=bb_rwkv, c=c, tv=tv, nsub=nsub)
        if prompt:
            y_s = _swa_prompt(ps, lp["sinks"], biases[0])
            kc.append(ps[:, t - WINDOW:, S_WIDTH:S_WIDTH + S_KV_WIDTH])
            vc.append(ps[:, t - WINDOW:, S_WIDTH + S_KV_WIDTH:])
        else:
            y_s, kc, vc = _swa_sample(ps, swa_k, swa_v, l, kc, vc, lp["sinks"], biases[1], biases[2], bb=bb, tv=tv)
        y_g, s_g, cb = _gdn_mixer(pg, gdn_conv, gdn_s, l, s_g, cb, lp["gdn"], bb=bb, c=c, tv=tv, nsub=nsub)
        xf = _out_ffn(x1, y_r.reshape(b * t, R_WIDTH), y_s.reshape(b * t, S_WIDTH), y_g.reshape(b * t, G_WIDTH),
                      *lp["out"], alpha=alpha, tm=tm, n_chunks=n_chunks)
    if prompt:
        kc, vc = jnp.stack(kc, axis=0), jnp.stack(vc, axis=0)
    cache_shape = (depth, b, WINDOW, S_KV_HEADS, HEAD_DIM)
    return xf.reshape(b, t, d), [s_r, sh_r.reshape(depth, b, R_PROJ), kc.reshape(cache_shape),
                                 vc.reshape(cache_shape), s_g, cb]


def kernel(x_prompt, x_sample, state_rwkv, state_rwkv_shift, cache_swa_k, cache_swa_v, state_gdn, state_gdn_conv, ffn1_w_in, ffn1_w_out, ln1_g, ln1_b, w_in, rwkv_mu, rwkv_w0, rwkv_w_up, rwkv_a0, rwkv_a_up, rwkv_g_up, rwkv_k_k, rwkv_k_a, rwkv_r_k, rwkv_gn_g, rwkv_gn_b, swa_sinks, rel_table, gdn_conv_w, gdn_a_log, gdn_dt_bias, gdn_norm_g, w_out, ln2_g, ln2_b, ffn2_w_in, ffn2_w_out, ln3_g, ln3_b):
    depth = ffn1_w_in.shape[0]
    alpha = (2 * depth) ** 0.25
    layers = [_prep_layer(l, ffn1_w_in, ffn1_w_out, ln1_g, ln1_b, w_in, rwkv_mu, rwkv_w0, rwkv_w_up, rwkv_a0,
                          rwkv_a_up, rwkv_g_up, rwkv_k_k, rwkv_k_a, rwkv_r_k, rwkv_gn_g, rwkv_gn_b, swa_sinks,
                          gdn_conv_w, gdn_a_log, gdn_dt_bias, gdn_norm_g, w_out, ln2_g, ln2_b,
                          ffn2_w_in, ffn2_w_out, ln3_g, ln3_b) for l in range(depth)]
    bp, tp_len, d = x_prompt.shape
    bs, ts, _ = x_sample.shape
    ts_pad = -(-ts // SUBLANES) * SUBLANES

    assert WINDOW == ATTN_BLOCK
    qi = np.arange(ATTN_BLOCK)[:, None]
    kc = np.arange(ATTN_BLOCK)[None, :]
    dist_p = np.where(kc > qi, ATTN_BLOCK + qi - kc, qi - kc)
    bias_p = jnp.stack([_rel_bias(rel_table, dist_p, kc <= qi), _rel_bias(rel_table, dist_p)], axis=0)
    ti = (np.arange(S_GROUP * ts_pad) % ts_pad)[:, None]
    wc = cache_swa_k.shape[2]
    bias_c = _rel_bias(rel_table, wc + ti - np.arange(wc)[None, :])
    bias_n = _rel_bias(rel_table, ti - np.arange(ts_pad)[None, :])
    regroup = lambda a: jnp.stack([jnp.concatenate([a[kv * S_GROUP + g, g * ts_pad:(g + 1) * ts_pad]
                                                    for g in range(S_GROUP)], axis=0)
                                   for kv in range(S_KV_HEADS)], axis=0)
    biases = (bias_p, regroup(bias_c), regroup(bias_n))

    zeros = lambda *s: jnp.zeros((depth, bp) + s, F32)
    y_prompt, p_states = _run_trunk(
        x_prompt, zeros(N_HEADS, HEAD_DIM, HEAD_DIM), zeros(R_PROJ), None, None,
        zeros(N_HEADS, HEAD_DIM, HEAD_DIM), zeros(G_CONV - 1, G_QKV), layers, biases,
        prompt=True, tv=64, alpha=alpha, tm=512, n_chunks=2, bb=bp, bb_rwkv=bp, c=64, nsub=2)

    xs = jnp.concatenate([x_sample, jnp.zeros((bs, ts_pad - ts, d), F32)], axis=1)
    ck = cache_swa_k.reshape(depth, bs, wc, S_KV_WIDTH)
    cv = cache_swa_v.reshape(depth, bs, wc, S_KV_WIDTH)
    y_sample, s_states = _run_trunk(
        xs, state_rwkv, state_rwkv_shift, ck, cv, state_gdn, state_gdn_conv, layers, biases,
        prompt=False, tv=ts, alpha=alpha, tm=512, n_chunks=2, bb=8, bb_rwkv=16, c=ts_pad, nsub=1)
    return (y_prompt, y_sample[:, :ts]) + tuple(p_states) + tuple(s_states)
```

```python
import functools
import math

import numpy as np
import jax
import jax.numpy as jnp
from jax import lax
from jax.experimental import pallas as pl
from jax.experimental.pallas import tpu as pltpu

F32 = jnp.float32
BF16 = jnp.bfloat16

HEAD_DIM = 64
N_HEADS = 4
R_WIDTH = N_HEADS * HEAD_DIM
R_PROJ = 896
S_Q_HEADS = 8
S_KV_HEADS = 2
S_GROUP = S_Q_HEADS // S_KV_HEADS
S_WIDTH = S_Q_HEADS * HEAD_DIM
S_KV_WIDTH = S_KV_HEADS * HEAD_DIM
S_PROJ = S_WIDTH + 2 * S_KV_WIDTH
G_WIDTH = N_HEADS * HEAD_DIM
G_CONV = 4
G_QKV = 3 * G_WIDTH
G_PROJ_PAD = 4 * G_WIDTH + 128
WINDOW = 128
ATTN_BLOCK = 128
REL_BUCKETS = 32
REL_MAX_EXACT = 16
REL_MAX_DIST = 128
NEG_INF = -1e30
R_GN_EPS = 64e-5
G_NORM_EPS = 1e-6
LN_EPS = 1e-5
L2_EPS = 1e-6
PROJ_PAD = R_PROJ + S_PROJ + G_PROJ_PAD

LANES = 128
SUBLANES = 8
VMEM_LIMIT_BYTES = 56 * 1024 * 1024

NN = ((1,), (0,))
NT = ((1,), (1,))
TN = ((0,), (0,))


def _dg(a, b, dims):
    return lax.dot_general(a, b, (dims, ((), ())), preferred_element_type=F32)


def _split2(a):
    hi = a.astype(BF16)
    lo = (a - hi.astype(F32)).astype(BF16)
    return hi, lo


def _mm_exact_lhs(a_bf16, b, dims=NN):
    b1 = b.astype(BF16)
    r1 = b - b1.astype(F32)
    b2 = r1.astype(BF16)
    b3 = (r1 - b2.astype(F32)).astype(BF16)
    return _dg(a_bf16, b1, dims) + (_dg(a_bf16, b2, dims) + _dg(a_bf16, b3, dims))


def _sigmoid(x):
    return 1.0 / (1.0 + jnp.exp(-x))


def _silu(x):
    return x * _sigmoid(x)


def _softplus(x):
    return jnp.maximum(x, 0.0) + jnp.log(1.0 + jnp.exp(-jnp.abs(x)))


def _layer_norm(z, g, b):
    mu = jnp.mean(z, axis=-1, keepdims=True)
    zc = z - mu
    var = jnp.mean(zc * zc, axis=-1, keepdims=True)
    return zc * lax.rsqrt(var + LN_EPS) * g + b


def _mm3_each(a_list, b_list, dims=NN):
    sa = [_split2(a) for a in a_list]
    sb = [_split2(b) for b in b_list]
    return [_dg(ah, bh, dims) + (_dg(ah, bl, dims) + _dg(al, bh, dims))
            for (ah, al), (bh, bl) in zip(sa, sb)]


def _mm1_each(a_list, b_list, dims=NN):
    return [_dg(a.astype(BF16), b.astype(BF16), dims) for a, b in zip(a_list, b_list)]


def _bf16_each(a_list):
    return [a.astype(BF16) for a in a_list]


def _tri_inverse_each(m_list, c):
    row = lax.broadcasted_iota(jnp.int32, (c, c), 0)
    col = lax.broadcasted_iota(jnp.int32, (c, c), 1)
    eye = jnp.where(row == col, 1.0, 0.0).astype(F32)
    t = [eye + m for m in m_list]
    mp = _bf16_each(m_list)
    span = 2
    while span < c:
        mp = _bf16_each(_mm1_each(mp, mp))
        t = [a + b for a, b in zip(t, _mm1_each(t, mp))]
        span *= 2
    return t


def _for_each_group(bb, one_group):
    one_group(list(range(bb)))


def _full_spec(shape):
    nd = len(shape)
    return pl.BlockSpec(shape, lambda *_: (0,) * nd)


def _resident_spec(shape, index_map):
    return pl.BlockSpec(shape, index_map, pipeline_mode=pl.Buffered(1))


SUB_ROWS = 256


def _row_tiles(tm):
    n = max(tm // SUB_ROWS, 1)
    return [slice(i * (tm // n), (i + 1) * (tm // n)) for i in range(n)]


def _swiglu_each(xb_l, wi_ref, wo_ref, d_ff, n_chunks):
    cw = d_ff // n_chunks
    acc = [None] * len(xb_l)
    for j in range(n_chunks):
        gate = [_dg(xb, wi_ref[:, j * cw:(j + 1) * cw], NN) for xb in xb_l]
        up = [_dg(xb, wi_ref[:, d_ff + j * cw:d_ff + (j + 1) * cw], NN) for xb in xb_l]
        act = [(_silu(g) * u).astype(BF16) for g, u in zip(gate, up)]
        part = [_dg(a, wo_ref[j * cw:(j + 1) * cw, :], NN) for a in act]
        acc = [p if a is None else a + p for a, p in zip(acc, part)]
    return acc


def _ffn_proj_kernel(x_ref, wi_ref, wo_ref, g_ref, b_ref, win_ref,
                     x1_ref, pr_ref, ps_ref, pg_ref, *, alpha, d_ff, n_chunks, tm):
    tiles = _row_tiles(tm)
    x_l = [x_ref[r, :] for r in tiles]
    y_l = _swiglu_each([x.astype(BF16) for x in x_l], wi_ref, wo_ref, d_ff, n_chunks)
    x1_l = [_layer_norm(alpha * x + 0.5 * y, g_ref[...], b_ref[...]) for x, y in zip(x_l, y_l)]
    p_l = [_dg(x1.astype(BF16), win_ref[...], NN) for x1 in x1_l]
    for r, x1, p in zip(tiles, x1_l, p_l):
        x1_ref[r, :] = x1
        pr_ref[r, :] = p[:, :R_PROJ]
        ps_ref[r, :] = p[:, R_PROJ:R_PROJ + S_PROJ]
        pg_ref[r, :] = p[:, R_PROJ + S_PROJ:]


def _ffn_proj(x, wi, wo, g, b, win, *, alpha, tm, n_chunks):
    m, d = x.shape
    tm = min(tm, m)
    assert m % tm == 0
    d_ff = wo.shape[0]
    kern = functools.partial(_ffn_proj_kernel, alpha=alpha, d_ff=d_ff, n_chunks=n_chunks, tm=tm)
    row = lambda w: pl.BlockSpec((tm, w), lambda i: (i, 0))
    const = lambda i: (0, 0)
    return pl.pallas_call(
        kern,
        grid=(m // tm,),
        in_specs=[row(d),
                  _resident_spec(wi.shape, const), _resident_spec(wo.shape, const),
                  _resident_spec(g.shape, const), _resident_spec(b.shape, const),
                  _resident_spec(win.shape, const)],
        out_specs=[row(d), row(R_PROJ), row(S_PROJ), row(G_PROJ_PAD)],
        out_shape=[jax.ShapeDtypeStruct((m, d), F32), jax.ShapeDtypeStruct((m, R_PROJ), F32),
                   jax.ShapeDtypeStruct((m, S_PROJ), F32), jax.ShapeDtypeStruct((m, G_PROJ_PAD), F32)],
        compiler_params=pltpu.CompilerParams(dimension_semantics=("arbitrary",),
                                             vmem_limit_bytes=VMEM_LIMIT_BYTES),
        name="ffn_proj",
    )(x, wi, wo, g, b, win)


def _out_ffn_kernel(x_ref, yr_ref, ys_ref, yg_ref, wout_ref, g2_ref, b2_ref, wi_ref, wo_ref, g3_ref, b3_ref,
                    o_ref, *, alpha, d_ff, n_chunks, tm):
    tiles = _row_tiles(tm)
    x_l = [x_ref[r, :] for r in tiles]
    mix_l = [(_dg(yr_ref[r, :].astype(BF16), wout_ref[0:R_WIDTH, :], NN)
              + _dg(ys_ref[r, :].astype(BF16), wout_ref[R_WIDTH:R_WIDTH + S_WIDTH, :], NN)
              + _dg(yg_ref[r, :].astype(BF16), wout_ref[R_WIDTH + S_WIDTH:, :], NN)) for r in tiles]
    x2_l = [_layer_norm(alpha * x + mix, g2_ref[...], b2_ref[...]) for x, mix in zip(x_l, mix_l)]
    y_l = _swiglu_each([x2.astype(BF16) for x2 in x2_l], wi_ref, wo_ref, d_ff, n_chunks)
    for r, x2, y in zip(tiles, x2_l, y_l):
        o_ref[r, :] = _layer_norm(alpha * x2 + 0.5 * y, g3_ref[...], b3_ref[...])


def _out_ffn(x, yr, ys, yg, wout, g2, b2, wi, wo, g3, b3, *, alpha, tm, n_chunks):
    m, d = x.shape
    tm = min(tm, m)
    assert m % tm == 0
    d_ff = wo.shape[0]
    kern = functools.partial(_out_ffn_kernel, alpha=alpha, d_ff=d_ff, n_chunks=n_chunks, tm=tm)
    row = lambda w: pl.BlockSpec((tm, w), lambda i: (i, 0))
    const = lambda i: (0, 0)
    res = lambda a: _resident_spec(a.shape, const)
    return pl.pallas_call(
        kern,
        grid=(m // tm,),
        in_specs=[row(d), row(R_WIDTH), row(S_WIDTH), row(G_WIDTH),
                  res(wout), res(g2), res(b2), res(wi), res(wo), res(g3), res(b3)],
        out_specs=row(d),
        out_shape=jax.ShapeDtypeStruct((m, d), F32),
        compiler_params=pltpu.CompilerParams(dimension_semantics=("arbitrary",),
                                             vmem_limit_bytes=VMEM_LIMIT_BYTES),
        name="out_ffn",
    )(x, yr, ys, yg, wout, g2, b2, wi, wo, g3, b3)


def _rwkv_kernel(f_ref, shift_ref, s0_ref, mu_ref, w0_ref, wup_ref, a0_ref, aup_ref, gup_ref,
                 kk_ref, ka_ref, rk_ref, gng_ref, gnb_ref, s_acc_ref, sh_acc_ref,
                 y_ref, sout_ref, shout_ref, s_scr, prev_scr, *, bb, c, nsub, tv, nc):
    del s_acc_ref, sh_acc_ref
    ci = pl.program_id(1)

    @pl.when(ci == 0)
    def _():
        s_scr[...] = s0_ref[...]
        prev_scr[...] = shift_ref[...]

    row = lax.broadcasted_iota(jnp.int32, (c, 1), 0)
    ri = lax.broadcasted_iota(jnp.int32, (c, 2 * c), 0)
    cj = lax.broadcasted_iota(jnp.int32, (c, 2 * c), 1)
    cj = jnp.where(cj >= c, cj - c, cj)
    strict = ri > cj
    incl = ri >= cj
    tri = (lax.broadcasted_iota(jnp.int32, (c, c), 0) >= lax.broadcasted_iota(jnp.int32, (c, c), 1))
    tri = jnp.where(tri, 1.0, 0.0).astype(BF16)
    valid = row < tv

    rows = nsub * c
    first_row = lax.broadcasted_iota(jnp.int32, (rows, 1), 0) == 0

    def one_group(seqs):
        n_seq = len(seqs)
        heads = [(n, h) for n in range(n_seq) for h in range(N_HEADS)]
        hs = lambda arr, h: arr[:, h * HEAD_DIM:(h + 1) * HEAD_DIM]
        fl = [f_ref[i] for i in seqs]
        prevs = [jnp.where(first_row, prev_scr[i], pltpu.roll(f, 1, 0)) for i, f in zip(seqs, fl)]
        for i, f in zip(seqs, fl):
            prev_scr[i] = f[rows - 1:rows, :]

        fsl = [f + (p - f) * mu_ref[...] for f, p in zip(fl, prevs)]
        w_l = _mm3_each([jnp.tanh(fs[:, 768:800]) for fs in fsl], [wup_ref[...]] * n_seq)
        lr_l = _mm3_each([fs[:, 800:832] for fs in fsl], [aup_ref[...]] * n_seq)
        gate_l = _mm3_each([_sigmoid(fs[:, 832:896]) for fs in fsl], [gup_ref[...]] * n_seq)
        lr_l = [_sigmoid(a0_ref[...] + x) for x in lr_l]
        lw_l = [-jnp.exp(-_softplus(-(w0_ref[...] + x)) - 0.5) for x in w_l]
        r_l = [fs[:, 0:R_WIDTH] for fs in fsl]
        k_l = [fs[:, R_WIDTH:2 * R_WIDTH] for fs in fsl]
        v_l = [fs[:, 2 * R_WIDTH:3 * R_WIDTH] for fs in fsl]
        kkx_l = [k * kk_ref[...] for k in k_l]
        k2_l = [k * (1.0 + (lr - 1.0) * ka_ref[...]) for k, lr in zip(k_l, lr_l)]
        if tv < c:
            lw_l = [jnp.where(valid, x, 0.0) for x in lw_l]
            kkx_l = [jnp.where(valid, x, 0.0) for x in kkx_l]
            k2_l = [jnp.where(valid, x, 0.0) for x in k2_l]
        cum_l = [jnp.concatenate([_mm_exact_lhs(tri, lw[s * c:(s + 1) * c]) for s in range(nsub)], axis=0)
                 if nsub > 1 else _mm_exact_lhs(tri, lw) for lw in lw_l]
        p_in_l = [jnp.exp(x) for x in cum_l]
        p_prev_l = [jnp.exp(x - lw) for x, lw in zip(cum_l, lw_l)]
        p_inv_l = [jnp.exp(-x) for x in cum_l]

        def intra(s):
            hs = lambda arr, h: arr[s * c:(s + 1) * c, h * HEAD_DIM:(h + 1) * HEAD_DIM]
            xs, y2s, vhs, plast, rk2 = [], [], [], [], []
            for n, h in heads:
                kx = hs(kkx_l[n], h)
                kkh = kx * lax.rsqrt(jnp.sum(kx * kx, axis=-1, keepdims=True) + L2_EPS)
                at = -kkh * hs(p_prev_l[n], h)
                bt = kkh * hs(lr_l[n], h) * hs(p_inv_l[n], h)
                kt = hs(k2_l[n], h) * hs(p_inv_l[n], h)
                qt = hs(r_l[n], h) * hs(p_in_l[n], h)
                xs.append(jnp.concatenate([at, qt], axis=0))
                y2s.append(jnp.concatenate([bt, kt], axis=0))
                vhs.append(hs(v_l[n], h))
                plast.append(hs(p_in_l[n], h)[c - 1:c])
                rk2.append(hs(r_l[n], h) * hs(k2_l[n], h))
            xs = _bf16_each(xs)
            y2s = _bf16_each(y2s)
            gram = _mm1_each(xs, y2s, NT)
            xa = [jnp.where(strict, g[:c], 0.0) for g in gram]
            xq = [jnp.where(incl, g[c:], 0.0) for g in gram]
            lakv = _mm1_each([x[:, c:] for x in xa], vhs)
            t = _tri_inverse_each([x[:, :c] for x in xa], c)
            gate = [hs(gate_l[n], h) for n, h in heads]
            return dict(xs=xs, y2s=y2s, vhs=vhs, xq=xq, lakv=lakv, t=t, plast=plast, rk2=rk2, gate=gate)

        parts = [intra(s) for s in range(nsub)]
        ss = [s_scr[seqs[n], h] for n, h in heads]
        for s, p in enumerate(parts):
            xst = _mm1_each(p["xs"], ss, NT)
            u = _mm1_each(p["t"], [a[:c] + b for a, b in zip(xst, p["lakv"])])
            uv = _bf16_each([jnp.concatenate([a, b], axis=0) for a, b in zip(u, p["vhs"])])
            o_l = [a[c:] + b for a, b in zip(xst, _mm1_each(p["xq"], uv))]
            ds = _mm1_each(uv, p["y2s"], TN)
            ss = [(st + d) * pl_ for st, d, pl_ in zip(ss, ds, p["plast"])]
            for (n, h), o, vh, rk2, gate in zip(heads, o_l, p["vhs"], p["rk2"], p["gate"]):
                sl = slice(h * HEAD_DIM, (h + 1) * HEAD_DIM)
                mean = jnp.mean(o, axis=-1, keepdims=True)
                oc = o - mean
                var = jnp.mean(oc * oc, axis=-1, keepdims=True)
                on = oc * lax.rsqrt(var + R_GN_EPS) * gng_ref[:, sl] + gnb_ref[:, sl]
                bonus = jnp.sum(rk2 * rk_ref[:, sl], axis=-1, keepdims=True) * vh
                y_ref[seqs[n], s * c:(s + 1) * c, sl] = (on + bonus) * gate
        for (n, h), st in zip(heads, ss):
            s_scr[seqs[n], h] = st

        @pl.when(ci == nc - 1)
        def _():
            last = rows - c + tv
            for i, f in zip(seqs, fl):
                shout_ref[i] = f[last - 1:last, :]

    _for_each_group(bb, one_group)

    @pl.when(ci == nc - 1)
    def _():
        sout_ref[...] = s_scr[...]


def _layer_block(shape, l):
    nd = len(shape)
    return pl.BlockSpec((None,) + tuple(shape), lambda i, j=0: (l, i) + (0,) * (nd - 1))


_ANY_SPEC = pl.BlockSpec(memory_space=pl.ANY)


def _stacked_outputs(kern, n_in, accs):
    if accs[0] is None:
        def first(*refs):
            return kern(*refs[:n_in], None, None, *refs[n_in:])
        return first, [], [], {}
    return kern, [_ANY_SPEC, _ANY_SPEC], list(accs), {n_in: 1, n_in + 1: 2}


def _rwkv_mixer(f, shift_all, s_all, l, s_acc, sh_acc, params, *, bb, c, tv, nsub=1):
    b, t, _ = f.shape
    bb = min(bb, b)
    assert b % bb == 0
    rows = nsub * c
    nc = t // rows
    assert nsub == 1 or tv == c
    kern = functools.partial(_rwkv_kernel, bb=bb, c=c, nsub=nsub, tv=tv, nc=nc)
    state_spec = _layer_block((bb, N_HEADS, HEAD_DIM, HEAD_DIM), l)
    shift_spec = _layer_block((bb, 1, R_PROJ), l)
    kern, acc_specs, acc_args, aliases = _stacked_outputs(kern, 3 + len(params), (s_acc, sh_acc))
    return pl.pallas_call(
        kern,
        grid=(b // bb, nc),
        in_specs=[pl.BlockSpec((bb, rows, R_PROJ), lambda i, j: (i, j, 0)), shift_spec, state_spec]
                 + [_full_spec(p.shape) for p in params] + acc_specs,
        out_specs=[pl.BlockSpec((bb, rows, R_WIDTH), lambda i, j: (i, j, 0)), state_spec, shift_spec],
        out_shape=[jax.ShapeDtypeStruct((b, t, R_WIDTH), F32),
                   jax.ShapeDtypeStruct(s_all.shape, F32), jax.ShapeDtypeStruct(shift_all.shape, F32)],
        input_output_aliases=aliases,
        scratch_shapes=[pltpu.VMEM((bb, N_HEADS, HEAD_DIM, HEAD_DIM), F32),
                        pltpu.VMEM((bb, 1, R_PROJ), F32)],
        compiler_params=pltpu.CompilerParams(dimension_semantics=("arbitrary", "arbitrary"),
                                             vmem_limit_bytes=VMEM_LIMIT_BYTES),
        name="rwkv7",
    )(f, shift_all, s_all, *params, *acc_args)


def _gdn_kernel(x_ref, z_ref, gb_ref, cb_ref, s0_ref, cw_ref, alog_ref, dtb_ref, ng_ref, s_acc_ref, cb_acc_ref,
                y_ref, sout_ref, cbout_ref, s_scr, tail_scr, *, bb, c, nsub, tv, nc):
    del s_acc_ref, cb_acc_ref
    ci = pl.program_id(1)

    @pl.when(ci == 0)
    def _():
        s_scr[...] = s0_ref[...]
        tail_scr[...] = jnp.zeros_like(tail_scr)
        tail_scr[:, SUBLANES - (G_CONV - 1):SUBLANES, :] = cb_ref[...]

    row = lax.broadcasted_iota(jnp.int32, (c, 1), 0)
    ri = lax.broadcasted_iota(jnp.int32, (c, c), 0)
    cj = lax.broadcasted_iota(jnp.int32, (c, c), 1)
    strict = ri > cj
    incl = ri >= cj
    tri = jnp.where(incl, 1.0, 0.0).astype(BF16)
    valid = row < tv

    rows = nsub * c

    def one_group(seqs):
        n_seq = len(seqs)
        heads = [(n, h) for n in range(n_seq) for h in range(N_HEADS)]
        chains = [(n, s, h) for s in range(nsub) for n, h in heads]
        sub = lambda a, s: a[s * c:(s + 1) * c]
        x_l = [x_ref[i] for i in seqs]
        act_l = []
        for i, x in zip(seqs, x_l):
            xe = jnp.concatenate([tail_scr[i], x], axis=0)
            conv = x * cw_ref[G_CONV - 1:G_CONV, :]
            for s in range(1, G_CONV):
                conv = conv + pltpu.roll(xe, s, 0)[SUBLANES:, :] * cw_ref[G_CONV - 1 - s:G_CONV - s, :]
            tail_scr[i] = x[rows - SUBLANES:, :]
            act_l.append(_silu(conv))
        gb_l = [gb_ref[i] for i in seqs]
        beta_l = [_sigmoid(g) for g in gb_l]
        g_l = [-jnp.exp(alog_ref[...]) * _softplus(g + dtb_ref[...]) for g in gb_l]
        if tv < c:
            beta_l = [jnp.where(valid, x, 0.0) for x in beta_l]
            g_l = [jnp.where(valid, x, 0.0) for x in g_l]
        pad = lambda g: jnp.concatenate([g, jnp.zeros((LANES - c, LANES), F32)], axis=0) if c < LANES else g
        gcum = {(n, s): _mm_exact_lhs(tri, sub(g_l[n], s)) for s in range(nsub) for n in range(n_seq)}
        gcum_t = {k: pad(g).T for k, g in gcum.items()}
        qegs, khs, vbs, kbs, egs, decs, kdec, elast = [], [], [], [], [], [], [], []
        for n, s, h in chains:
            act = sub(act_l[n], s)
            qx = act[:, h * HEAD_DIM:(h + 1) * HEAD_DIM]
            kx = act[:, G_WIDTH + h * HEAD_DIM:G_WIDTH + (h + 1) * HEAD_DIM]
            vh = act[:, 2 * G_WIDTH + h * HEAD_DIM:2 * G_WIDTH + (h + 1) * HEAD_DIM]
            qh = qx * lax.rsqrt(jnp.sum(qx * qx, axis=-1, keepdims=True) + L2_EPS) * (HEAD_DIM ** -0.5)
            kh = kx * lax.rsqrt(jnp.sum(kx * kx, axis=-1, keepdims=True) + L2_EPS)
            if tv < c:
                kh = jnp.where(valid, kh, 0.0)
            beta = sub(beta_l[n], s)[:, h:h + 1]
            gc = gcum[n, s]
            gcol = gc[:, N_HEADS + h:N_HEADS + h + 1]
            grow = gcum_t[n, s][N_HEADS + h:N_HEADS + h + 1, 0:c]
            glast = gc[c - 1:c, N_HEADS + h:N_HEADS + h + 1]
            decs.append(jnp.exp(jnp.where(incl, gcol - grow, NEG_INF)))
            eg = jnp.exp(gcol)
            qegs.append((qh, eg))
            khs.append(kh)
            kbs.append(kh * beta)
            vbs.append(vh * beta)
            egs.append(eg)
            kdec.append(kh * jnp.exp(glast - gcol))
            elast.append(jnp.exp(glast))
        sol, qk = [], []
        for s in range(nsub):
            pick = lambda lst: lst[s * len(heads):(s + 1) * len(heads)]
            gram = _mm1_each([jnp.concatenate([kb, qh], axis=0) for kb, (qh, _) in zip(pick(kbs), pick(qegs))],
                             pick(khs), NT)
            t = _tri_inverse_each([-jnp.where(strict, g[:c] * d, 0.0) for g, d in zip(gram, pick(decs))], c)
            sol += _mm3_each(t, [jnp.concatenate([vb, kb * eg], axis=1)
                                 for vb, kb, eg in zip(pick(vbs), pick(kbs), pick(egs))])
            qk += [g[c:] * d for g, d in zip(gram, pick(decs))]
        wq = [jnp.concatenate([so[:, HEAD_DIM:], qh * eg], axis=0) for so, (qh, eg) in zip(sol, qegs)]
        ss = [s_scr[seqs[n], h] for n, h in heads]
        for s in range(nsub):
            pick = lambda lst: lst[s * len(heads):(s + 1) * len(heads)]
            ws = _mm3_each(pick(wq), ss)
            u = [so[:, :HEAD_DIM] - w[:c] for so, w in zip(pick(sol), ws)]
            qku = _mm1_each(pick(qk), u)
            ds = _mm3_each(pick(kdec), u, TN)
            ss = [el * st + d for el, st, d in zip(pick(elast), ss, ds)]
            for (n, h), w, qu in zip(heads, ws, qku):
                sl = slice(h * HEAD_DIM, (h + 1) * HEAD_DIM)
                o = w[c:] + qu
                o = o * lax.rsqrt(jnp.mean(o * o, axis=-1, keepdims=True) + G_NORM_EPS) * ng_ref[...]
                y_ref[seqs[n], s * c:(s + 1) * c, sl] = o * _silu(z_ref[seqs[n], s * c:(s + 1) * c, sl])
        for (n, h), st in zip(heads, ss):
            s_scr[seqs[n], h] = st

        @pl.when(ci == nc - 1)
        def _():
            shift = (rows - (rows - c + tv - (G_CONV - 1))) % rows
            for i, x in zip(seqs, x_l):
                xs = pltpu.roll(x, shift, 0) if shift else x
                cbout_ref[i] = xs[0:G_CONV - 1, :]

    _for_each_group(bb, one_group)

    @pl.when(ci == nc - 1)
    def _():
        sout_ref[...] = s_scr[...]


def _gdn_mixer(pg, cb_all, s_all, l, s_acc, cb_acc, params, *, bb, c, tv, nsub=1):
    b, t, _ = pg.shape
    bb = min(bb, b)
    assert b % bb == 0
    rows = nsub * c
    nc = t // rows
    assert nsub == 1 or tv == c
    kern = functools.partial(_gdn_kernel, bb=bb, c=c, nsub=nsub, tv=tv, nc=nc)
    state_spec = _layer_block((bb, N_HEADS, HEAD_DIM, HEAD_DIM), l)
    cb_spec = _layer_block((bb, G_CONV - 1, G_QKV), l)
    kern, acc_specs, acc_args, aliases = _stacked_outputs(kern, 5 + len(params), (s_acc, cb_acc))
    return pl.pallas_call(
        kern,
        grid=(b // bb, nc),
        in_specs=[pl.BlockSpec((bb, rows, G_QKV), lambda i, j: (i, j, 0)),
                  pl.BlockSpec((bb, rows, G_WIDTH), lambda i, j: (i, j, G_QKV // G_WIDTH)),
                  pl.BlockSpec((bb, rows, LANES), lambda i, j: (i, j, (G_QKV + G_WIDTH) // LANES)),
                  cb_spec, state_spec] + [_full_spec(p.shape) for p in params] + acc_specs,
        out_specs=[pl.BlockSpec((bb, rows, G_WIDTH), lambda i, j: (i, j, 0)), state_spec, cb_spec],
        out_shape=[jax.ShapeDtypeStruct((b, t, G_WIDTH), F32),
                   jax.ShapeDtypeStruct(s_all.shape, F32), jax.ShapeDtypeStruct(cb_all.shape, F32)],
        input_output_aliases=aliases,
        scratch_shapes=[pltpu.VMEM((bb, N_HEADS, HEAD_DIM, HEAD_DIM), F32),
                        pltpu.VMEM((bb, SUBLANES, G_QKV), F32)],
        compiler_params=pltpu.CompilerParams(dimension_semantics=("arbitrary", "arbitrary"),
                                             vmem_limit_bytes=VMEM_LIMIT_BYTES),
        name="gdn",
    )(pg, pg, pg, cb_all, s_all, *params, *acc_args)


def _to_feature_major(a, width):
    return jnp.concatenate([a[:, i * LANES:(i + 1) * LANES].T for i in range(width // LANES)], axis=0)


def _to_batch_major(a, width):
    return jnp.concatenate([a[i * LANES:(i + 1) * LANES, :].T for i in range(width // LANES)], axis=1)


def _gdn_sample_kernel(alog_ref, dtb_ref, x_ref, cb_ref, s_ref, cwt_ref, ng_ref, s_acc_ref, cb_acc_ref,
                       y_ref, sout_ref, cbout_ref, kq_scr, *, tv, t_pad):
    del s_acc_ref, cb_acc_ref
    nb = x_ref.shape[1]
    xs = [x_ref[t] for t in range(tv)]
    full = ([_to_feature_major(cb_ref[j], G_QKV) for j in range(G_CONV - 1)]
            + [_to_feature_major(x[:, :G_QKV], G_QKV) for x in xs])
    zt = [_to_feature_major(x[:, G_QKV:G_QKV + G_WIDTH], G_WIDTH) for x in xs]
    gbt = [x[:, G_QKV + G_WIDTH:].T for x in xs]
    wcol = [jnp.broadcast_to(cwt_ref[:, j:j + 1], (G_QKV, nb)) for j in range(G_CONV)]
    zero = jnp.zeros((HEAD_DIM, nb), F32)
    for t in range(tv):
        conv = full[t] * wcol[0]
        for j in range(1, G_CONV):
            conv = conv + full[t + j] * wcol[j]
        act = _silu(conv)
        y_heads = []
        for h in range(N_HEADS):
            hs = slice(h * HEAD_DIM, (h + 1) * HEAD_DIM)
            qx = act[h * HEAD_DIM:(h + 1) * HEAD_DIM]
            kx = act[G_WIDTH + h * HEAD_DIM:G_WIDTH + (h + 1) * HEAD_DIM]
            vh = act[2 * G_WIDTH + h * HEAD_DIM:2 * G_WIDTH + (h + 1) * HEAD_DIM]
            qh = qx * lax.rsqrt(jnp.sum(qx * qx, axis=0, keepdims=True) + L2_EPS) * (HEAD_DIM ** -0.5)
            kh = kx * lax.rsqrt(jnp.sum(kx * kx, axis=0, keepdims=True) + L2_EPS)
            beta = _sigmoid(gbt[t][h:h + 1, :])
            neg_a = -jnp.exp(jnp.full((1, nb), alog_ref[h], F32))
            eg = jnp.exp(neg_a * _softplus(gbt[t][N_HEADS + h:N_HEADS + h + 1, :] + dtb_ref[h]))
            slot = t * N_HEADS + h
            kq_scr[slot, 0] = kh
            kq_scr[slot, 1] = qh
            src = s_ref if t == 0 else sout_ref

            def pass1(k, acc, h=h, slot=slot, src=src):
                return acc + src[h, k] * kq_scr[slot, 0, pl.ds(k, 1), :]

            stk = lax.fori_loop(0, HEAD_DIM, pass1, zero, unroll=8)
            u = beta * (vh - eg * stk)

            def pass2(k, acc, h=h, slot=slot, src=src, eg=eg, u=u):
                sn = eg * src[h, k] + kq_scr[slot, 0, pl.ds(k, 1), :] * u
                sout_ref[h, k] = sn
                return acc + sn * kq_scr[slot, 1, pl.ds(k, 1), :]

            o = lax.fori_loop(0, HEAD_DIM, pass2, zero, unroll=8)
            o = o * lax.rsqrt(jnp.mean(o * o, axis=0, keepdims=True) + G_NORM_EPS) * ng_ref[...]
            y_heads.append(o * _silu(zt[t][hs]))
        y_ref[t] = _to_batch_major(jnp.concatenate(y_heads, axis=0), G_WIDTH)
    for t in range(tv, t_pad):
        y_ref[t] = jnp.zeros((nb, G_WIDTH), F32)
    for j in range(G_CONV - 1):
        cbout_ref[j] = _to_batch_major(full[tv + j], G_QKV)


def _gdn_sample(x_tm, cb_tm, s_nat, l, s_acc, cb_acc, params, *, t_pad):
    tv, nb, _ = x_tm.shape
    alog, dtb, cwt, ng = params
    kern = functools.partial(_gdn_sample_kernel, tv=tv, t_pad=t_pad)
    kern, acc_specs, acc_args, aliases = _stacked_outputs(kern, 7, (s_acc, cb_acc))
    state_spec = pl.BlockSpec((None, N_HEADS, HEAD_DIM, HEAD_DIM, nb), lambda i: (l, 0, 0, 0, 0))
    cb_spec = pl.BlockSpec((None, G_CONV - 1, nb, G_QKV), lambda i: (l, 0, 0, 0))
    smem = pl.BlockSpec(memory_space=pltpu.SMEM)
    return pl.pallas_call(
        kern,
        grid=(1,),
        in_specs=[smem, smem, _full_spec(x_tm.shape), cb_spec, state_spec,
                  _full_spec(cwt.shape), _full_spec(ng.shape)] + acc_specs,
        out_specs=[_full_spec((t_pad, nb, G_WIDTH)), state_spec, cb_spec],
        out_shape=[jax.ShapeDtypeStruct((t_pad, nb, G_WIDTH), F32),
                   jax.ShapeDtypeStruct(s_nat.shape, F32), jax.ShapeDtypeStruct(cb_tm.shape, F32)],
        input_output_aliases=aliases,
        scratch_shapes=[pltpu.VMEM((tv * N_HEADS, 2, HEAD_DIM, nb), F32)],
        compiler_params=pltpu.CompilerParams(dimension_semantics=("arbitrary",),
                                             vmem_limit_bytes=VMEM_LIMIT_BYTES),
        name="gdn_sample",
    )(alog, dtb, x_tm, cb_tm, s_nat, cwt, ng, *acc_args)


def _rwkv_sample_kernel(x_ref, shift_ref, s_ref, mu_ref, w0_ref, wupt_ref, a0_ref, aupt_ref, gupt_ref,
                        kk_ref, ka_ref, rk_ref, gng_ref, gnb_ref, s_acc_ref, sh_acc_ref,
                        y_ref, sout_ref, shout_ref, vec_scr, y_scr, *, tv, t_pad):
    del s_acc_ref, sh_acc_ref
    nb = x_ref.shape[1]
    xs = [x_ref[t] for t in range(tv)]
    shout_ref[...] = xs[tv - 1]
    prev = _to_feature_major(shift_ref[...], R_PROJ)
    per_step = []
    for t in range(tv):
        f = _to_feature_major(xs[t], R_PROJ)
        fs = f + (prev - f) * mu_ref[...]
        prev = f
        r = fs[0:R_WIDTH]
        k = fs[R_WIDTH:2 * R_WIDTH]
        v = fs[2 * R_WIDTH:3 * R_WIDTH]
        w = w0_ref[...] + _mm3_each([wupt_ref[...]], [jnp.tanh(fs[768:800])])[0]
        lr = _sigmoid(a0_ref[...] + _mm3_each([aupt_ref[...]], [fs[800:832]])[0])
        gate = _mm3_each([gupt_ref[...]], [_sigmoid(fs[832:896])])[0]
        decay = jnp.exp(-jnp.exp(-_softplus(-w) - 0.5))
        kkx = k * kk_ref[...]
        k2 = k * (1.0 + (lr - 1.0) * ka_ref[...])
        bonus = []
        for h in range(N_HEADS):
            hs = slice(h * HEAD_DIM, (h + 1) * HEAD_DIM)
            kx = kkx[hs]
            kkh = kx * lax.rsqrt(jnp.sum(kx * kx, axis=0, keepdims=True) + L2_EPS)
            vec_scr[t, h, 0] = -kkh
            vec_scr[t, h, 1] = decay[hs]
            vec_scr[t, h, 2] = kkh * lr[hs]
            vec_scr[t, h, 3] = k2[hs]
            vec_scr[t, h, 4] = r[hs]
            vec_scr[t, h, 5] = v[hs]
            bonus.append(jnp.sum(r[hs] * k2[hs] * rk_ref[hs, :], axis=0, keepdims=True) * v[hs])
        per_step.append((jnp.concatenate(bonus, axis=0), gate))

    for h in range(N_HEADS):
        def row(vi, carry, h=h):
            s = s_ref[h, vi]
            for t in range(tv):
                sa = jnp.sum(s * vec_scr[t, h, 0], axis=0, keepdims=True)
                s = (s * vec_scr[t, h, 1] + sa * vec_scr[t, h, 2]
                     + vec_scr[t, h, 5, pl.ds(vi, 1), :] * vec_scr[t, h, 3])
                y_scr[t, h, pl.ds(vi, 1), :] = jnp.sum(s * vec_scr[t, h, 4], axis=0, keepdims=True)
            sout_ref[h, vi] = s
            return carry

        lax.fori_loop(0, HEAD_DIM, row, 0, unroll=2)

    for t in range(tv):
        bonus, gate = per_step[t]
        outs = []
        for h in range(N_HEADS):
            hs = slice(h * HEAD_DIM, (h + 1) * HEAD_DIM)
            o = y_scr[t, h]
            oc = o - jnp.mean(o, axis=0, keepdims=True)
            var = jnp.mean(oc * oc, axis=0, keepdims=True)
            outs.append(oc * lax.rsqrt(var + R_GN_EPS) * gng_ref[hs, :] + gnb_ref[hs, :])
        y = (jnp.concatenate(outs, axis=0) + bonus) * gate
        y_ref[t] = _to_batch_major(y, R_WIDTH)
    for t in range(tv, t_pad):
        y_ref[t] = jnp.zeros((nb, R_WIDTH), F32)


def _rwkv_sample(x_tm, shift_all, s_nat, l, s_acc, sh_acc, params, *, t_pad):
    tv, nb, _ = x_tm.shape
    kern = functools.partial(_rwkv_sample_kernel, tv=tv, t_pad=t_pad)
    kern, acc_specs, acc_args, aliases = _stacked_outputs(kern, 3 + len(params), (s_acc, sh_acc))
    state_spec = pl.BlockSpec((None, N_HEADS, HEAD_DIM, HEAD_DIM, nb), lambda i: (l, 0, 0, 0, 0))
    shift_spec = pl.BlockSpec((None, nb, R_PROJ), lambda i: (l, 0, 0))
    return pl.pallas_call(
        kern,
        grid=(1,),
        in_specs=[_full_spec(x_tm.shape), shift_spec, state_spec] + [_full_spec(p.shape) for p in params] + acc_specs,
        out_specs=[_full_spec((t_pad, nb, R_WIDTH)), state_spec, shift_spec],
        out_shape=[jax.ShapeDtypeStruct((t_pad, nb, R_WIDTH), F32),
                   jax.ShapeDtypeStruct(s_nat.shape, F32), jax.ShapeDtypeStruct(shift_all.shape, F32)],
        input_output_aliases=aliases,
        scratch_shapes=[pltpu.VMEM((tv, N_HEADS, 6, HEAD_DIM, nb), F32),
                        pltpu.VMEM((tv, N_HEADS, HEAD_DIM, nb), F32)],
        compiler_params=pltpu.CompilerParams(dimension_semantics=("arbitrary",),
                                             vmem_limit_bytes=VMEM_LIMIT_BYTES),
        name="rwkv7_sample",
    )(x_tm, shift_all, s_nat, *params, *acc_args)


def _rel_buckets(dist):
    n = np.maximum(dist, 0)
    nf = np.maximum(n, 1).astype(np.float32)
    large = REL_MAX_EXACT + (np.log(nf / REL_MAX_EXACT) / math.log(REL_MAX_DIST / REL_MAX_EXACT)
                             * (REL_BUCKETS - REL_MAX_EXACT)).astype(np.int32)
    return np.where(n < REL_MAX_EXACT, n, np.minimum(large, REL_BUCKETS - 1)).astype(np.int32)


def _bias_kernel(table_ref, bucket_ref, o_ref):
    bucket = bucket_ref[...]
    for h in range(S_Q_HEADS):
        acc = jnp.full(bucket.shape, NEG_INF, F32)
        for k in range(REL_BUCKETS):
            acc = jnp.where(bucket == k, table_ref[k, h], acc)
        o_ref[h] = acc


def _rel_bias(table, dist, visible=None):
    bucket = _rel_buckets(dist)
    if visible is not None:
        bucket = np.where(visible, bucket, -1).astype(np.int32)
    bucket = jnp.asarray(bucket)
    return pl.pallas_call(
        _bias_kernel,
        in_specs=[pl.BlockSpec(memory_space=pltpu.SMEM), _full_spec(bucket.shape)],
        out_specs=_full_spec((S_Q_HEADS,) + bucket.shape),
        out_shape=jax.ShapeDtypeStruct((S_Q_HEADS,) + bucket.shape, F32),
        grid=(1,),
        name="rel_bias",
    )(table, bucket)


Q_BLOCKS_PER_STEP = 2


def _swa_prompt_kernel(sink_ref, q_ref, kp_ref, kc_ref, vp_ref, vc_ref, bias_ref, o_ref):
    n = pl.program_id(1)
    q = q_ref[...]
    kall = jnp.concatenate([kp_ref[...], kc_ref[...]], axis=0).astype(BF16)
    vall = jnp.concatenate([vp_ref[...], vc_ref[...]], axis=0).astype(BF16)
    chains = [(s, h) for s in range(Q_BLOCKS_PER_STEP) for h in range(S_Q_HEADS)]
    band = lambda a, s, h: a[s * ATTN_BLOCK:(s + 2) * ATTN_BLOCK,
                             (h // S_GROUP) * HEAD_DIM:(h // S_GROUP + 1) * HEAD_DIM]
    from_prev = (lax.broadcasted_iota(jnp.int32, (ATTN_BLOCK, ATTN_BLOCK), 1)
                 > lax.broadcasted_iota(jnp.int32, (ATTN_BLOCK, ATTN_BLOCK), 0))
    has_prev = [jnp.where(n == 0, 0, 1) if s == 0 else 1 for s in range(Q_BLOCKS_PER_STEP)]
    qk = [_dg(q[s * ATTN_BLOCK:(s + 1) * ATTN_BLOCK, h * HEAD_DIM:(h + 1) * HEAD_DIM].astype(BF16),
              band(kall, s, h), NT) for s, h in chains]
    logits = [jnp.where(from_prev, x[:, :ATTN_BLOCK], x[:, ATTN_BLOCK:]) * (HEAD_DIM ** -0.5)
              + bias_ref[has_prev[s], h] for x, (s, h) in zip(qk, chains)]
    m = [jnp.maximum(jnp.max(x, axis=-1, keepdims=True), sink_ref[h]) for x, (s, h) in zip(logits, chains)]
    p = [jnp.exp(x - mm) for x, mm in zip(logits, m)]
    probs = [x / (jnp.sum(x, axis=-1, keepdims=True) + jnp.exp(sink_ref[h] - mm))
             for x, mm, (s, h) in zip(p, m, chains)]
    for x, (s, h) in zip(probs, chains):
        pband = jnp.concatenate([jnp.where(from_prev, x, 0.0), jnp.where(from_prev, 0.0, x)], axis=1)
        o_ref[s * ATTN_BLOCK:(s + 1) * ATTN_BLOCK, h * HEAD_DIM:(h + 1) * HEAD_DIM] = _dg(
            pband.astype(BF16), band(vall, s, h), NN)


def _swa_prompt(ps, sinks, bias):
    b, t, _ = ps.shape
    rows = Q_BLOCKS_PER_STEP * ATTN_BLOCK
    kcol = S_WIDTH // S_KV_WIDTH
    prev = lambda j: jnp.maximum(Q_BLOCKS_PER_STEP * j - 1, 0)
    return pl.pallas_call(
        _swa_prompt_kernel,
        grid=(b, t // rows),
        in_specs=[pl.BlockSpec(memory_space=pltpu.SMEM),
                  pl.BlockSpec((None, rows, S_WIDTH), lambda i, j: (i, j, 0)),
                  pl.BlockSpec((None, ATTN_BLOCK, S_KV_WIDTH), lambda i, j: (i, prev(j), kcol)),
                  pl.BlockSpec((None, rows, S_KV_WIDTH), lambda i, j: (i, j, kcol)),
                  pl.BlockSpec((None, ATTN_BLOCK, S_KV_WIDTH), lambda i, j: (i, prev(j), kcol + 1)),
                  pl.BlockSpec((None, rows, S_KV_WIDTH), lambda i, j: (i, j, kcol + 1)),
                  _full_spec(bias.shape)],
        out_specs=pl.BlockSpec((None, rows, S_WIDTH), lambda i, j: (i, j, 0)),
        out_shape=jax.ShapeDtypeStruct((b, t, S_WIDTH), F32),
        compiler_params=pltpu.CompilerParams(dimension_semantics=("arbitrary", "arbitrary"),
                                             vmem_limit_bytes=VMEM_LIMIT_BYTES),
        name="swa_prompt",
    )(sinks, ps, ps, ps, ps, ps, bias)


def _swa_sample_kernel(sink_ref, q_ref, kn_ref, vn_ref, ck_ref, cv_ref, bias_c_ref, bias_n_ref,
                       k_acc_ref, v_acc_ref, o_ref, cko_ref, cvo_ref, *, bb, tp, tv):
    del k_acc_ref, v_acc_ref
    wc = WINDOW
    step = lambda n: jnp.concatenate([lax.broadcasted_iota(jnp.int32, (tp, n), 0)] * S_GROUP, axis=0)
    ti_c = step(wc)
    kj_c = lax.broadcasted_iota(jnp.int32, (S_GROUP * tp, wc), 1)
    dist_c = wc + ti_c - kj_c
    mask_c = (dist_c >= 0) & (dist_c < WINDOW)
    ti_n = step(tp)
    kj_n = lax.broadcasted_iota(jnp.int32, (S_GROUP * tp, tp), 1)
    dist_n = ti_n - kj_n
    mask_n = (dist_n >= 0) & (dist_n < WINDOW) & (kj_n < tv)

    sinks = [jnp.concatenate([jnp.full((tp, 1), sink_ref[kv * S_GROUP + g], F32) for g in range(S_GROUP)], axis=0)
             for kv in range(S_KV_HEADS)]
    old_rows = lax.broadcasted_iota(jnp.int32, (wc, 1), 0) < wc - tv
    scale = HEAD_DIM ** -0.5

    def one_group(seqs):
        chains = [(n, kv) for n in range(len(seqs)) for kv in range(S_KV_HEADS)]
        kvs = lambda a, kv: a[:, kv * HEAD_DIM:(kv + 1) * HEAD_DIM]
        q_l = [q_ref[i] for i in seqs]
        kn_l = [kn_ref[i] for i in seqs]
        vn_l = [vn_ref[i] for i in seqs]
        ck_l = [ck_ref[i] for i in seqs]
        cv_l = [cv_ref[i] for i in seqs]
        qg = [jnp.concatenate([q_l[n][:, (kv * S_GROUP + g) * HEAD_DIM:(kv * S_GROUP + g + 1) * HEAD_DIM]
                               for g in range(S_GROUP)], axis=0).astype(BF16) for n, kv in chains]
        lc = [_dg(x, kvs(ck_l[n], kv).astype(BF16), NT) for x, (n, kv) in zip(qg, chains)]
        ln = [_dg(x, kvs(kn_l[n], kv).astype(BF16), NT) for x, (n, kv) in zip(qg, chains)]
        lc = [jnp.where(mask_c, x * scale + bias_c_ref[kv], NEG_INF) for x, (n, kv) in zip(lc, chains)]
        ln = [jnp.where(mask_n, x * scale + bias_n_ref[kv], NEG_INF) for x, (n, kv) in zip(ln, chains)]
        m = [jnp.maximum(jnp.maximum(jnp.max(a, axis=-1, keepdims=True), jnp.max(b, axis=-1, keepdims=True)),
                         sinks[kv]) for a, b, (n, kv) in zip(lc, ln, chains)]
        pc = [jnp.exp(a - mm) for a, mm in zip(lc, m)]
        pn = [jnp.exp(a - mm) for a, mm in zip(ln, m)]
        den = [jnp.sum(a, axis=-1, keepdims=True) + jnp.sum(b, axis=-1, keepdims=True) + jnp.exp(sinks[kv] - mm)
               for a, b, mm, (n, kv) in zip(pc, pn, m, chains)]
        oc = [_dg((a / d).astype(BF16), kvs(cv_l[n], kv).astype(BF16), NN) for a, d, (n, kv) in zip(pc, den, chains)]
        on = [_dg((a / d).astype(BF16), kvs(vn_l[n], kv).astype(BF16), NN) for a, d, (n, kv) in zip(pn, den, chains)]
        for a, b, (n, kv) in zip(oc, on, chains):
            og = a + b
            for g in range(S_GROUP):
                hq = kv * S_GROUP + g
                o_ref[seqs[n], :, hq * HEAD_DIM:(hq + 1) * HEAD_DIM] = og[g * tp:(g + 1) * tp, :]
        zpad = jnp.zeros((wc - tp, S_KV_WIDTH), F32)
        for n, i in enumerate(seqs):
            cko_ref[i] = jnp.where(old_rows, pltpu.roll(ck_l[n], wc - tv, 0),
                                   pltpu.roll(jnp.concatenate([kn_l[n], zpad], axis=0), wc - tv, 0))
            cvo_ref[i] = jnp.where(old_rows, pltpu.roll(cv_l[n], wc - tv, 0),
                                   pltpu.roll(jnp.concatenate([vn_l[n], zpad], axis=0), wc - tv, 0))

    _for_each_group(bb, one_group)


def _swa_sample(ps, ck_all, cv_all, l, k_acc, v_acc, sinks, bias_c, bias_n, *, bb, tv):
    b, tp, _ = ps.shape
    bb = min(bb, b)
    assert b % bb == 0
    kcol = S_WIDTH // S_KV_WIDTH
    kern = functools.partial(_swa_sample_kernel, bb=bb, tp=tp, tv=tv)
    kern, acc_specs, acc_args, aliases = _stacked_outputs(kern, 8, (k_acc, v_acc))
    cache_spec = _layer_block((bb, WINDOW, S_KV_WIDTH), l)
    return pl.pallas_call(
        kern,
        grid=(b // bb,),
        in_specs=[pl.BlockSpec(memory_space=pltpu.SMEM),
                  pl.BlockSpec((bb, tp, S_WIDTH), lambda i: (i, 0, 0)),
                  pl.BlockSpec((bb, tp, S_KV_WIDTH), lambda i: (i, 0, kcol)),
                  pl.BlockSpec((bb, tp, S_KV_WIDTH), lambda i: (i, 0, kcol + 1)),
                  cache_spec, cache_spec, _full_spec(bias_c.shape), _full_spec(bias_n.shape)] + acc_specs,
        out_specs=[pl.BlockSpec((bb, tp, S_WIDTH), lambda i: (i, 0, 0)), cache_spec, cache_spec],
        out_shape=[jax.ShapeDtypeStruct((b, tp, S_WIDTH), F32),
                   jax.ShapeDtypeStruct(ck_all.shape, F32), jax.ShapeDtypeStruct(cv_all.shape, F32)],
        input_output_aliases=aliases,
        compiler_params=pltpu.CompilerParams(dimension_semantics=("arbitrary",),
                                             vmem_limit_bytes=VMEM_LIMIT_BYTES),
        name="swa_sample",
    )(sinks, ps, ps, ps, ck_all, cv_all, bias_c, bias_n, *acc_args)


def _prep_layer(l, ffn1_w_in, ffn1_w_out, ln1_g, ln1_b, w_in, rwkv_mu, rwkv_w0, rwkv_w_up, rwkv_a0,
                rwkv_a_up, rwkv_g_up, rwkv_k_k, rwkv_k_a, rwkv_r_k, rwkv_gn_g, rwkv_gn_b, swa_sinks,
                gdn_conv_w, gdn_a_log, gdn_dt_bias, gdn_norm_g, w_out, ln2_g, ln2_b,
                ffn2_w_in, ffn2_w_out, ln3_g, ln3_b):
    row = lambda a: a[l].reshape(1, -1)
    col = lambda a: a[l].reshape(-1, 1)
    d = w_in.shape[1]
    n_beta = 4 * G_WIDTH
    win = jnp.concatenate([w_in[l], jnp.zeros((d, PROJ_PAD - w_in.shape[2]), F32)], axis=1).astype(BF16)
    lane_tile = lambda a: jnp.zeros((1, LANES), F32).at[0, N_HEADS:2 * N_HEADS].set(a[l])
    del n_beta
    return dict(
        ffn1=(ffn1_w_in[l].astype(BF16), ffn1_w_out[l].astype(BF16), row(ln1_g), row(ln1_b), win),
        rwkv=(row(rwkv_mu), row(rwkv_w0), rwkv_w_up[l], row(rwkv_a0), rwkv_a_up[l], rwkv_g_up[l],
              row(rwkv_k_k), row(rwkv_k_a), row(rwkv_r_k), row(rwkv_gn_g), row(rwkv_gn_b)),
        rwkv_sample=(col(rwkv_mu), col(rwkv_w0), rwkv_w_up[l].T, col(rwkv_a0), rwkv_a_up[l].T, rwkv_g_up[l].T,
                     col(rwkv_k_k), col(rwkv_k_a), col(rwkv_r_k), col(rwkv_gn_g), col(rwkv_gn_b)),
        sinks=swa_sinks[l],
        gdn=(gdn_conv_w[l], lane_tile(gdn_a_log), lane_tile(gdn_dt_bias), row(gdn_norm_g)),
        gdn_sample=(gdn_a_log[l], gdn_dt_bias[l], gdn_conv_w[l].T, gdn_norm_g[l].reshape(HEAD_DIM, 1)),
        out=(w_out[l].astype(BF16), row(ln2_g), row(ln2_b), ffn2_w_in[l].astype(BF16),
             ffn2_w_out[l].astype(BF16), row(ln3_g), row(ln3_b)),
    )


def _run_trunk(x, rwkv_s, rwkv_shift, swa_k, swa_v, gdn_s, gdn_conv, layers, biases, *, prompt, tv, alpha,
               tm, n_chunks, bb, bb_rwkv, c, nsub):
    b, t, d = x.shape
    depth = len(layers)
    xf = x.reshape(b * t, d)
    s_r = sh_r = s_g = cb = None
    kc, vc = ([], []) if prompt else (None, None)
    if prompt:
        rwkv_shift = rwkv_shift.reshape(depth, b, 1, R_PROJ)
    else:
        rwkv_s = rwkv_s.transpose(0, 2, 3, 4, 1)
        gdn_s = gdn_s.transpose(0, 2, 3, 4, 1)
        gdn_conv = gdn_conv.transpose(0, 2, 1, 3)
    for l, lp in enumerate(layers):
        x1, pr, ps, pg = _ffn_proj(xf, *lp["ffn1"], alpha=alpha, tm=tm, n_chunks=n_chunks)
        pr = pr.reshape(b, t, R_PROJ)
        ps = ps.reshape(b, t, S_PROJ)
        pg = pg.reshape(b, t, G_PROJ_PAD)
        if prompt:
            y_r, s_r, sh_r = _rwkv_mixer(pr, rwkv_shift, rwkv_s, l, s_r, sh_r, lp["rwkv"], bb=bb_rwkv, c=c, tv=tv,
                                         nsub=nsub)
        else:
            y_tm, s_r, sh_r = _rwkv_sample(pr[:, :tv].transpose(1, 0, 2), rwkv_shift, rwkv_s, l, s_r, sh_r,
                                           lp["rwkv_sample"], t_pad=t)
            y_r = y_tm.transpose(1, 0, 2)
        if prompt:
            y_s = _swa_prompt(ps, lp["sinks"], biases[0])
            kc.append(ps[:, t - WINDOW:, S_WIDTH:S_WIDTH + S_KV_WIDTH])
            vc.append(ps[:, t - WINDOW:, S_WIDTH + S_KV_WIDTH:])
        else:
            y_s, kc, vc = _swa_sample(ps, swa_k, swa_v, l, kc, vc, lp["sinks"], biases[1], biases[2], bb=bb, tv=tv)
        if prompt:
            y_g, s_g, cb = _gdn_mixer(pg, gdn_conv, gdn_s, l, s_g, cb, lp["gdn"], bb=bb, c=c, tv=tv, nsub=nsub)
        else:
            y_tm, s_g, cb = _gdn_sample(pg[:, :tv].transpose(1, 0, 2), gdn_conv, gdn_s, l, s_g, cb,
                                        lp["gdn_sample"], t_pad=t)
            y_g = y_tm.transpose(1, 0, 2)
        xf = _out_ffn(x1, y_r.reshape(b * t, R_WIDTH), y_s.reshape(b * t, S_WIDTH), y_g.reshape(b * t, G_WIDTH),
                      *lp["out"], alpha=alpha, tm=tm, n_chunks=n_chunks)
    if prompt:
        kc, vc = jnp.stack(kc, axis=0), jnp.stack(vc, axis=0)
    else:
        s_r = s_r.transpose(0, 4, 1, 2, 3)
        s_g = s_g.transpose(0, 4, 1, 2, 3)
        cb = cb.transpose(0, 2, 1, 3)
    cache_shape = (depth, b, WINDOW, S_KV_HEADS, HEAD_DIM)
    return xf.reshape(b, t, d), [s_r, sh_r.reshape(depth, b, R_PROJ), kc.reshape(cache_shape),
                                 vc.reshape(cache_shape), s_g, cb]


def kernel(x_prompt, x_sample, state_rwkv, state_rwkv_shift, cache_swa_k, cache_swa_v, state_gdn, state_gdn_conv, ffn1_w_in, ffn1_w_out, ln1_g, ln1_b, w_in, rwkv_mu, rwkv_w0, rwkv_w_up, rwkv_a0, rwkv_a_up, rwkv_g_up, rwkv_k_k, rwkv_k_a, rwkv_r_k, rwkv_gn_g, rwkv_gn_b, swa_sinks, rel_table, gdn_conv_w, gdn_a_log, gdn_dt_bias, gdn_norm_g, w_out, ln2_g, ln2_b, ffn2_w_in, ffn2_w_out, ln3_g, ln3_b):
    depth = ffn1_w_in.shape[0]
    alpha = (2 * depth) ** 0.25
    layers = [_prep_layer(l, ffn1_w_in, ffn1_w_out, ln1_g, ln1_b, w_in, rwkv_mu, rwkv_w0, rwkv_w_up, rwkv_a0,
                          rwkv_a_up, rwkv_g_up, rwkv_k_k, rwkv_k_a, rwkv_r_k, rwkv_gn_g, rwkv_gn_b, swa_sinks,
                          gdn_conv_w, gdn_a_log, gdn_dt_bias, gdn_norm_g, w_out, ln2_g, ln2_b,
                          ffn2_w_in, ffn2_w_out, ln3_g, ln3_b) for l in range(depth)]
    bp, tp_len, d = x_prompt.shape
    bs, ts, _ = x_sample.shape
    ts_pad = -(-ts // SUBLANES) * SUBLANES

    assert WINDOW == ATTN_BLOCK
    qi = np.arange(ATTN_BLOCK)[:, None]
    kc = np.arange(ATTN_BLOCK)[None, :]
    dist_p = np.where(kc > qi, ATTN_BLOCK + qi - kc, qi - kc)
    bias_p = jnp.stack([_rel_bias(rel_table, dist_p, kc <= qi), _rel_bias(rel_table, dist_p)], axis=0)
    ti = (np.arange(S_GROUP * ts_pad) % ts_pad)[:, None]
    wc = cache_swa_k.shape[2]
    bias_c = _rel_bias(rel_table, wc + ti - np.arange(wc)[None, :])
    bias_n = _rel_bias(rel_table, ti - np.arange(ts_pad)[None, :])
    regroup = lambda a: jnp.stack([jnp.concatenate([a[kv * S_GROUP + g, g * ts_pad:(g + 1) * ts_pad]
                                                    for g in range(S_GROUP)], axis=0)
                                   for kv in range(S_KV_HEADS)], axis=0)
    biases = (bias_p, regroup(bias_c), regroup(bias_n))

    zeros = lambda *s: jnp.zeros((depth, bp) + s, F32)
    y_prompt, p_states = _run_trunk(
        x_prompt, zeros(N_HEADS, HEAD_DIM, HEAD_DIM), zeros(R_PROJ), None, None,
        zeros(N_HEADS, HEAD_DIM, HEAD_DIM), zeros(G_CONV - 1, G_QKV), layers, biases,
        prompt=True, tv=64, alpha=alpha, tm=512, n_chunks=2, bb=bp, bb_rwkv=bp, c=64, nsub=2)

    xs = jnp.concatenate([x_sample, jnp.zeros((bs, ts_pad - ts, d), F32)], axis=1)
    ck = cache_swa_k.reshape(depth, bs, wc, S_KV_WIDTH)
    cv = cache_swa_v.reshape(depth, bs, wc, S_KV_WIDTH)
    y_sample, s_states = _run_trunk(
        xs, state_rwkv, state_rwkv_shift, ck, cv, state_gdn, state_gdn_conv, layers, biases,
        prompt=False, tv=ts, alpha=alpha, tm=512, n_chunks=2, bb=8, bb_rwkv=16, c=ts_pad, nsub=1)
    return (y_prompt, y_sample[:, :ts]) + tuple(p_states) + tuple(s_states)
```

```python
import functools
import math

import numpy as np
import jax
import jax.numpy as jnp
from jax import lax
from jax.experimental import pallas as pl
from jax.experimental.pallas import tpu as pltpu

F32 = jnp.float32
BF16 = jnp.bfloat16

HEAD_DIM = 64
N_HEADS = 4
R_WIDTH = N_HEADS * HEAD_DIM
R_PROJ = 896
S_Q_HEADS = 8
S_KV_HEADS = 2
S_GROUP = S_Q_HEADS // S_KV_HEADS
S_WIDTH = S_Q_HEADS * HEAD_DIM
S_KV_WIDTH = S_KV_HEADS * HEAD_DIM
S_PROJ = S_WIDTH + 2 * S_KV_WIDTH
G_WIDTH = N_HEADS * HEAD_DIM
G_CONV = 4
G_QKV = 3 * G_WIDTH
G_PROJ_PAD = 4 * G_WIDTH + 128
WINDOW = 128
ATTN_BLOCK = 128
REL_BUCKETS = 32
REL_MAX_EXACT = 16
REL_MAX_DIST = 128
NEG_INF = -1e30
R_GN_EPS = 64e-5
G_NORM_EPS = 1e-6
LN_EPS = 1e-5
L2_EPS = 1e-6
PROJ_PAD = R_PROJ + S_PROJ + G_PROJ_PAD

LANES = 128
SUBLANES = 8
VMEM_LIMIT_BYTES = 56 * 1024 * 1024

NN = ((1,), (0,))
NT = ((1,), (1,))
TN = ((0,), (0,))


def _dg(a, b, dims):
    return lax.dot_general(a, b, (dims, ((), ())), preferred_element_type=F32)


def _split2(a):
    hi = a.astype(BF16)
    lo = (a - hi.astype(F32)).astype(BF16)
    return hi, lo


def _mm_exact_lhs(a_bf16, b, dims=NN):
    b1 = b.astype(BF16)
    r1 = b - b1.astype(F32)
    b2 = r1.astype(BF16)
    b3 = (r1 - b2.astype(F32)).astype(BF16)
    return _dg(a_bf16, b1, dims) + (_dg(a_bf16, b2, dims) + _dg(a_bf16, b3, dims))


def _sigmoid(x):
    return 1.0 / (1.0 + jnp.exp(-x))


def _silu(x):
    return x * _sigmoid(x)


def _softplus(x):
    return jnp.maximum(x, 0.0) + jnp.log(1.0 + jnp.exp(-jnp.abs(x)))


def _layer_norm(z, g, b):
    mu = jnp.mean(z, axis=-1, keepdims=True)
    zc = z - mu
    var = jnp.mean(zc * zc, axis=-1, keepdims=True)
    return zc * lax.rsqrt(var + LN_EPS) * g + b


def _mm3_each(a_list, b_list, dims=NN):
    sa = [_split2(a) for a in a_list]
    sb = [_split2(b) for b in b_list]
    return [_dg(ah, bh, dims) + (_dg(ah, bl, dims) + _dg(al, bh, dims))
            for (ah, al), (bh, bl) in zip(sa, sb)]


def _mm1_each(a_list, b_list, dims=NN):
    return [_dg(a.astype(BF16), b.astype(BF16), dims) for a, b in zip(a_list, b_list)]


def _bf16_each(a_list):
    return [a.astype(BF16) for a in a_list]


def _tri_inverse_each(m_list, c):
    row = lax.broadcasted_iota(jnp.int32, (c, c), 0)
    col = lax.broadcasted_iota(jnp.int32, (c, c), 1)
    eye = jnp.where(row == col, 1.0, 0.0).astype(F32)
    t = [eye + m for m in m_list]
    mp = _bf16_each(m_list)
    span = 2
    while span < c:
        mp = _bf16_each(_mm1_each(mp, mp))
        t = [a + b for a, b in zip(t, _mm1_each(t, mp))]
        span *= 2
    return t


def _for_each_group(bb, one_group):
    one_group(list(range(bb)))


def _full_spec(shape):
    nd = len(shape)
    return pl.BlockSpec(shape, lambda *_: (0,) * nd)


def _resident_spec(shape, index_map):
    return pl.BlockSpec(shape, index_map, pipeline_mode=pl.Buffered(1))


SUB_ROWS = 256


def _row_tiles(tm):
    n = max(tm // SUB_ROWS, 1)
    return [slice(i * (tm // n), (i + 1) * (tm // n)) for i in range(n)]


def _swiglu_each(xb_l, wi_ref, wo_ref, d_ff, n_chunks):
    cw = d_ff // n_chunks
    acc = [None] * len(xb_l)
    for j in range(n_chunks):
        gate = [_dg(xb, wi_ref[:, j * cw:(j + 1) * cw], NN) for xb in xb_l]
        up = [_dg(xb, wi_ref[:, d_ff + j * cw:d_ff + (j + 1) * cw], NN) for xb in xb_l]
        act = [(_silu(g) * u).astype(BF16) for g, u in zip(gate, up)]
        part = [_dg(a, wo_ref[j * cw:(j + 1) * cw, :], NN) for a in act]
        acc = [p if a is None else a + p for a, p in zip(acc, part)]
    return acc


def _ffn_proj_kernel(x_ref, wi_ref, wo_ref, g_ref, b_ref, win_ref,
                     x1_ref, pr_ref, ps_ref, pg_ref, *, alpha, d_ff, n_chunks, tm):
    tiles = _row_tiles(tm)
    x_l = [x_ref[r, :] for r in tiles]
    y_l = _swiglu_each([x.astype(BF16) for x in x_l], wi_ref, wo_ref, d_ff, n_chunks)
    x1_l = [_layer_norm(alpha * x + 0.5 * y, g_ref[...], b_ref[...]) for x, y in zip(x_l, y_l)]
    p_l = [_dg(x1.astype(BF16), win_ref[...], NN) for x1 in x1_l]
    for r, x1, p in zip(tiles, x1_l, p_l):
        x1_ref[r, :] = x1
        pr_ref[r, :] = p[:, :R_PROJ]
        ps_ref[r, :] = p[:, R_PROJ:R_PROJ + S_PROJ]
        pg_ref[r, :] = p[:, R_PROJ + S_PROJ:]


def _ffn_proj(x, wi, wo, g, b, win, *, alpha, tm, n_chunks):
    m, d = x.shape
    tm = min(tm, m)
    assert m % tm == 0
    d_ff = wo.shape[0]
    kern = functools.partial(_ffn_proj_kernel, alpha=alpha, d_ff=d_ff, n_chunks=n_chunks, tm=tm)
    row = lambda w: pl.BlockSpec((tm, w), lambda i: (i, 0))
    const = lambda i: (0, 0)
    return pl.pallas_call(
        kern,
        grid=(m // tm,),
        in_specs=[row(d),
                  _resident_spec(wi.shape, const), _resident_spec(wo.shape, const),
                  _resident_spec(g.shape, const), _resident_spec(b.shape, const),
                  _resident_spec(win.shape, const)],
        out_specs=[row(d), row(R_PROJ), row(S_PROJ), row(G_PROJ_PAD)],
        out_shape=[jax.ShapeDtypeStruct((m, d), F32), jax.ShapeDtypeStruct((m, R_PROJ), F32),
                   jax.ShapeDtypeStruct((m, S_PROJ), F32), jax.ShapeDtypeStruct((m, G_PROJ_PAD), F32)],
        compiler_params=pltpu.CompilerParams(dimension_semantics=("arbitrary",),
                                             vmem_limit_bytes=VMEM_LIMIT_BYTES),
        name="ffn_proj",
    )(x, wi, wo, g, b, win)


def _out_ffn_kernel(x_ref, yr_ref, ys_ref, yg_ref, wout_ref, g2_ref, b2_ref, wi_ref, wo_ref, g3_ref, b3_ref,
                    o_ref, *, alpha, d_ff, n_chunks, tm):
    tiles = _row_tiles(tm)
    x_l = [x_ref[r, :] for r in tiles]
    mix_l = [(_dg(yr_ref[r, :].astype(BF16), wout_ref[0:R_WIDTH, :], NN)
              + _dg(ys_ref[r, :].astype(BF16), wout_ref[R_WIDTH:R_WIDTH + S_WIDTH, :], NN)
              + _dg(yg_ref[r, :].astype(BF16), wout_ref[R_WIDTH + S_WIDTH:, :], NN)) for r in tiles]
    x2_l = [_layer_norm(alpha * x + mix, g2_ref[...], b2_ref[...]) for x, mix in zip(x_l, mix_l)]
    y_l = _swiglu_each([x2.astype(BF16) for x2 in x2_l], wi_ref, wo_ref, d_ff, n_chunks)
    for r, x2, y in zip(tiles, x2_l, y_l):
        o_ref[r, :] = _layer_norm(alpha * x2 + 0.5 * y, g3_ref[...], b3_ref[...])


def _out_ffn(x, yr, ys, yg, wout, g2, b2, wi, wo, g3, b3, *, alpha, tm, n_chunks):
    m, d = x.shape
    tm = min(tm, m)
    assert m % tm == 0
    d_ff = wo.shape[0]
    kern = functools.partial(_out_ffn_kernel, alpha=alpha, d_ff=d_ff, n_chunks=n_chunks, tm=tm)
    row = lambda w: pl.BlockSpec((tm, w), lambda i: (i, 0))
    const = lambda i: (0, 0)
    res = lambda a: _resident_spec(a.shape, const)
    return pl.pallas_call(
        kern,
        grid=(m // tm,),
        in_specs=[row(d), row(R_WIDTH), row(S_WIDTH), row(G_WIDTH),
                  res(wout), res(g2), res(b2), res(wi), res(wo), res(g3), res(b3)],
        out_specs=row(d),
        out_shape=jax.ShapeDtypeStruct((m, d), F32),
        compiler_params=pltpu.CompilerParams(dimension_semantics=("arbitrary",),
                                             vmem_limit_bytes=VMEM_LIMIT_BYTES),
        name="out_ffn",
    )(x, yr, ys, yg, wout, g2, b2, wi, wo, g3, b3)


def _rwkv_kernel(f_ref, shift_ref, s0_ref, mu_ref, w0_ref, wup_ref, a0_ref, aup_ref, gup_ref,
                 kk_ref, ka_ref, rk_ref, gng_ref, gnb_ref, s_acc_ref, sh_acc_ref,
                 y_ref, sout_ref, shout_ref, s_scr, prev_scr, *, bb, c, nsub, tv, nc):
    del s_acc_ref, sh_acc_ref
    ci = pl.program_id(1)

    @pl.when(ci == 0)
    def _():
        s_scr[...] = s0_ref[...]
        prev_scr[...] = shift_ref[...]

    row = lax.broadcasted_iota(jnp.int32, (c, 1), 0)
    ri = lax.broadcasted_iota(jnp.int32, (c, 2 * c), 0)
    cj = lax.broadcasted_iota(jnp.int32, (c, 2 * c), 1)
    cj = jnp.where(cj >= c, cj - c, cj)
    strict = ri > cj
    incl = ri >= cj
    tri = (lax.broadcasted_iota(jnp.int32, (c, c), 0) >= lax.broadcasted_iota(jnp.int32, (c, c), 1))
    tri = jnp.where(tri, 1.0, 0.0).astype(BF16)
    valid = row < tv

    rows = nsub * c
    first_row = lax.broadcasted_iota(jnp.int32, (rows, 1), 0) == 0

    def one_group(seqs):
        n_seq = len(seqs)
        heads = [(n, h) for n in range(n_seq) for h in range(N_HEADS)]
        hs = lambda arr, h: arr[:, h * HEAD_DIM:(h + 1) * HEAD_DIM]
        fl = [f_ref[i] for i in seqs]
        prevs = [jnp.where(first_row, prev_scr[i], pltpu.roll(f, 1, 0)) for i, f in zip(seqs, fl)]
        for i, f in zip(seqs, fl):
            prev_scr[i] = f[rows - 1:rows, :]

        fsl = [f + (p - f) * mu_ref[...] for f, p in zip(fl, prevs)]
        w_l = _mm3_each([jnp.tanh(fs[:, 768:800]) for fs in fsl], [wup_ref[...]] * n_seq)
        lr_l = _mm3_each([fs[:, 800:832] for fs in fsl], [aup_ref[...]] * n_seq)
        gate_l = _mm3_each([_sigmoid(fs[:, 832:896]) for fs in fsl], [gup_ref[...]] * n_seq)
        lr_l = [_sigmoid(a0_ref[...] + x) for x in lr_l]
        lw_l = [-jnp.exp(-_softplus(-(w0_ref[...] + x)) - 0.5) for x in w_l]
        r_l = [fs[:, 0:R_WIDTH] for fs in fsl]
        k_l = [fs[:, R_WIDTH:2 * R_WIDTH] for fs in fsl]
        v_l = [fs[:, 2 * R_WIDTH:3 * R_WIDTH] for fs in fsl]
        kkx_l = [k * kk_ref[...] for k in k_l]
        k2_l = [k * (1.0 + (lr - 1.0) * ka_ref[...]) for k, lr in zip(k_l, lr_l)]
        if tv < c:
            lw_l = [jnp.where(valid, x, 0.0) for x in lw_l]
            kkx_l = [jnp.where(valid, x, 0.0) for x in kkx_l]
            k2_l = [jnp.where(valid, x, 0.0) for x in k2_l]
        cum_l = [jnp.concatenate([_mm_exact_lhs(tri, lw[s * c:(s + 1) * c]) for s in range(nsub)], axis=0)
                 if nsub > 1 else _mm_exact_lhs(tri, lw) for lw in lw_l]
        p_in_l = [jnp.exp(x) for x in cum_l]
        p_prev_l = [jnp.exp(x - lw) for x, lw in zip(cum_l, lw_l)]
        p_inv_l = [jnp.exp(-x) for x in cum_l]

        def intra(s):
            hs = lambda arr, h: arr[s * c:(s + 1) * c, h * HEAD_DIM:(h + 1) * HEAD_DIM]
            xs, y2s, vhs, plast, rk2 = [], [], [], [], []
            for n, h in heads:
                kx = hs(kkx_l[n], h)
                kkh = kx * lax.rsqrt(jnp.sum(kx * kx, axis=-1, keepdims=True) + L2_EPS)
                at = -kkh * hs(p_prev_l[n], h)
                bt = kkh * hs(lr_l[n], h) * hs(p_inv_l[n], h)
                kt = hs(k2_l[n], h) * hs(p_inv_l[n], h)
                qt = hs(r_l[n], h) * hs(p_in_l[n], h)
                xs.append(jnp.concatenate([at, qt], axis=0))
                y2s.append(jnp.concatenate([bt, kt], axis=0))
                vhs.append(hs(v_l[n], h))
                plast.append(hs(p_in_l[n], h)[c - 1:c])
                rk2.append(hs(r_l[n], h) * hs(k2_l[n], h))
            xs = _bf16_each(xs)
            y2s = _bf16_each(y2s)
            gram = _mm1_each(xs, y2s, NT)
            xa = [jnp.where(strict, g[:c], 0.0) for g in gram]
            xq = [jnp.where(incl, g[c:], 0.0) for g in gram]
            lakv = _mm1_each([x[:, c:] for x in xa], vhs)
            t = _tri_inverse_each([x[:, :c] for x in xa], c)
            gate = [hs(gate_l[n], h) for n, h in heads]
            return dict(xs=xs, y2s=y2s, vhs=vhs, xq=xq, lakv=lakv, t=t, plast=plast, rk2=rk2, gate=gate)

        parts = [intra(s) for s in range(nsub)]
        ss = [s_scr[seqs[n], h] for n, h in heads]
        for s, p in enumerate(parts):
            xst = _mm1_each(p["xs"], ss, NT)
            u = _mm1_each(p["t"], [a[:c] + b for a, b in zip(xst, p["lakv"])])
            uv = _bf16_each([jnp.concatenate([a, b], axis=0) for a, b in zip(u, p["vhs"])])
            o_l = [a[c:] + b for a, b in zip(xst, _mm1_each(p["xq"], uv))]
            ds = _mm1_each(uv, p["y2s"], TN)
            ss = [(st + d) * pl_ for st, d, pl_ in zip(ss, ds, p["plast"])]
            for (n, h), o, vh, rk2, gate in zip(heads, o_l, p["vhs"], p["rk2"], p["gate"]):
                sl = slice(h * HEAD_DIM, (h + 1) * HEAD_DIM)
                mean = jnp.mean(o, axis=-1, keepdims=True)
                oc = o - mean
                var = jnp.mean(oc * oc, axis=-1, keepdims=True)
                on = oc * lax.rsqrt(var + R_GN_EPS) * gng_ref[:, sl] + gnb_ref[:, sl]
                bonus = jnp.sum(rk2 * rk_ref[:, sl], axis=-1, keepdims=True) * vh
                y_ref[seqs[n], s * c:(s + 1) * c, sl] = (on + bonus) * gate
        for (n, h), st in zip(heads, ss):
            s_scr[seqs[n], h] = st

        @pl.when(ci == nc - 1)
        def _():
            last = rows - c + tv
            for i, f in zip(seqs, fl):
                shout_ref[i] = f[last - 1:last, :]

    _for_each_group(bb, one_group)

    @pl.when(ci == nc - 1)
    def _():
        sout_ref[...] = s_scr[...]


def _layer_block(shape, l):
    nd = len(shape)
    return pl.BlockSpec((None,) + tuple(shape), lambda i, j=0: (l, i) + (0,) * (nd - 1))


_ANY_SPEC = pl.BlockSpec(memory_space=pl.ANY)


def _stacked_outputs(kern, n_in, accs):
    if accs[0] is None:
        def first(*refs):
            return kern(*refs[:n_in], None, None, *refs[n_in:])
        return first, [], [], {}
    return kern, [_ANY_SPEC, _ANY_SPEC], list(accs), {n_in: 1, n_in + 1: 2}


def _rwkv_mixer(f, shift_all, s_all, l, s_acc, sh_acc, params, *, bb, c, tv, nsub=1):
    b, t, _ = f.shape
    bb = min(bb, b)
    assert b % bb == 0
    rows = nsub * c
    nc = t // rows
    assert nsub == 1 or tv == c
    kern = functools.partial(_rwkv_kernel, bb=bb, c=c, nsub=nsub, tv=tv, nc=nc)
    state_spec = _layer_block((bb, N_HEADS, HEAD_DIM, HEAD_DIM), l)
    shift_spec = _layer_block((bb, 1, R_PROJ), l)
    kern, acc_specs, acc_args, aliases = _stacked_outputs(kern, 3 + len(params), (s_acc, sh_acc))
    return pl.pallas_call(
        kern,
        grid=(b // bb, nc),
        in_specs=[pl.BlockSpec((bb, rows, R_PROJ), lambda i, j: (i, j, 0)), shift_spec, state_spec]
                 + [_full_spec(p.shape) for p in params] + acc_specs,
        out_specs=[pl.BlockSpec((bb, rows, R_WIDTH), lambda i, j: (i, j, 0)), state_spec, shift_spec],
        out_shape=[jax.ShapeDtypeStruct((b, t, R_WIDTH), F32),
                   jax.ShapeDtypeStruct(s_all.shape, F32), jax.ShapeDtypeStruct(shift_all.shape, F32)],
        input_output_aliases=aliases,
        scratch_shapes=[pltpu.VMEM((bb, N_HEADS, HEAD_DIM, HEAD_DIM), F32),
                        pltpu.VMEM((bb, 1, R_PROJ), F32)],
        compiler_params=pltpu.CompilerParams(dimension_semantics=("arbitrary", "arbitrary"),
                                             vmem_limit_bytes=VMEM_LIMIT_BYTES),
        name="rwkv7",
    )(f, shift_all, s_all, *params, *acc_args)


def _gdn_kernel(x_ref, z_ref, gb_ref, cb_ref, s0_ref, cw_ref, alog_ref, dtb_ref, ng_ref, s_acc_ref, cb_acc_ref,
                y_ref, sout_ref, cbout_ref, s_scr, tail_scr, *, bb, c, nsub, tv, nc):
    del s_acc_ref, cb_acc_ref
    ci = pl.program_id(1)

    @pl.when(ci == 0)
    def _():
        s_scr[...] = s0_ref[...]
        tail_scr[...] = jnp.zeros_like(tail_scr)
        tail_scr[:, SUBLANES - (G_CONV - 1):SUBLANES, :] = cb_ref[...]

    row = lax.broadcasted_iota(jnp.int32, (c, 1), 0)
    ri = lax.broadcasted_iota(jnp.int32, (c, c), 0)
    cj = lax.broadcasted_iota(jnp.int32, (c, c), 1)
    strict = ri > cj
    incl = ri >= cj
    tri = jnp.where(incl, 1.0, 0.0).astype(BF16)
    valid = row < tv

    rows = nsub * c

    def one_group(seqs):
        n_seq = len(seqs)
        heads = [(n, h) for n in range(n_seq) for h in range(N_HEADS)]
        chains = [(n, s, h) for s in range(nsub) for n, h in heads]
        sub = lambda a, s: a[s * c:(s + 1) * c]
        x_l = [x_ref[i] for i in seqs]
        act_l = []
        for i, x in zip(seqs, x_l):
            xe = jnp.concatenate([tail_scr[i], x], axis=0)
            conv = x * cw_ref[G_CONV - 1:G_CONV, :]
            for s in range(1, G_CONV):
                conv = conv + pltpu.roll(xe, s, 0)[SUBLANES:, :] * cw_ref[G_CONV - 1 - s:G_CONV - s, :]
            tail_scr[i] = x[rows - SUBLANES:, :]
            act_l.append(_silu(conv))
        gb_l = [gb_ref[i] for i in seqs]
        beta_l = [_sigmoid(g) for g in gb_l]
        g_l = [-jnp.exp(alog_ref[...]) * _softplus(g + dtb_ref[...]) for g in gb_l]
        if tv < c:
            beta_l = [jnp.where(valid, x, 0.0) for x in beta_l]
            g_l = [jnp.where(valid, x, 0.0) for x in g_l]
        pad = lambda g: jnp.concatenate([g, jnp.zeros((LANES - c, LANES), F32)], axis=0) if c < LANES else g
        gcum = {(n, s): _mm_exact_lhs(tri, sub(g_l[n], s)) for s in range(nsub) for n in range(n_seq)}
        gcum_t = {k: pad(g).T for k, g in gcum.items()}
        qegs, khs, vbs, kbs, egs, decs, kdec, elast = [], [], [], [], [], [], [], []
        for n, s, h in chains:
            act = sub(act_l[n], s)
            qx = act[:, h * HEAD_DIM:(h + 1) * HEAD_DIM]
            kx = act[:, G_WIDTH + h * HEAD_DIM:G_WIDTH + (h + 1) * HEAD_DIM]
            vh = act[:, 2 * G_WIDTH + h * HEAD_DIM:2 * G_WIDTH + (h + 1) * HEAD_DIM]
            qh = qx * lax.rsqrt(jnp.sum(qx * qx, axis=-1, keepdims=True) + L2_EPS) * (HEAD_DIM ** -0.5)
            kh = kx * lax.rsqrt(jnp.sum(kx * kx, axis=-1, keepdims=True) + L2_EPS)
            if tv < c:
                kh = jnp.where(valid, kh, 0.0)
            beta = sub(beta_l[n], s)[:, h:h + 1]
            gc = gcum[n, s]
            gcol = gc[:, N_HEADS + h:N_HEADS + h + 1]
            grow = gcum_t[n, s][N_HEADS + h:N_HEADS + h + 1, 0:c]
            glast = gc[c - 1:c, N_HEADS + h:N_HEADS + h + 1]
            decs.append(jnp.exp(jnp.where(incl, gcol - grow, NEG_INF)))
            eg = jnp.exp(gcol)
            qegs.append((qh, eg))
            khs.append(kh)
            kbs.append(kh * beta)
            vbs.append(vh * beta)
            egs.append(eg)
            kdec.append(kh * jnp.exp(glast - gcol))
            elast.append(jnp.exp(glast))
        sol, qk = [], []
        for s in range(nsub):
            pick = lambda lst: lst[s * len(heads):(s + 1) * len(heads)]
            gram = _mm1_each([jnp.concatenate([kb, qh], axis=0) for kb, (qh, _) in zip(pick(kbs), pick(qegs))],
                             pick(khs), NT)
            t = _tri_inverse_each([-jnp.where(strict, g[:c] * d, 0.0) for g, d in zip(gram, pick(decs))], c)
            sol += _mm3_each(t, [jnp.concatenate([vb, kb * eg], axis=1)
                                 for vb, kb, eg in zip(pick(vbs), pick(kbs), pick(egs))])
            qk += [g[c:] * d for g, d in zip(gram, pick(decs))]
        wq = [jnp.concatenate([so[:, HEAD_DIM:], qh * eg], axis=0) for so, (qh, eg) in zip(sol, qegs)]
        ss = [s_scr[seqs[n], h] for n, h in heads]
        for s in range(nsub):
            pick = lambda lst: lst[s * len(heads):(s + 1) * len(heads)]
            ws = _mm3_each(pick(wq), ss)
            u = [so[:, :HEAD_DIM] - w[:c] for so, w in zip(pick(sol), ws)]
            qku = _mm1_each(pick(qk), u)
            ds = _mm3_each(pick(kdec), u, TN)
            ss = [el * st + d for el, st, d in zip(pick(elast), ss, ds)]
            for (n, h), w, qu in zip(heads, ws, qku):
                sl = slice(h * HEAD_DIM, (h + 1) * HEAD_DIM)
                o = w[c:] + qu
                o = o * lax.rsqrt(jnp.mean(o * o, axis=-1, keepdims=True) + G_NORM_EPS) * ng_ref[...]
                y_ref[seqs[n], s * c:(s + 1) * c, sl] = o * _silu(z_ref[seqs[n], s * c:(s + 1) * c, sl])
        for (n, h), st in zip(heads, ss):
            s_scr[seqs[n], h] = st

        @pl.when(ci == nc - 1)
        def _():
            shift = (rows - (rows - c + tv - (G_CONV - 1))) % rows
            for i, x in zip(seqs, x_l):
                xs = pltpu.roll(x, shift, 0) if shift else x
                cbout_ref[i] = xs[0:G_CONV - 1, :]

    _for_each_group(bb, one_group)

    @pl.when(ci == nc - 1)
    def _():
        sout_ref[...] = s_scr[...]


def _gdn_mixer(pg, cb_all, s_all, l, s_acc, cb_acc, params, *, bb, c, tv, nsub=1):
    b, t, _ = pg.shape
    bb = min(bb, b)
    assert b % bb == 0
    rows = nsub * c
    nc = t // rows
    assert nsub == 1 or tv == c
    kern = functools.partial(_gdn_kernel, bb=bb, c=c, nsub=nsub, tv=tv, nc=nc)
    state_spec = _layer_block((bb, N_HEADS, HEAD_DIM, HEAD_DIM), l)
    cb_spec = _layer_block((bb, G_CONV - 1, G_QKV), l)
    kern, acc_specs, acc_args, aliases = _stacked_outputs(kern, 5 + len(params), (s_acc, cb_acc))
    return pl.pallas_call(
        kern,
        grid=(b // bb, nc),
        in_specs=[pl.BlockSpec((bb, rows, G_QKV), lambda i, j: (i, j, 0)),
                  pl.BlockSpec((bb, rows, G_WIDTH), lambda i, j: (i, j, G_QKV // G_WIDTH)),
                  pl.BlockSpec((bb, rows, LANES), lambda i, j: (i, j, (G_QKV + G_WIDTH) // LANES)),
                  cb_spec, state_spec] + [_full_spec(p.shape) for p in params] + acc_specs,
        out_specs=[pl.BlockSpec((bb, rows, G_WIDTH), lambda i, j: (i, j, 0)), state_spec, cb_spec],
        out_shape=[jax.ShapeDtypeStruct((b, t, G_WIDTH), F32),
                   jax.ShapeDtypeStruct(s_all.shape, F32), jax.ShapeDtypeStruct(cb_all.shape, F32)],
        input_output_aliases=aliases,
        scratch_shapes=[pltpu.VMEM((bb, N_HEADS, HEAD_DIM, HEAD_DIM), F32),
                        pltpu.VMEM((bb, SUBLANES, G_QKV), F32)],
        compiler_params=pltpu.CompilerParams(dimension_semantics=("arbitrary", "arbitrary"),
                                             vmem_limit_bytes=VMEM_LIMIT_BYTES),
        name="gdn",
    )(pg, pg, pg, cb_all, s_all, *params, *acc_args)


def _to_feature_major(a, width):
    return jnp.concatenate([a[:, i * LANES:(i + 1) * LANES].T for i in range(width // LANES)], axis=0)


def _to_batch_major(a, width):
    return jnp.concatenate([a[i * LANES:(i + 1) * LANES, :].T for i in range(width // LANES)], axis=1)


def _gdn_sample_kernel(alog_ref, dtb_ref, x_ref, cb_ref, s_ref, cwt_ref, ng_ref, s_acc_ref, cb_acc_ref,
                       y_ref, sout_ref, cbout_ref, kq_scr, *, tv, t_pad):
    del s_acc_ref, cb_acc_ref
    nb = x_ref.shape[1]
    xs = [x_ref[t] for t in range(tv)]
    full = ([_to_feature_major(cb_ref[j], G_QKV) for j in range(G_CONV - 1)]
            + [_to_feature_major(x[:, :G_QKV], G_QKV) for x in xs])
    zt = [_to_feature_major(x[:, G_QKV:G_QKV + G_WIDTH], G_WIDTH) for x in xs]
    gbt = [x[:, G_QKV + G_WIDTH:].T for x in xs]
    wcol = [jnp.broadcast_to(cwt_ref[:, j:j + 1], (G_QKV, nb)) for j in range(G_CONV)]
    zero = jnp.zeros((HEAD_DIM, nb), F32)
    for t in range(tv):
        conv = full[t] * wcol[0]
        for j in range(1, G_CONV):
            conv = conv + full[t + j] * wcol[j]
        act = _silu(conv)
        y_heads = []
        for h in range(N_HEADS):
            hs = slice(h * HEAD_DIM, (h + 1) * HEAD_DIM)
            qx = act[h * HEAD_DIM:(h + 1) * HEAD_DIM]
            kx = act[G_WIDTH + h * HEAD_DIM:G_WIDTH + (h + 1) * HEAD_DIM]
            vh = act[2 * G_WIDTH + h * HEAD_DIM:2 * G_WIDTH + (h + 1) * HEAD_DIM]
            qh = qx * lax.rsqrt(jnp.sum(qx * qx, axis=0, keepdims=True) + L2_EPS) * (HEAD_DIM ** -0.5)
            kh = kx * lax.rsqrt(jnp.sum(kx * kx, axis=0, keepdims=True) + L2_EPS)
            beta = _sigmoid(gbt[t][h:h + 1, :])
            neg_a = -jnp.exp(jnp.full((1, nb), alog_ref[h], F32))
            eg = jnp.exp(neg_a * _softplus(gbt[t][N_HEADS + h:N_HEADS + h + 1, :] + dtb_ref[h]))
            slot = t * N_HEADS + h
            kq_scr[slot, 0] = kh
            kq_scr[slot, 1] = qh
            src = s_ref if t == 0 else sout_ref

            def pass1(k, acc, h=h, slot=slot, src=src):
                return acc + src[h, k] * kq_scr[slot, 0, pl.ds(k, 1), :]

            stk = lax.fori_loop(0, HEAD_DIM, pass1, zero, unroll=8)
            u = beta * (vh - eg * stk)

            def pass2(k, acc, h=h, slot=slot, src=src, eg=eg, u=u):
                sn = eg * src[h, k] + kq_scr[slot, 0, pl.ds(k, 1), :] * u
                sout_ref[h, k] = sn
                return acc + sn * kq_scr[slot, 1, pl.ds(k, 1), :]

            o = lax.fori_loop(0, HEAD_DIM, pass2, zero, unroll=8)
            o = o * lax.rsqrt(jnp.mean(o * o, axis=0, keepdims=True) + G_NORM_EPS) * ng_ref[...]
            y_heads.append(o * _silu(zt[t][hs]))
        y_ref[t] = _to_batch_major(jnp.concatenate(y_heads, axis=0), G_WIDTH)
    for t in range(tv, t_pad):
        y_ref[t] = jnp.zeros((nb, G_WIDTH), F32)
    for j in range(G_CONV - 1):
        cbout_ref[j] = _to_batch_major(full[tv + j], G_QKV)


def _gdn_sample(x_tm, cb_tm, s_nat, l, s_acc, cb_acc, params, *, t_pad):
    tv, nb, _ = x_tm.shape
    alog, dtb, cwt, ng = params
    kern = functools.partial(_gdn_sample_kernel, tv=tv, t_pad=t_pad)
    kern, acc_specs, acc_args, aliases = _stacked_outputs(kern, 7, (s_acc, cb_acc))
    state_spec = pl.BlockSpec((None, N_HEADS, HEAD_DIM, HEAD_DIM, nb), lambda i: (l, 0, 0, 0, 0))
    cb_spec = pl.BlockSpec((None, G_CONV - 1, nb, G_QKV), lambda i: (l, 0, 0, 0))
    smem = pl.BlockSpec(memory_space=pltpu.SMEM)
    return pl.pallas_call(
        kern,
        grid=(1,),
        in_specs=[smem, smem, _full_spec(x_tm.shape), cb_spec, state_spec,
                  _full_spec(cwt.shape), _full_spec(ng.shape)] + acc_specs,
        out_specs=[_full_spec((t_pad, nb, G_WIDTH)), state_spec, cb_spec],
        out_shape=[jax.ShapeDtypeStruct((t_pad, nb, G_WIDTH), F32),
                   jax.ShapeDtypeStruct(s_nat.shape, F32), jax.ShapeDtypeStruct(cb_tm.shape, F32)],
        input_output_aliases=aliases,
        scratch_shapes=[pltpu.VMEM((tv * N_HEADS, 2, HEAD_DIM, nb), F32)],
        compiler_params=pltpu.CompilerParams(dimension_semantics=("arbitrary",),
                                             vmem_limit_bytes=VMEM_LIMIT_BYTES),
        name="gdn_sample",
    )(alog, dtb, x_tm, cb_tm, s_nat, cwt, ng, *acc_args)


def _rwkv_sample_kernel(x_ref, shift_ref, s_ref, mu_ref, w0_ref, wupt_ref, a0_ref, aupt_ref, gupt_ref,
                        kk_ref, ka_ref, rk_ref, gng_ref, gnb_ref, s_acc_ref, sh_acc_ref,
                        y_ref, sout_ref, shout_ref, vec_scr, y_scr, *, tv, t_pad):
    del s_acc_ref, sh_acc_ref
    nb = x_ref.shape[1]
    xs = [x_ref[t] for t in range(tv)]
    shout_ref[...] = xs[tv - 1]
    prev = _to_feature_major(shift_ref[...], R_PROJ)
    per_step = []
    for t in range(tv):
        f = _to_feature_major(xs[t], R_PROJ)
        fs = f + (prev - f) * mu_ref[...]
        prev = f
        r = fs[0:R_WIDTH]
        k = fs[R_WIDTH:2 * R_WIDTH]
        v = fs[2 * R_WIDTH:3 * R_WIDTH]
        w = w0_ref[...] + _mm3_each([wupt_ref[...]], [jnp.tanh(fs[768:800])])[0]
        lr = _sigmoid(a0_ref[...] + _mm3_each([aupt_ref[...]], [fs[800:832]])[0])
        gate = _mm3_each([gupt_ref[...]], [_sigmoid(fs[832:896])])[0]
        decay = jnp.exp(-jnp.exp(-_softplus(-w) - 0.5))
        kkx = k * kk_ref[...]
        k2 = k * (1.0 + (lr - 1.0) * ka_ref[...])
        bonus = []
        for h in range(N_HEADS):
            hs = slice(h * HEAD_DIM, (h + 1) * HEAD_DIM)
            kx = kkx[hs]
            kkh = kx * lax.rsqrt(jnp.sum(kx * kx, axis=0, keepdims=True) + L2_EPS)
            vec_scr[t, h, 0] = -kkh
            vec_scr[t, h, 1] = decay[hs]
            vec_scr[t, h, 2] = kkh * lr[hs]
            vec_scr[t, h, 3] = k2[hs]
            vec_scr[t, h, 4] = r[hs]
            vec_scr[t, h, 5] = v[hs]
            bonus.append(jnp.sum(r[hs] * k2[hs] * rk_ref[hs, :], axis=0, keepdims=True) * v[hs])
        per_step.append((jnp.concatenate(bonus, axis=0), gate))

    for h in range(N_HEADS):
        def row(vi, carry, h=h):
            s = s_ref[h, vi]
            for t in range(tv):
                sa = jnp.sum(s * vec_scr[t, h, 0], axis=0, keepdims=True)
                s = (s * vec_scr[t, h, 1] + sa * vec_scr[t, h, 2]
                     + vec_scr[t, h, 5, pl.ds(vi, 1), :] * vec_scr[t, h, 3])
                y_scr[t, h, pl.ds(vi, 1), :] = jnp.sum(s * vec_scr[t, h, 4], axis=0, keepdims=True)
            sout_ref[h, vi] = s
            return carry

        lax.fori_loop(0, HEAD_DIM, row, 0, unroll=2)

    for t in range(tv):
        bonus, gate = per_step[t]
        outs = []
        for h in range(N_HEADS):
            hs = slice(h * HEAD_DIM, (h + 1) * HEAD_DIM)
            o = y_scr[t, h]
            oc = o - jnp.mean(o, axis=0, keepdims=True)
            var = jnp.mean(oc * oc, axis=0, keepdims=True)
            outs.append(oc * lax.rsqrt(var + R_GN_EPS) * gng_ref[hs, :] + gnb_ref[hs, :])
        y = (jnp.concatenate(outs, axis=0) + bonus) * gate
        y_ref[t] = _to_batch_major(y, R_WIDTH)
    for t in range(tv, t_pad):
        y_ref[t] = jnp.zeros((nb, R_WIDTH), F32)


def _rwkv_sample(x_tm, shift_all, s_nat, l, s_acc, sh_acc, params, *, t_pad):
    tv, nb, _ = x_tm.shape
    kern = functools.partial(_rwkv_sample_kernel, tv=tv, t_pad=t_pad)
    kern, acc_specs, acc_args, aliases = _stacked_outputs(kern, 3 + len(params), (s_acc, sh_acc))
    state_spec = pl.BlockSpec((None, N_HEADS, HEAD_DIM, HEAD_DIM, nb), lambda i: (l, 0, 0, 0, 0))
    shift_spec = pl.BlockSpec((None, nb, R_PROJ), lambda i: (l, 0, 0))
    return pl.pallas_call(
        kern,
        grid=(1,),
        in_specs=[_full_spec(x_tm.shape), shift_spec, state_spec] + [_full_spec(p.shape) for p in params] + acc_specs,
        out_specs=[_full_spec((t_pad, nb, R_WIDTH)), state_spec, shift_spec],
        out_shape=[jax.ShapeDtypeStruct((t_pad, nb, R_WIDTH), F32),
                   jax.ShapeDtypeStruct(s_nat.shape, F32), jax.ShapeDtypeStruct(shift_all.shape, F32)],
        input_output_aliases=aliases,
        scratch_shapes=[pltpu.VMEM((tv, N_HEADS, 6, HEAD_DIM, nb), F32),
                        pltpu.VMEM((tv, N_HEADS, HEAD_DIM, nb), F32)],
        compiler_params=pltpu.CompilerParams(dimension_semantics=("arbitrary",),
                                             vmem_limit_bytes=VMEM_LIMIT_BYTES),
        name="rwkv7_sample",
    )(x_tm, shift_all, s_nat, *params, *acc_args)


def _rel_buckets(dist):
    n = np.maximum(dist, 0)
    nf = np.maximum(n, 1).astype(np.float32)
    large = REL_MAX_EXACT + (np.log(nf / REL_MAX_EXACT) / math.log(REL_MAX_DIST / REL_MAX_EXACT)
                             * (REL_BUCKETS - REL_MAX_EXACT)).astype(np.int32)
    return np.where(n < REL_MAX_EXACT, n, np.minimum(large, REL_BUCKETS - 1)).astype(np.int32)


def _bias_kernel(table_ref, bucket_ref, o_ref):
    bucket = bucket_ref[...]
    for h in range(S_Q_HEADS):
        acc = jnp.full(bucket.shape, NEG_INF, F32)
        for k in range(REL_BUCKETS):
            acc = jnp.where(bucket == k, table_ref[k, h], acc)
        o_ref[h] = acc


def _rel_bias(table, dist, visible=None):
    bucket = _rel_buckets(dist)
    if visible is not None:
        bucket = np.where(visible, bucket, -1).astype(np.int32)
    bucket = jnp.asarray(bucket)
    return pl.pallas_call(
        _bias_kernel,
        in_specs=[pl.BlockSpec(memory_space=pltpu.SMEM), _full_spec(bucket.shape)],
        out_specs=_full_spec((S_Q_HEADS,) + bucket.shape),
        out_shape=jax.ShapeDtypeStruct((S_Q_HEADS,) + bucket.shape, F32),
        grid=(1,),
        name="rel_bias",
    )(table, bucket)


Q_BLOCKS_PER_STEP = 2


def _swa_prompt_kernel(sink_ref, q_ref, kp_ref, kc_ref, vp_ref, vc_ref, bias_ref, o_ref):
    n = pl.program_id(1)
    q = q_ref[...]
    kall = jnp.concatenate([kp_ref[...], kc_ref[...]], axis=0).astype(BF16)
    vall = jnp.concatenate([vp_ref[...], vc_ref[...]], axis=0).astype(BF16)
    chains = [(s, h) for s in range(Q_BLOCKS_PER_STEP) for h in range(S_Q_HEADS)]
    band = lambda a, s, h: a[s * ATTN_BLOCK:(s + 2) * ATTN_BLOCK,
                             (h // S_GROUP) * HEAD_DIM:(h // S_GROUP + 1) * HEAD_DIM]
    from_prev = (lax.broadcasted_iota(jnp.int32, (ATTN_BLOCK, ATTN_BLOCK), 1)
                 > lax.broadcasted_iota(jnp.int32, (ATTN_BLOCK, ATTN_BLOCK), 0))
    has_prev = [jnp.where(n == 0, 0, 1) if s == 0 else 1 for s in range(Q_BLOCKS_PER_STEP)]
    qk = [_dg(q[s * ATTN_BLOCK:(s + 1) * ATTN_BLOCK, h * HEAD_DIM:(h + 1) * HEAD_DIM].astype(BF16),
              band(kall, s, h), NT) for s, h in chains]
    logits = [jnp.where(from_prev, x[:, :ATTN_BLOCK], x[:, ATTN_BLOCK:]) * (HEAD_DIM ** -0.5)
              + bias_ref[has_prev[s], h] for x, (s, h) in zip(qk, chains)]
    m = [jnp.maximum(jnp.max(x, axis=-1, keepdims=True), sink_ref[h]) for x, (s, h) in zip(logits, chains)]
    p = [jnp.exp(x - mm) for x, mm in zip(logits, m)]
    probs = [x / (jnp.sum(x, axis=-1, keepdims=True) + jnp.exp(sink_ref[h] - mm))
             for x, mm, (s, h) in zip(p, m, chains)]
    for x, (s, h) in zip(probs, chains):
        pband = jnp.concatenate([jnp.where(from_prev, x, 0.0), jnp.where(from_prev, 0.0, x)], axis=1)
        o_ref[s * ATTN_BLOCK:(s + 1) * ATTN_BLOCK, h * HEAD_DIM:(h + 1) * HEAD_DIM] = _dg(
            pband.astype(BF16), band(vall, s, h), NN)


def _swa_prompt(ps, sinks, bias):
    b, t, _ = ps.shape
    rows = Q_BLOCKS_PER_STEP * ATTN_BLOCK
    kcol = S_WIDTH // S_KV_WIDTH
    prev = lambda j: jnp.maximum(Q_BLOCKS_PER_STEP * j - 1, 0)
    return pl.pallas_call(
        _swa_prompt_kernel,
        grid=(b, t // rows),
        in_specs=[pl.BlockSpec(memory_space=pltpu.SMEM),
                  pl.BlockSpec((None, rows, S_WIDTH), lambda i, j: (i, j, 0)),
                  pl.BlockSpec((None, ATTN_BLOCK, S_KV_WIDTH), lambda i, j: (i, prev(j), kcol)),
                  pl.BlockSpec((None, rows, S_KV_WIDTH), lambda i, j: (i, j, kcol)),
                  pl.BlockSpec((None, ATTN_BLOCK, S_KV_WIDTH), lambda i, j: (i, prev(j), kcol + 1)),
                  pl.BlockSpec((None, rows, S_KV_WIDTH), lambda i, j: (i, j, kcol + 1)),
                  _full_spec(bias.shape)],
        out_specs=pl.BlockSpec((None, rows, S_WIDTH), lambda i, j: (i, j, 0)),
        out_shape=jax.ShapeDtypeStruct((b, t, S_WIDTH), F32),
        compiler_params=pltpu.CompilerParams(dimension_semantics=("arbitrary", "arbitrary"),
                                             vmem_limit_bytes=VMEM_LIMIT_BYTES),
        name="swa_prompt",
    )(sinks, ps, ps, ps, ps, ps, bias)


def _swa_sample_kernel(sink_ref, q_ref, kn_ref, vn_ref, ck_ref, cv_ref, bias_c_ref, bias_n_ref,
                       k_acc_ref, v_acc_ref, o_ref, cko_ref, cvo_ref, *, bb, tp, tv):
    del k_acc_ref, v_acc_ref
    wc = WINDOW
    step = lambda n: jnp.concatenate([lax.broadcasted_iota(jnp.int32, (tp, n), 0)] * S_GROUP, axis=0)
    ti_c = step(wc)
    kj_c = lax.broadcasted_iota(jnp.int32, (S_GROUP * tp, wc), 1)
    dist_c = wc + ti_c - kj_c
    mask_c = (dist_c >= 0) & (dist_c < WINDOW)
    ti_n = step(tp)
    kj_n = lax.broadcasted_iota(jnp.int32, (S_GROUP * tp, tp), 1)
    dist_n = ti_n - kj_n
    mask_n = (dist_n >= 0) & (dist_n < WINDOW) & (kj_n < tv)

    sinks = [jnp.concatenate([jnp.full((tp, 1), sink_ref[kv * S_GROUP + g], F32) for g in range(S_GROUP)], axis=0)
             for kv in range(S_KV_HEADS)]
    old_cols = lax.broadcasted_iota(jnp.int32, (1, wc), 1) < wc - tv
    scale = HEAD_DIM ** -0.5

    def one_group(seqs):
        chains = [(n, kv) for n in range(len(seqs)) for kv in range(S_KV_HEADS)]
        kvs = lambda a, kv: a[:, kv * HEAD_DIM:(kv + 1) * HEAD_DIM]
        kvt = lambda a, kv: a[kv * HEAD_DIM:(kv + 1) * HEAD_DIM, :]
        q_l = [q_ref[i] for i in seqs]
        kn_l = [kn_ref[i] for i in seqs]
        vn_l = [vn_ref[i] for i in seqs]
        ck_l = [ck_ref[i] for i in seqs]
        cv_l = [cv_ref[i] for i in seqs]
        qg = [jnp.concatenate([q_l[n][:, (kv * S_GROUP + g) * HEAD_DIM:(kv * S_GROUP + g + 1) * HEAD_DIM]
                               for g in range(S_GROUP)], axis=0).astype(BF16) for n, kv in chains]
        lc = [_dg(x, kvt(ck_l[n], kv).astype(BF16), NN) for x, (n, kv) in zip(qg, chains)]
        ln = [_dg(x, kvs(kn_l[n], kv).astype(BF16), NT) for x, (n, kv) in zip(qg, chains)]
        lc = [jnp.where(mask_c, x * scale + bias_c_ref[kv], NEG_INF) for x, (n, kv) in zip(lc, chains)]
        ln = [jnp.where(mask_n, x * scale + bias_n_ref[kv], NEG_INF) for x, (n, kv) in zip(ln, chains)]
        m = [jnp.maximum(jnp.maximum(jnp.max(a, axis=-1, keepdims=True), jnp.max(b, axis=-1, keepdims=True)),
                         sinks[kv]) for a, b, (n, kv) in zip(lc, ln, chains)]
        pc = [jnp.exp(a - mm) for a, mm in zip(lc, m)]
        pn = [jnp.exp(a - mm) for a, mm in zip(ln, m)]
        den = [jnp.sum(a, axis=-1, keepdims=True) + jnp.sum(b, axis=-1, keepdims=True) + jnp.exp(sinks[kv] - mm)
               for a, b, mm, (n, kv) in zip(pc, pn, m, chains)]
        oc = [_dg((a / d).astype(BF16), kvt(cv_l[n], kv).astype(BF16), NT) for a, d, (n, kv) in zip(pc, den, chains)]
        on = [_dg((a / d).astype(BF16), kvs(vn_l[n], kv).astype(BF16), NN) for a, d, (n, kv) in zip(pn, den, chains)]
        for a, b, (n, kv) in zip(oc, on, chains):
            og = a + b
            for g in range(S_GROUP):
                hq = kv * S_GROUP + g
                o_ref[seqs[n], :, hq * HEAD_DIM:(hq + 1) * HEAD_DIM] = og[g * tp:(g + 1) * tp, :]
        zpad = jnp.zeros((wc - tp, S_KV_WIDTH), F32)
        as_cols = lambda new: jnp.concatenate([new, zpad], axis=0).T
        for n, i in enumerate(seqs):
            cko_ref[i] = jnp.where(old_cols, pltpu.roll(ck_l[n], wc - tv, 1),
                                   pltpu.roll(as_cols(kn_l[n]), wc - tv, 1))
            cvo_ref[i] = jnp.where(old_cols, pltpu.roll(cv_l[n], wc - tv, 1),
                                   pltpu.roll(as_cols(vn_l[n]), wc - tv, 1))

    _for_each_group(bb, one_group)


def _swa_sample(ps, ck_all, cv_all, l, k_acc, v_acc, sinks, bias_c, bias_n, *, bb, tv):
    b, tp, _ = ps.shape
    bb = min(bb, b)
    assert b % bb == 0
    kcol = S_WIDTH // S_KV_WIDTH
    kern = functools.partial(_swa_sample_kernel, bb=bb, tp=tp, tv=tv)
    kern, acc_specs, acc_args, aliases = _stacked_outputs(kern, 8, (k_acc, v_acc))
    cache_spec = _layer_block((bb, S_KV_WIDTH, WINDOW), l)
    return pl.pallas_call(
        kern,
        grid=(b // bb,),
        in_specs=[pl.BlockSpec(memory_space=pltpu.SMEM),
                  pl.BlockSpec((bb, tp, S_WIDTH), lambda i: (i, 0, 0)),
                  pl.BlockSpec((bb, tp, S_KV_WIDTH), lambda i: (i, 0, kcol)),
                  pl.BlockSpec((bb, tp, S_KV_WIDTH), lambda i: (i, 0, kcol + 1)),
                  cache_spec, cache_spec, _full_spec(bias_c.shape), _full_spec(bias_n.shape)] + acc_specs,
        out_specs=[pl.BlockSpec((bb, tp, S_WIDTH), lambda i: (i, 0, 0)), cache_spec, cache_spec],
        out_shape=[jax.ShapeDtypeStruct((b, tp, S_WIDTH), F32),
                   jax.ShapeDtypeStruct(ck_all.shape, F32), jax.ShapeDtypeStruct(cv_all.shape, F32)],
        input_output_aliases=aliases,
        compiler_params=pltpu.CompilerParams(dimension_semantics=("arbitrary",),
                                             vmem_limit_bytes=VMEM_LIMIT_BYTES),
        name="swa_sample",
    )(sinks, ps, ps, ps, ck_all, cv_all, bias_c, bias_n, *acc_args)


def _prep_layer(l, ffn1_w_in, ffn1_w_out, ln1_g, ln1_b, w_in, rwkv_mu, rwkv_w0, rwkv_w_up, rwkv_a0,
                rwkv_a_up, rwkv_g_up, rwkv_k_k, rwkv_k_a, rwkv_r_k, rwkv_gn_g, rwkv_gn_b, swa_sinks,
                gdn_conv_w, gdn_a_log, gdn_dt_bias, gdn_norm_g, w_out, ln2_g, ln2_b,
                ffn2_w_in, ffn2_w_out, ln3_g, ln3_b):
    row = lambda a: a[l].reshape(1, -1)
    col = lambda a: a[l].reshape(-1, 1)
    d = w_in.shape[1]
    n_beta = 4 * G_WIDTH
    win = jnp.concatenate([w_in[l], jnp.zeros((d, PROJ_PAD - w_in.shape[2]), F32)], axis=1).astype(BF16)
    lane_tile = lambda a: jnp.zeros((1, LANES), F32).at[0, N_HEADS:2 * N_HEADS].set(a[l])
    del n_beta
    return dict(
        ffn1=(ffn1_w_in[l].astype(BF16), ffn1_w_out[l].astype(BF16), row(ln1_g), row(ln1_b), win),
        rwkv=(row(rwkv_mu), row(rwkv_w0), rwkv_w_up[l], row(rwkv_a0), rwkv_a_up[l], rwkv_g_up[l],
              row(rwkv_k_k), row(rwkv_k_a), row(rwkv_r_k), row(rwkv_gn_g), row(rwkv_gn_b)),
        rwkv_sample=(col(rwkv_mu), col(rwkv_w0), rwkv_w_up[l].T, col(rwkv_a0), rwkv_a_up[l].T, rwkv_g_up[l].T,
                     col(rwkv_k_k), col(rwkv_k_a), col(rwkv_r_k), col(rwkv_gn_g), col(rwkv_gn_b)),
        sinks=swa_sinks[l],
        gdn=(gdn_conv_w[l], lane_tile(gdn_a_log), lane_tile(gdn_dt_bias), row(gdn_norm_g)),
        gdn_sample=(gdn_a_log[l], gdn_dt_bias[l], gdn_conv_w[l].T, gdn_norm_g[l].reshape(HEAD_DIM, 1)),
        out=(w_out[l].astype(BF16), row(ln2_g), row(ln2_b), ffn2_w_in[l].astype(BF16),
             ffn2_w_out[l].astype(BF16), row(ln3_g), row(ln3_b)),
    )


def _run_trunk(x, rwkv_s, rwkv_shift, swa_k, swa_v, gdn_s, gdn_conv, layers, biases, *, prompt, tv, alpha,
               tm, n_chunks, bb, bb_rwkv, c, nsub):
    b, t, d = x.shape
    depth = len(layers)
    xf = x.reshape(b * t, d)
    s_r = sh_r = s_g = cb = None
    kc, vc = ([], []) if prompt else (None, None)
    if prompt:
        rwkv_shift = rwkv_shift.reshape(depth, b, 1, R_PROJ)
    else:
        rwkv_s = rwkv_s.transpose(0, 2, 3, 4, 1)
        gdn_s = gdn_s.transpose(0, 2, 3, 4, 1)
        gdn_conv = gdn_conv.transpose(0, 2, 1, 3)
    for l, lp in enumerate(layers):
        x1, pr, ps, pg = _ffn_proj(xf, *lp["ffn1"], alpha=alpha, tm=tm, n_chunks=n_chunks)
        pr = pr.reshape(b, t, R_PROJ)
        ps = ps.reshape(b, t, S_PROJ)
        pg = pg.reshape(b, t, G_PROJ_PAD)
        if prompt:
            y_r, s_r, sh_r = _rwkv_mixer(pr, rwkv_shift, rwkv_s, l, s_r, sh_r, lp["rwkv"], bb=bb_rwkv, c=c, tv=tv,
                                         nsub=nsub)
        else:
            y_tm, s_r, sh_r = _rwkv_sample(pr[:, :tv].transpose(1, 0, 2), rwkv_shift, rwkv_s, l, s_r, sh_r,
                                           lp["rwkv_sample"], t_pad=t)
            y_r = y_tm.transpose(1, 0, 2)
        if prompt:
            y_s = _swa_prompt(ps, lp["sinks"], biases[0])
            kc.append(ps[:, t - WINDOW:, S_WIDTH:S_WIDTH + S_KV_WIDTH])
            vc.append(ps[:, t - WINDOW:, S_WIDTH + S_KV_WIDTH:])
        else:
            y_s, kc, vc = _swa_sample(ps, swa_k, swa_v, l, kc, vc, lp["sinks"], biases[1], biases[2], bb=bb, tv=tv)
        if prompt:
            y_g, s_g, cb = _gdn_mixer(pg, gdn_conv, gdn_s, l, s_g, cb, lp["gdn"], bb=bb, c=c, tv=tv, nsub=nsub)
        else:
            y_tm, s_g, cb = _gdn_sample(pg[:, :tv].transpose(1, 0, 2), gdn_conv, gdn_s, l, s_g, cb,
                                        lp["gdn_sample"], t_pad=t)
            y_g = y_tm.transpose(1, 0, 2)
        xf = _out_ffn(x1, y_r.reshape(b * t, R_WIDTH), y_s.reshape(b * t, S_WIDTH), y_g.reshape(b * t, G_WIDTH),
                      *lp["out"], alpha=alpha, tm=tm, n_chunks=n_chunks)
    if prompt:
        kc, vc = jnp.stack(kc, axis=0), jnp.stack(vc, axis=0)
    else:
        s_r = s_r.transpose(0, 4, 1, 2, 3)
        kc, vc = kc.transpose(0, 1, 3, 2), vc.transpose(0, 1, 3, 2)
        s_g = s_g.transpose(0, 4, 1, 2, 3)
        cb = cb.transpose(0, 2, 1, 3)
    cache_shape = (depth, b, WINDOW, S_KV_HEADS, HEAD_DIM)
    return xf.reshape(b, t, d), [s_r, sh_r.reshape(depth, b, R_PROJ), kc.reshape(cache_shape),
                                 vc.reshape(cache_shape), s_g, cb]


def kernel(x_prompt, x_sample, state_rwkv, state_rwkv_shift, cache_swa_k, cache_swa_v, state_gdn, state_gdn_conv, ffn1_w_in, ffn1_w_out, ln1_g, ln1_b, w_in, rwkv_mu, rwkv_w0, rwkv_w_up, rwkv_a0, rwkv_a_up, rwkv_g_up, rwkv_k_k, rwkv_k_a, rwkv_r_k, rwkv_gn_g, rwkv_gn_b, swa_sinks, rel_table, gdn_conv_w, gdn_a_log, gdn_dt_bias, gdn_norm_g, w_out, ln2_g, ln2_b, ffn2_w_in, ffn2_w_out, ln3_g, ln3_b):
    depth = ffn1_w_in.shape[0]
    alpha = (2 * depth) ** 0.25
    layers = [_prep_layer(l, ffn1_w_in, ffn1_w_out, ln1_g, ln1_b, w_in, rwkv_mu, rwkv_w0, rwkv_w_up, rwkv_a0,
                          rwkv_a_up, rwkv_g_up, rwkv_k_k, rwkv_k_a, rwkv_r_k, rwkv_gn_g, rwkv_gn_b, swa_sinks,
                          gdn_conv_w, gdn_a_log, gdn_dt_bias, gdn_norm_g, w_out, ln2_g, ln2_b,
                          ffn2_w_in, ffn2_w_out, ln3_g, ln3_b) for l in range(depth)]
    bp, tp_len, d = x_prompt.shape
    bs, ts, _ = x_sample.shape
    ts_pad = -(-ts // SUBLANES) * SUBLANES

    assert WINDOW == ATTN_BLOCK
    qi = np.arange(ATTN_BLOCK)[:, None]
    kc = np.arange(ATTN_BLOCK)[None, :]
    dist_p = np.where(kc > qi, ATTN_BLOCK + qi - kc, qi - kc)
    bias_p = jnp.stack([_rel_bias(rel_table, dist_p, kc <= qi), _rel_bias(rel_table, dist_p)], axis=0)
    ti = (np.arange(S_GROUP * ts_pad) % ts_pad)[:, None]
    wc = cache_swa_k.shape[2]
    bias_c = _rel_bias(rel_table, wc + ti - np.arange(wc)[None, :])
    bias_n = _rel_bias(rel_table, ti - np.arange(ts_pad)[None, :])
    regroup = lambda a: jnp.stack([jnp.concatenate([a[kv * S_GROUP + g, g * ts_pad:(g + 1) * ts_pad]
                                                    for g in range(S_GROUP)], axis=0)
                                   for kv in range(S_KV_HEADS)], axis=0)
    biases = (bias_p, regroup(bias_c), regroup(bias_n))

    zeros = lambda *s: jnp.zeros((depth, bp) + s, F32)
    y_prompt, p_states = _run_trunk(
        x_prompt, zeros(N_HEADS, HEAD_DIM, HEAD_DIM), zeros(R_PROJ), None, None,
        zeros(N_HEADS, HEAD_DIM, HEAD_DIM), zeros(G_CONV - 1, G_QKV), layers, biases,
        prompt=True, tv=64, alpha=alpha, tm=512, n_chunks=2, bb=bp, bb_rwkv=bp, c=64, nsub=2)

    xs = jnp.concatenate([x_sample, jnp.zeros((bs, ts_pad - ts, d), F32)], axis=1)
    ck = cache_swa_k.reshape(depth, bs, wc, S_KV_WIDTH).transpose(0, 1, 3, 2)
    cv = cache_swa_v.reshape(depth, bs, wc, S_KV_WIDTH).transpose(0, 1, 3, 2)
    y_sample, s_states = _run_trunk(
        xs, state_rwkv, state_rwkv_shift, ck, cv, state_gdn, state_gdn_conv, layers, biases,
        prompt=False, tv=ts, alpha=alpha, tm=512, n_chunks=2, bb=8, bb_rwkv=16, c=ts_pad, nsub=1)
    return (y_prompt, y_sample[:, :ts]) + tuple(p_states) + tuple(s_states)
```

```python
import functools
import math

import numpy as np
import jax
import jax.numpy as jnp
from jax import lax
from jax.experimental import pallas as pl
from jax.experimental.pallas import tpu as pltpu

F32 = jnp.float32
BF16 = jnp.bfloat16

HEAD_DIM = 64
N_HEADS = 4
R_WIDTH = N_HEADS * HEAD_DIM
R_PROJ = 896
S_Q_HEADS = 8
S_KV_HEADS = 2
S_GROUP = S_Q_HEADS // S_KV_HEADS
S_WIDTH = S_Q_HEADS * HEAD_DIM
S_KV_WIDTH = S_KV_HEADS * HEAD_DIM
S_PROJ = S_WIDTH + 2 * S_KV_WIDTH
G_WIDTH = N_HEADS * HEAD_DIM
G_CONV = 4
G_QKV = 3 * G_WIDTH
G_PROJ_PAD = 4 * G_WIDTH + 128
WINDOW = 128
ATTN_BLOCK = 128
REL_BUCKETS = 32
REL_MAX_EXACT = 16
REL_MAX_DIST = 128
NEG_INF = -1e30
R_GN_EPS = 64e-5
G_NORM_EPS = 1e-6
LN_EPS = 1e-5
L2_EPS = 1e-6
PROJ_PAD = R_PROJ + S_PROJ + G_PROJ_PAD

LANES = 128
SUBLANES = 8
VMEM_LIMIT_BYTES = 56 * 1024 * 1024

NN = ((1,), (0,))
NT = ((1,), (1,))
TN = ((0,), (0,))


def _dg(a, b, dims):
    return lax.dot_general(a, b, (dims, ((), ())), preferred_element_type=F32)


def _split2(a):
    hi = a.astype(BF16)
    lo = (a - hi.astype(F32)).astype(BF16)
    return hi, lo


def _mm_exact_lhs(a_bf16, b, dims=NN):
    b1 = b.astype(BF16)
    r1 = b - b1.astype(F32)
    b2 = r1.astype(BF16)
    b3 = (r1 - b2.astype(F32)).astype(BF16)
    return _dg(a_bf16, b1, dims) + (_dg(a_bf16, b2, dims) + _dg(a_bf16, b3, dims))


def _sigmoid(x):
    return 1.0 / (1.0 + jnp.exp(-x))


def _silu(x):
    return x * _sigmoid(x)


def _softplus(x):
    return jnp.maximum(x, 0.0) + jnp.log(1.0 + jnp.exp(-jnp.abs(x)))


def _layer_norm(z, g, b):
    mu = jnp.mean(z, axis=-1, keepdims=True)
    zc = z - mu
    var = jnp.mean(zc * zc, axis=-1, keepdims=True)
    return zc * lax.rsqrt(var + LN_EPS) * g + b


def _mm3_each(a_list, b_list, dims=NN):
    sa = [_split2(a) for a in a_list]
    sb = [_split2(b) for b in b_list]
    return [_dg(ah, bh, dims) + (_dg(ah, bl, dims) + _dg(al, bh, dims))
            for (ah, al), (bh, bl) in zip(sa, sb)]


def _mm1_each(a_list, b_list, dims=NN):
    return [_dg(a.astype(BF16), b.astype(BF16), dims) for a, b in zip(a_list, b_list)]


def _bf16_each(a_list):
    return [a.astype(BF16) for a in a_list]


def _tri_inverse_each(m_list, c):
    row = lax.broadcasted_iota(jnp.int32, (c, c), 0)
    col = lax.broadcasted_iota(jnp.int32, (c, c), 1)
    eye = jnp.where(row == col, 1.0, 0.0).astype(F32)
    t = [eye + m for m in m_list]
    mp = _bf16_each(m_list)
    span = 2
    while span < c:
        mp = _bf16_each(_mm1_each(mp, mp))
        t = [a + b for a, b in zip(t, _mm1_each(t, mp))]
        span *= 2
    return t


def _for_each_group(bb, one_group):
    one_group(list(range(bb)))


def _full_spec(shape):
    nd = len(shape)
    return pl.BlockSpec(shape, lambda *_: (0,) * nd)


def _resident_spec(shape, index_map):
    return pl.BlockSpec(shape, index_map, pipeline_mode=pl.Buffered(1))


SUB_ROWS = 256


def _row_tiles(tm):
    n = max(tm // SUB_ROWS, 1)
    return [slice(i * (tm // n), (i + 1) * (tm // n)) for i in range(n)]


def _swiglu_each(xb_l, wi_ref, wo_ref, d_ff, n_chunks):
    cw = d_ff // n_chunks
    acc = [None] * len(xb_l)
    for j in range(n_chunks):
        gate = [_dg(xb, wi_ref[:, j * cw:(j + 1) * cw], NN) for xb in xb_l]
        up = [_dg(xb, wi_ref[:, d_ff + j * cw:d_ff + (j + 1) * cw], NN) for xb in xb_l]
        act = [(_silu(g) * u).astype(BF16) for g, u in zip(gate, up)]
        part = [_dg(a, wo_ref[j * cw:(j + 1) * cw, :], NN) for a in act]
        acc = [p if a is None else a + p for a, p in zip(acc, part)]
    return acc


def _ffn_proj_kernel(x_ref, wi_ref, wo_ref, g_ref, b_ref, win_ref,
                     x1_ref, pr_ref, ps_ref, pg_ref, *, alpha, d_ff, n_chunks, tm):
    tiles = _row_tiles(tm)
    x_l = [x_ref[r, :] for r in tiles]
    y_l = _swiglu_each([x.astype(BF16) for x in x_l], wi_ref, wo_ref, d_ff, n_chunks)
    x1_l = [_layer_norm(alpha * x + 0.5 * y, g_ref[...], b_ref[...]) for x, y in zip(x_l, y_l)]
    p_l = [_dg(x1.astype(BF16), win_ref[...], NN) for x1 in x1_l]
    for r, x1, p in zip(tiles, x1_l, p_l):
        x1_ref[r, :] = x1
        pr_ref[r, :] = p[:, :R_PROJ]
        ps_ref[r, :] = p[:, R_PROJ:R_PROJ + S_PROJ]
        pg_ref[r, :] = p[:, R_PROJ + S_PROJ:]


def _weight_spec(a, l):
    if a.ndim == 3:
        return _resident_spec((None,) + a.shape[1:], lambda i: (l, 0, 0))
    return _resident_spec(a.shape, lambda i: (0, 0))


def _ffn_proj(x, wi, wo, g, b, win, *, l, alpha, tm, n_chunks):
    m, d = x.shape
    tm = min(tm, m)
    assert m % tm == 0
    d_ff = wo.shape[-2]
    kern = functools.partial(_ffn_proj_kernel, alpha=alpha, d_ff=d_ff, n_chunks=n_chunks, tm=tm)
    row = lambda w: pl.BlockSpec((tm, w), lambda i: (i, 0))
    return pl.pallas_call(
        kern,
        grid=(m // tm,),
        in_specs=[row(d)] + [_weight_spec(a, l) for a in (wi, wo, g, b, win)],
        out_specs=[row(d), row(R_PROJ), row(S_PROJ), row(G_PROJ_PAD)],
        out_shape=[jax.ShapeDtypeStruct((m, d), F32), jax.ShapeDtypeStruct((m, R_PROJ), F32),
                   jax.ShapeDtypeStruct((m, S_PROJ), F32), jax.ShapeDtypeStruct((m, G_PROJ_PAD), F32)],
        compiler_params=pltpu.CompilerParams(dimension_semantics=("arbitrary",),
                                             vmem_limit_bytes=VMEM_LIMIT_BYTES),
        name="ffn_proj",
    )(x, wi, wo, g, b, win)


def _out_ffn_kernel(x_ref, yr_ref, ys_ref, yg_ref, wout_ref, g2_ref, b2_ref, wi_ref, wo_ref, g3_ref, b3_ref,
                    o_ref, *, alpha, d_ff, n_chunks, tm):
    tiles = _row_tiles(tm)
    x_l = [x_ref[r, :] for r in tiles]
    mix_l = [(_dg(yr_ref[r, :].astype(BF16), wout_ref[0:R_WIDTH, :], NN)
              + _dg(ys_ref[r, :].astype(BF16), wout_ref[R_WIDTH:R_WIDTH + S_WIDTH, :], NN)
              + _dg(yg_ref[r, :].astype(BF16), wout_ref[R_WIDTH + S_WIDTH:, :], NN)) for r in tiles]
    x2_l = [_layer_norm(alpha * x + mix, g2_ref[...], b2_ref[...]) for x, mix in zip(x_l, mix_l)]
    y_l = _swiglu_each([x2.astype(BF16) for x2 in x2_l], wi_ref, wo_ref, d_ff, n_chunks)
    for r, x2, y in zip(tiles, x2_l, y_l):
        o_ref[r, :] = _layer_norm(alpha * x2 + 0.5 * y, g3_ref[...], b3_ref[...])


def _out_ffn(x, yr, ys, yg, wout, g2, b2, wi, wo, g3, b3, *, l, alpha, tm, n_chunks):
    m, d = x.shape
    tm = min(tm, m)
    assert m % tm == 0
    d_ff = wo.shape[-2]
    kern = functools.partial(_out_ffn_kernel, alpha=alpha, d_ff=d_ff, n_chunks=n_chunks, tm=tm)
    row = lambda w: pl.BlockSpec((tm, w), lambda i: (i, 0))
    return pl.pallas_call(
        kern,
        grid=(m // tm,),
        in_specs=[row(d), row(R_WIDTH), row(S_WIDTH), row(G_WIDTH)]
                 + [_weight_spec(a, l) for a in (wout, g2, b2, wi, wo, g3, b3)],
        out_specs=row(d),
        out_shape=jax.ShapeDtypeStruct((m, d), F32),
        compiler_params=pltpu.CompilerParams(dimension_semantics=("arbitrary",),
                                             vmem_limit_bytes=VMEM_LIMIT_BYTES),
        name="out_ffn",
    )(x, yr, ys, yg, wout, g2, b2, wi, wo, g3, b3)


def _rwkv_kernel(f_ref, shift_ref, s0_ref, mu_ref, w0_ref, wup_ref, a0_ref, aup_ref, gup_ref,
                 kk_ref, ka_ref, rk_ref, gng_ref, gnb_ref, s_acc_ref, sh_acc_ref,
                 y_ref, sout_ref, shout_ref, s_scr, prev_scr, *, bb, c, nsub, tv, nc):
    del s_acc_ref, sh_acc_ref
    ci = pl.program_id(1)

    @pl.when(ci == 0)
    def _():
        s_scr[...] = s0_ref[...]
        prev_scr[...] = shift_ref[...]

    row = lax.broadcasted_iota(jnp.int32, (c, 1), 0)
    ri = lax.broadcasted_iota(jnp.int32, (c, 2 * c), 0)
    cj = lax.broadcasted_iota(jnp.int32, (c, 2 * c), 1)
    cj = jnp.where(cj >= c, cj - c, cj)
    strict = ri > cj
    incl = ri >= cj
    tri = (lax.broadcasted_iota(jnp.int32, (c, c), 0) >= lax.broadcasted_iota(jnp.int32, (c, c), 1))
    tri = jnp.where(tri, 1.0, 0.0).astype(BF16)
    valid = row < tv

    rows = nsub * c
    first_row = lax.broadcasted_iota(jnp.int32, (rows, 1), 0) == 0

    def one_group(seqs):
        n_seq = len(seqs)
        heads = [(n, h) for n in range(n_seq) for h in range(N_HEADS)]
        hs = lambda arr, h: arr[:, h * HEAD_DIM:(h + 1) * HEAD_DIM]
        fl = [f_ref[i] for i in seqs]
        prevs = [jnp.where(first_row, prev_scr[i], pltpu.roll(f, 1, 0)) for i, f in zip(seqs, fl)]
        for i, f in zip(seqs, fl):
            prev_scr[i] = f[rows - 1:rows, :]

        fsl = [f + (p - f) * mu_ref[...] for f, p in zip(fl, prevs)]
        w_l = _mm3_each([jnp.tanh(fs[:, 768:800]) for fs in fsl], [wup_ref[...]] * n_seq)
        lr_l = _mm3_each([fs[:, 800:832] for fs in fsl], [aup_ref[...]] * n_seq)
        gate_l = _mm3_each([_sigmoid(fs[:, 832:896]) for fs in fsl], [gup_ref[...]] * n_seq)
        lr_l = [_sigmoid(a0_ref[...] + x) for x in lr_l]
        lw_l = [-jnp.exp(-_softplus(-(w0_ref[...] + x)) - 0.5) for x in w_l]
        r_l = [fs[:, 0:R_WIDTH] for fs in fsl]
        k_l = [fs[:, R_WIDTH:2 * R_WIDTH] for fs in fsl]
        v_l = [fs[:, 2 * R_WIDTH:3 * R_WIDTH] for fs in fsl]
        kkx_l = [k * kk_ref[...] for k in k_l]
        k2_l = [k * (1.0 + (lr - 1.0) * ka_ref[...]) for k, lr in zip(k_l, lr_l)]
        if tv < c:
            lw_l = [jnp.where(valid, x, 0.0) for x in lw_l]
            kkx_l = [jnp.where(valid, x, 0.0) for x in kkx_l]
            k2_l = [jnp.where(valid, x, 0.0) for x in k2_l]
        cum_l = [jnp.concatenate([_mm_exact_lhs(tri, lw[s * c:(s + 1) * c]) for s in range(nsub)], axis=0)
                 if nsub > 1 else _mm_exact_lhs(tri, lw) for lw in lw_l]
        p_in_l = [jnp.exp(x) for x in cum_l]
        p_prev_l = [jnp.exp(x - lw) for x, lw in zip(cum_l, lw_l)]
        p_inv_l = [jnp.exp(-x) for x in cum_l]

        def intra(s):
            hs = lambda arr, h: arr[s * c:(s + 1) * c, h * HEAD_DIM:(h + 1) * HEAD_DIM]
            xs, y2s, vhs, plast, rk2 = [], [], [], [], []
            for n, h in heads:
                kx = hs(kkx_l[n], h)
                kkh = kx * lax.rsqrt(jnp.sum(kx * kx, axis=-1, keepdims=True) + L2_EPS)
                at = -kkh * hs(p_prev_l[n], h)
                bt = kkh * hs(lr_l[n], h) * hs(p_inv_l[n], h)
                kt = hs(k2_l[n], h) * hs(p_inv_l[n], h)
                qt = hs(r_l[n], h) * hs(p_in_l[n], h)
                xs.append(jnp.concatenate([at, qt], axis=0))
                y2s.append(jnp.concatenate([bt, kt], axis=0))
                vhs.append(hs(v_l[n], h))
                plast.append(hs(p_in_l[n], h)[c - 1:c])
                rk2.append(hs(r_l[n], h) * hs(k2_l[n], h))
            xs = _bf16_each(xs)
            y2s = _bf16_each(y2s)
            gram = _mm1_each(xs, y2s, NT)
            xa = [jnp.where(strict, g[:c], 0.0) for g in gram]
            xq = [jnp.where(incl, g[c:], 0.0) for g in gram]
            lakv = _mm1_each([x[:, c:] for x in xa], vhs)
            t = _tri_inverse_each([x[:, :c] for x in xa], c)
            gate = [hs(gate_l[n], h) for n, h in heads]
            return dict(xs=xs, y2s=y2s, vhs=vhs, xq=xq, lakv=lakv, t=t, plast=plast, rk2=rk2, gate=gate)

        parts = [intra(s) for s in range(nsub)]
        ss = [s_scr[seqs[n], h] for n, h in heads]
        for s, p in enumerate(parts):
            xst = _mm1_each(p["xs"], ss, NT)
            u = _mm1_each(p["t"], [a[:c] + b for a, b in zip(xst, p["lakv"])])
            uv = _bf16_each([jnp.concatenate([a, b], axis=0) for a, b in zip(u, p["vhs"])])
            o_l = [a[c:] + b for a, b in zip(xst, _mm1_each(p["xq"], uv))]
            ds = _mm1_each(uv, p["y2s"], TN)
            ss = [(st + d) * pl_ for st, d, pl_ in zip(ss, ds, p["plast"])]
            for (n, h), o, vh, rk2, gate in zip(heads, o_l, p["vhs"], p["rk2"], p["gate"]):
                sl = slice(h * HEAD_DIM, (h + 1) * HEAD_DIM)
                mean = jnp.mean(o, axis=-1, keepdims=True)
                oc = o - mean
                var = jnp.mean(oc * oc, axis=-1, keepdims=True)
                on = oc * lax.rsqrt(var + R_GN_EPS) * gng_ref[:, sl] + gnb_ref[:, sl]
                bonus = jnp.sum(rk2 * rk_ref[:, sl], axis=-1, keepdims=True) * vh
                y_ref[seqs[n], s * c:(s + 1) * c, sl] = (on + bonus) * gate
        for (n, h), st in zip(heads, ss):
            s_scr[seqs[n], h] = st

        @pl.when(ci == nc - 1)
        def _():
            last = rows - c + tv
            for i, f in zip(seqs, fl):
                shout_ref[i] = f[last - 1:last, :]

    _for_each_group(bb, one_group)

    @pl.when(ci == nc - 1)
    def _():
        sout_ref[...] = s_scr[...]


def _layer_block(shape, l):
    nd = len(shape)
    return pl.BlockSpec((None,) + tuple(shape), lambda i, j=0: (l, i) + (0,) * (nd - 1))


_ANY_SPEC = pl.BlockSpec(memory_space=pl.ANY)


def _stacked_outputs(kern, n_in, accs):
    if accs[0] is None:
        def first(*refs):
            return kern(*refs[:n_in], None, None, *refs[n_in:])
        return first, [], [], {}
    return kern, [_ANY_SPEC, _ANY_SPEC], list(accs), {n_in: 1, n_in + 1: 2}


def _rwkv_mixer(f, shift_all, s_all, l, s_acc, sh_acc, params, *, bb, c, tv, nsub=1):
    b, t, _ = f.shape
    bb = min(bb, b)
    assert b % bb == 0
    rows = nsub * c
    nc = t // rows
    assert nsub == 1 or tv == c
    kern = functools.partial(_rwkv_kernel, bb=bb, c=c, nsub=nsub, tv=tv, nc=nc)
    state_spec = _layer_block((bb, N_HEADS, HEAD_DIM, HEAD_DIM), l)
    shift_spec = _layer_block((bb, 1, R_PROJ), l)
    kern, acc_specs, acc_args, aliases = _stacked_outputs(kern, 3 + len(params), (s_acc, sh_acc))
    return pl.pallas_call(
        kern,
        grid=(b // bb, nc),
        in_specs=[pl.BlockSpec((bb, rows, R_PROJ), lambda i, j: (i, j, 0)), shift_spec, state_spec]
                 + [_full_spec(p.shape) for p in params] + acc_specs,
        out_specs=[pl.BlockSpec((bb, rows, R_WIDTH), lambda i, j: (i, j, 0)), state_spec, shift_spec],
        out_shape=[jax.ShapeDtypeStruct((b, t, R_WIDTH), F32),
                   jax.ShapeDtypeStruct(s_all.shape, F32), jax.ShapeDtypeStruct(shift_all.shape, F32)],
        input_output_aliases=aliases,
        scratch_shapes=[pltpu.VMEM((bb, N_HEADS, HEAD_DIM, HEAD_DIM), F32),
                        pltpu.VMEM((bb, 1, R_PROJ), F32)],
        compiler_params=pltpu.CompilerParams(dimension_semantics=("arbitrary", "arbitrary"),
                                             vmem_limit_bytes=VMEM_LIMIT_BYTES),
        name="rwkv7",
    )(f, shift_all, s_all, *params, *acc_args)


def _gdn_kernel(x_ref, z_ref, gb_ref, cb_ref, s0_ref, cw_ref, alog_ref, dtb_ref, ng_ref, s_acc_ref, cb_acc_ref,
                y_ref, sout_ref, cbout_ref, s_scr, tail_scr, *, bb, c, nsub, tv, nc):
    del s_acc_ref, cb_acc_ref
    ci = pl.program_id(1)

    @pl.when(ci == 0)
    def _():
        s_scr[...] = s0_ref[...]
        tail_scr[...] = jnp.zeros_like(tail_scr)
        tail_scr[:, SUBLANES - (G_CONV - 1):SUBLANES, :] = cb_ref[...]

    row = lax.broadcasted_iota(jnp.int32, (c, 1), 0)
    ri = lax.broadcasted_iota(jnp.int32, (c, c), 0)
    cj = lax.broadcasted_iota(jnp.int32, (c, c), 1)
    strict = ri > cj
    incl = ri >= cj
    tri = jnp.where(incl, 1.0, 0.0).astype(BF16)
    valid = row < tv

    rows = nsub * c

    def one_group(seqs):
        n_seq = len(seqs)
        heads = [(n, h) for n in range(n_seq) for h in range(N_HEADS)]
        chains = [(n, s, h) for s in range(nsub) for n, h in heads]
        sub = lambda a, s: a[s * c:(s + 1) * c]
        x_l = [x_ref[i] for i in seqs]
        act_l = []
        for i, x in zip(seqs, x_l):
            xe = jnp.concatenate([tail_scr[i], x], axis=0)
            conv = x * cw_ref[G_CONV - 1:G_CONV, :]
            for s in range(1, G_CONV):
                conv = conv + pltpu.roll(xe, s, 0)[SUBLANES:, :] * cw_ref[G_CONV - 1 - s:G_CONV - s, :]
            tail_scr[i] = x[rows - SUBLANES:, :]
            act_l.append(_silu(conv))
        gb_l = [gb_ref[i] for i in seqs]
        beta_l = [_sigmoid(g) for g in gb_l]
        g_l = [-jnp.exp(alog_ref[...]) * _softplus(g + dtb_ref[...]) for g in gb_l]
        if tv < c:
            beta_l = [jnp.where(valid, x, 0.0) for x in beta_l]
            g_l = [jnp.where(valid, x, 0.0) for x in g_l]
        pad = lambda g: jnp.concatenate([g, jnp.zeros((LANES - c, LANES), F32)], axis=0) if c < LANES else g
        gcum = {(n, s): _mm_exact_lhs(tri, sub(g_l[n], s)) for s in range(nsub) for n in range(n_seq)}
        gcum_t = {k: pad(g).T for k, g in gcum.items()}
        qegs, khs, vbs, kbs, egs, decs, kdec, elast = [], [], [], [], [], [], [], []
        for n, s, h in chains:
            act = sub(act_l[n], s)
            qx = act[:, h * HEAD_DIM:(h + 1) * HEAD_DIM]
            kx = act[:, G_WIDTH + h * HEAD_DIM:G_WIDTH + (h + 1) * HEAD_DIM]
            vh = act[:, 2 * G_WIDTH + h * HEAD_DIM:2 * G_WIDTH + (h + 1) * HEAD_DIM]
            qh = qx * lax.rsqrt(jnp.sum(qx * qx, axis=-1, keepdims=True) + L2_EPS) * (HEAD_DIM ** -0.5)
            kh = kx * lax.rsqrt(jnp.sum(kx * kx, axis=-1, keepdims=True) + L2_EPS)
            if tv < c:
                kh = jnp.where(valid, kh, 0.0)
            beta = sub(beta_l[n], s)[:, h:h + 1]
            gc = gcum[n, s]
            gcol = gc[:, N_HEADS + h:N_HEADS + h + 1]
            grow = gcum_t[n, s][N_HEADS + h:N_HEADS + h + 1, 0:c]
            glast = gc[c - 1:c, N_HEADS + h:N_HEADS + h + 1]
            decs.append(jnp.exp(jnp.where(incl, gcol - grow, NEG_INF)))
            eg = jnp.exp(gcol)
            qegs.append((qh, eg))
            khs.append(kh)
            kbs.append(kh * beta)
            vbs.append(vh * beta)
            egs.append(eg)
            kdec.append(kh * jnp.exp(glast - gcol))
            elast.append(jnp.exp(glast))
        sol, qk = [], []
        for s in range(nsub):
            pick = lambda lst: lst[s * len(heads):(s + 1) * len(heads)]
            gram = _mm1_each([jnp.concatenate([kb, qh], axis=0) for kb, (qh, _) in zip(pick(kbs), pick(qegs))],
                             pick(khs), NT)
            t = _tri_inverse_each([-jnp.where(strict, g[:c] * d, 0.0) for g, d in zip(gram, pick(decs))], c)
            sol += _mm3_each(t, [jnp.concatenate([vb, kb * eg], axis=1)
                                 for vb, kb, eg in zip(pick(vbs), pick(kbs), pick(egs))])
            qk += [g[c:] * d for g, d in zip(gram, pick(decs))]
        wq = [jnp.concatenate([so[:, HEAD_DIM:], qh * eg], axis=0) for so, (qh, eg) in zip(sol, qegs)]
        ss = [s_scr[seqs[n], h] for n, h in heads]
        for s in range(nsub):
            pick = lambda lst: lst[s * len(heads):(s + 1) * len(heads)]
            ws = _mm3_each(pick(wq), ss)
            u = [so[:, :HEAD_DIM] - w[:c] for so, w in zip(pick(sol), ws)]
            qku = _mm1_each(pick(qk), u)
            ds = _mm3_each(pick(kdec), u, TN)
            ss = [el * st + d for el, st, d in zip(pick(elast), ss, ds)]
            for (n, h), w, qu in zip(heads, ws, qku):
                sl = slice(h * HEAD_DIM, (h + 1) * HEAD_DIM)
                o = w[c:] + qu
                o = o * lax.rsqrt(jnp.mean(o * o, axis=-1, keepdims=True) + G_NORM_EPS) * ng_ref[...]
                y_ref[seqs[n], s * c:(s + 1) * c, sl] = o * _silu(z_ref[seqs[n], s * c:(s + 1) * c, sl])
        for (n, h), st in zip(heads, ss):
            s_scr[seqs[n], h] = st

        @pl.when(ci == nc - 1)
        def _():
            shift = (rows - (rows - c + tv - (G_CONV - 1))) % rows
            for i, x in zip(seqs, x_l):
                xs = pltpu.roll(x, shift, 0) if shift else x
                cbout_ref[i] = xs[0:G_CONV - 1, :]

    _for_each_group(bb, one_group)

    @pl.when(ci == nc - 1)
    def _():
        sout_ref[...] = s_scr[...]


def _gdn_mixer(pg, cb_all, s_all, l, s_acc, cb_acc, params, *, bb, c, tv, nsub=1):
    b, t, _ = pg.shape
    bb = min(bb, b)
    assert b % bb == 0
    rows = nsub * c
    nc = t // rows
    assert nsub == 1 or tv == c
    kern = functools.partial(_gdn_kernel, bb=bb, c=c, nsub=nsub, tv=tv, nc=nc)
    state_spec = _layer_block((bb, N_HEADS, HEAD_DIM, HEAD_DIM), l)
    cb_spec = _layer_block((bb, G_CONV - 1, G_QKV), l)
    kern, acc_specs, acc_args, aliases = _stacked_outputs(kern, 5 + len(params), (s_acc, cb_acc))
    return pl.pallas_call(
        kern,
        grid=(b // bb, nc),
        in_specs=[pl.BlockSpec((bb, rows, G_QKV), lambda i, j: (i, j, 0)),
                  pl.BlockSpec((bb, rows, G_WIDTH), lambda i, j: (i, j, G_QKV // G_WIDTH)),
                  pl.BlockSpec((bb, rows, LANES), lambda i, j: (i, j, (G_QKV + G_WIDTH) // LANES)),
                  cb_spec, state_spec] + [_full_spec(p.shape) for p in params] + acc_specs,
        out_specs=[pl.BlockSpec((bb, rows, G_WIDTH), lambda i, j: (i, j, 0)), state_spec, cb_spec],
        out_shape=[jax.ShapeDtypeStruct((b, t, G_WIDTH), F32),
                   jax.ShapeDtypeStruct(s_all.shape, F32), jax.ShapeDtypeStruct(cb_all.shape, F32)],
        input_output_aliases=aliases,
        scratch_shapes=[pltpu.VMEM((bb, N_HEADS, HEAD_DIM, HEAD_DIM), F32),
                        pltpu.VMEM((bb, SUBLANES, G_QKV), F32)],
        compiler_params=pltpu.CompilerParams(dimension_semantics=("arbitrary", "arbitrary"),
                                             vmem_limit_bytes=VMEM_LIMIT_BYTES),
        name="gdn",
    )(pg, pg, pg, cb_all, s_all, *params, *acc_args)


def _to_feature_major(a, width):
    return jnp.concatenate([a[:, i * LANES:(i + 1) * LANES].T for i in range(width // LANES)], axis=0)


def _to_batch_major(a, width):
    return jnp.concatenate([a[i * LANES:(i + 1) * LANES, :].T for i in range(width // LANES)], axis=1)


def _gdn_sample_kernel(alog_ref, dtb_ref, x_ref, cb_ref, s_ref, cwt_ref, ng_ref, s_acc_ref, cb_acc_ref,
                       y_ref, sout_ref, cbout_ref, kq_scr, *, tv, t_pad):
    del s_acc_ref, cb_acc_ref
    nb = x_ref.shape[1]
    xs = [x_ref[t] for t in range(tv)]
    full = ([_to_feature_major(cb_ref[j], G_QKV) for j in range(G_CONV - 1)]
            + [_to_feature_major(x[:, :G_QKV], G_QKV) for x in xs])
    zt = [_to_feature_major(x[:, G_QKV:G_QKV + G_WIDTH], G_WIDTH) for x in xs]
    gbt = [x[:, G_QKV + G_WIDTH:].T for x in xs]
    wcol = [jnp.broadcast_to(cwt_ref[:, j:j + 1], (G_QKV, nb)) for j in range(G_CONV)]
    zero = jnp.zeros((HEAD_DIM, nb), F32)
    for t in range(tv):
        conv = full[t] * wcol[0]
        for j in range(1, G_CONV):
            conv = conv + full[t + j] * wcol[j]
        act = _silu(conv)
        y_heads = []
        for h in range(N_HEADS):
            hs = slice(h * HEAD_DIM, (h + 1) * HEAD_DIM)
            qx = act[h * HEAD_DIM:(h + 1) * HEAD_DIM]
            kx = act[G_WIDTH + h * HEAD_DIM:G_WIDTH + (h + 1) * HEAD_DIM]
            vh = act[2 * G_WIDTH + h * HEAD_DIM:2 * G_WIDTH + (h + 1) * HEAD_DIM]
            qh = qx * lax.rsqrt(jnp.sum(qx * qx, axis=0, keepdims=True) + L2_EPS) * (HEAD_DIM ** -0.5)
            kh = kx * lax.rsqrt(jnp.sum(kx * kx, axis=0, keepdims=True) + L2_EPS)
            beta = _sigmoid(gbt[t][h:h + 1, :])
            neg_a = -jnp.exp(jnp.full((1, nb), alog_ref[h], F32))
            eg = jnp.exp(neg_a * _softplus(gbt[t][N_HEADS + h:N_HEADS + h + 1, :] + dtb_ref[h]))
            slot = t * N_HEADS + h
            kq_scr[slot, 0] = kh
            kq_scr[slot, 1] = qh
            src = s_ref if t == 0 else sout_ref

            def pass1(k, acc, h=h, slot=slot, src=src):
                return acc + src[h, k] * kq_scr[slot, 0, pl.ds(k, 1), :]

            stk = lax.fori_loop(0, HEAD_DIM, pass1, zero, unroll=8)
            u = beta * (vh - eg * stk)

            def pass2(k, acc, h=h, slot=slot, src=src, eg=eg, u=u):
                sn = eg * src[h, k] + kq_scr[slot, 0, pl.ds(k, 1), :] * u
                sout_ref[h, k] = sn
                return acc + sn * kq_scr[slot, 1, pl.ds(k, 1), :]

            o = lax.fori_loop(0, HEAD_DIM, pass2, zero, unroll=8)
            o = o * lax.rsqrt(jnp.mean(o * o, axis=0, keepdims=True) + G_NORM_EPS) * ng_ref[...]
            y_heads.append(o * _silu(zt[t][hs]))
        y_ref[t] = _to_batch_major(jnp.concatenate(y_heads, axis=0), G_WIDTH)
    for t in range(tv, t_pad):
        y_ref[t] = jnp.zeros((nb, G_WIDTH), F32)
    for j in range(G_CONV - 1):
        cbout_ref[j] = _to_batch_major(full[tv + j], G_QKV)


def _gdn_sample(x_tm, cb_tm, s_nat, l, s_acc, cb_acc, params, *, t_pad):
    tv, nb, _ = x_tm.shape
    alog, dtb, cwt, ng = params
    kern = functools.partial(_gdn_sample_kernel, tv=tv, t_pad=t_pad)
    kern, acc_specs, acc_args, aliases = _stacked_outputs(kern, 7, (s_acc, cb_acc))
    state_spec = pl.BlockSpec((None, N_HEADS, HEAD_DIM, HEAD_DIM, nb), lambda i: (l, 0, 0, 0, 0))
    cb_spec = pl.BlockSpec((None, G_CONV - 1, nb, G_QKV), lambda i: (l, 0, 0, 0))
    smem = pl.BlockSpec(memory_space=pltpu.SMEM)
    return pl.pallas_call(
        kern,
        grid=(1,),
        in_specs=[smem, smem, _full_spec(x_tm.shape), cb_spec, state_spec,
                  _full_spec(cwt.shape), _full_spec(ng.shape)] + acc_specs,
        out_specs=[_full_spec((t_pad, nb, G_WIDTH)), state_spec, cb_spec],
        out_shape=[jax.ShapeDtypeStruct((t_pad, nb, G_WIDTH), F32),
                   jax.ShapeDtypeStruct(s_nat.shape, F32), jax.ShapeDtypeStruct(cb_tm.shape, F32)],
        input_output_aliases=aliases,
        scratch_shapes=[pltpu.VMEM((tv * N_HEADS, 2, HEAD_DIM, nb), F32)],
        compiler_params=pltpu.CompilerParams(dimension_semantics=("arbitrary",),
                                             vmem_limit_bytes=VMEM_LIMIT_BYTES),
        name="gdn_sample",
    )(alog, dtb, x_tm, cb_tm, s_nat, cwt, ng, *acc_args)


def _rwkv_sample_kernel(x_ref, shift_ref, s_ref, mu_ref, w0_ref, wupt_ref, a0_ref, aupt_ref, gupt_ref,
                        kk_ref, ka_ref, rk_ref, gng_ref, gnb_ref, s_acc_ref, sh_acc_ref,
                        y_ref, sout_ref, shout_ref, vec_scr, y_scr, *, tv, t_pad):
    del s_acc_ref, sh_acc_ref
    nb = x_ref.shape[1]
    xs = [x_ref[t] for t in range(tv)]
    shout_ref[...] = xs[tv - 1]
    prev = _to_feature_major(shift_ref[...], R_PROJ)
    per_step = []
    for t in range(tv):
        f = _to_feature_major(xs[t], R_PROJ)
        fs = f + (prev - f) * mu_ref[...]
        prev = f
        r = fs[0:R_WIDTH]
        k = fs[R_WIDTH:2 * R_WIDTH]
        v = fs[2 * R_WIDTH:3 * R_WIDTH]
        w = w0_ref[...] + _mm3_each([wupt_ref[...]], [jnp.tanh(fs[768:800])])[0]
        lr = _sigmoid(a0_ref[...] + _mm3_each([aupt_ref[...]], [fs[800:832]])[0])
        gate = _mm3_each([gupt_ref[...]], [_sigmoid(fs[832:896])])[0]
        decay = jnp.exp(-jnp.exp(-_softplus(-w) - 0.5))
        kkx = k * kk_ref[...]
        k2 = k * (1.0 + (lr - 1.0) * ka_ref[...])
        bonus = []
        for h in range(N_HEADS):
            hs = slice(h * HEAD_DIM, (h + 1) * HEAD_DIM)
            kx = kkx[hs]
            kkh = kx * lax.rsqrt(jnp.sum(kx * kx, axis=0, keepdims=True) + L2_EPS)
            vec_scr[t, h, 0] = -kkh
            vec_scr[t, h, 1] = decay[hs]
            vec_scr[t, h, 2] = kkh * lr[hs]
            vec_scr[t, h, 3] = k2[hs]
            vec_scr[t, h, 4] = r[hs]
            vec_scr[t, h, 5] = v[hs]
            bonus.append(jnp.sum(r[hs] * k2[hs] * rk_ref[hs, :], axis=0, keepdims=True) * v[hs])
        per_step.append((jnp.concatenate(bonus, axis=0), gate))

    for h in range(N_HEADS):
        def row(vi, carry, h=h):
            s = s_ref[h, vi]
            for t in range(tv):
                sa = jnp.sum(s * vec_scr[t, h, 0], axis=0, keepdims=True)
                s = (s * vec_scr[t, h, 1] + sa * vec_scr[t, h, 2]
                     + vec_scr[t, h, 5, pl.ds(vi, 1), :] * vec_scr[t, h, 3])
                y_scr[t, h, pl.ds(vi, 1), :] = jnp.sum(s * vec_scr[t, h, 4], axis=0, keepdims=True)
            sout_ref[h, vi] = s
            return carry

        lax.fori_loop(0, HEAD_DIM, row, 0, unroll=2)

    for t in range(tv):
        bonus, gate = per_step[t]
        outs = []
        for h in range(N_HEADS):
            hs = slice(h * HEAD_DIM, (h + 1) * HEAD_DIM)
            o = y_scr[t, h]
            oc = o - jnp.mean(o, axis=0, keepdims=True)
            var = jnp.mean(oc * oc, axis=0, keepdims=True)
            outs.append(oc * lax.rsqrt(var + R_GN_EPS) * gng_ref[hs, :] + gnb_ref[hs, :])
        y = (jnp.concatenate(outs, axis=0) + bonus) * gate
        y_ref[t] = _to_batch_major(y, R_WIDTH)
    for t in range(tv, t_pad):
        y_ref[t] = jnp.zeros((nb, R_WIDTH), F32)


def _rwkv_sample(x_tm, shift_all, s_nat, l, s_acc, sh_acc, params, *, t_pad):
    tv, nb, _ = x_tm.shape
    kern = functools.partial(_rwkv_sample_kernel, tv=tv, t_pad=t_pad)
    kern, acc_specs, acc_args, aliases = _stacked_outputs(kern, 3 + len(params), (s_acc, sh_acc))
    state_spec = pl.BlockSpec((None, N_HEADS, HEAD_DIM, HEAD_DIM, nb), lambda i: (l, 0, 0, 0, 0))
    shift_spec = pl.BlockSpec((None, nb, R_PROJ), lambda i: (l, 0, 0))
    return pl.pallas_call(
        kern,
        grid=(1,),
        in_specs=[_full_spec(x_tm.shape), shift_spec, state_spec] + [_full_spec(p.shape) for p in params] + acc_specs,
        out_specs=[_full_spec((t_pad, nb, R_WIDTH)), state_spec, shift_spec],
        out_shape=[jax.ShapeDtypeStruct((t_pad, nb, R_WIDTH), F32),
                   jax.ShapeDtypeStruct(s_nat.shape, F32), jax.ShapeDtypeStruct(shift_all.shape, F32)],
        input_output_aliases=aliases,
        scratch_shapes=[pltpu.VMEM((tv, N_HEADS, 6, HEAD_DIM, nb), F32),
                        pltpu.VMEM((tv, N_HEADS, HEAD_DIM, nb), F32)],
        compiler_params=pltpu.CompilerParams(dimension_semantics=("arbitrary",),
                                             vmem_limit_bytes=VMEM_LIMIT_BYTES),
        name="rwkv7_sample",
    )(x_tm, shift_all, s_nat, *params, *acc_args)


def _rel_buckets(dist):
    n = np.maximum(dist, 0)
    nf = np.maximum(n, 1).astype(np.float32)
    large = REL_MAX_EXACT + (np.log(nf / REL_MAX_EXACT) / math.log(REL_MAX_DIST / REL_MAX_EXACT)
                             * (REL_BUCKETS - REL_MAX_EXACT)).astype(np.int32)
    return np.where(n < REL_MAX_EXACT, n, np.minimum(large, REL_BUCKETS - 1)).astype(np.int32)


def _bias_kernel(table_ref, bucket_ref, o_ref):
    bucket = bucket_ref[...]
    for h in range(S_Q_HEADS):
        acc = jnp.full(bucket.shape, NEG_INF, F32)
        for k in range(REL_BUCKETS):
            acc = jnp.where(bucket == k, table_ref[k, h], acc)
        o_ref[h] = acc


def _rel_bias(table, dist, visible=None):
    bucket = _rel_buckets(dist)
    if visible is not None:
        bucket = np.where(visible, bucket, -1).astype(np.int32)
    bucket = jnp.asarray(bucket)
    return pl.pallas_call(
        _bias_kernel,
        in_specs=[pl.BlockSpec(memory_space=pltpu.SMEM), _full_spec(bucket.shape)],
        out_specs=_full_spec((S_Q_HEADS,) + bucket.shape),
        out_shape=jax.ShapeDtypeStruct((S_Q_HEADS,) + bucket.shape, F32),
        grid=(1,),
        name="rel_bias",
    )(table, bucket)


Q_BLOCKS_PER_STEP = 2


def _swa_prompt_kernel(sink_ref, q_ref, kp_ref, kc_ref, vp_ref, vc_ref, bias_ref, o_ref):
    n = pl.program_id(1)
    q = q_ref[...]
    kall = jnp.concatenate([kp_ref[...], kc_ref[...]], axis=0).astype(BF16)
    vall = jnp.concatenate([vp_ref[...], vc_ref[...]], axis=0).astype(BF16)
    chains = [(s, h) for s in range(Q_BLOCKS_PER_STEP) for h in range(S_Q_HEADS)]
    band = lambda a, s, h: a[s * ATTN_BLOCK:(s + 2) * ATTN_BLOCK,
                             (h // S_GROUP) * HEAD_DIM:(h // S_GROUP + 1) * HEAD_DIM]
    from_prev = (lax.broadcasted_iota(jnp.int32, (ATTN_BLOCK, ATTN_BLOCK), 1)
                 > lax.broadcasted_iota(jnp.int32, (ATTN_BLOCK, ATTN_BLOCK), 0))
    has_prev = [jnp.where(n == 0, 0, 1) if s == 0 else 1 for s in range(Q_BLOCKS_PER_STEP)]
    qk = [_dg(q[s * ATTN_BLOCK:(s + 1) * ATTN_BLOCK, h * HEAD_DIM:(h + 1) * HEAD_DIM].astype(BF16),
              band(kall, s, h), NT) for s, h in chains]
    logits = [jnp.where(from_prev, x[:, :ATTN_BLOCK], x[:, ATTN_BLOCK:]) * (HEAD_DIM ** -0.5)
              + bias_ref[has_prev[s], h] for x, (s, h) in zip(qk, chains)]
    m = [jnp.maximum(jnp.max(x, axis=-1, keepdims=True), sink_ref[h]) for x, (s, h) in zip(logits, chains)]
    p = [jnp.exp(x - mm) for x, mm in zip(logits, m)]
    probs = [x / (jnp.sum(x, axis=-1, keepdims=True) + jnp.exp(sink_ref[h] - mm))
             for x, mm, (s, h) in zip(p, m, chains)]
    for x, (s, h) in zip(probs, chains):
        pband = jnp.concatenate([jnp.where(from_prev, x, 0.0), jnp.where(from_prev, 0.0, x)], axis=1)
        o_ref[s * ATTN_BLOCK:(s + 1) * ATTN_BLOCK, h * HEAD_DIM:(h + 1) * HEAD_DIM] = _dg(
            pband.astype(BF16), band(vall, s, h), NN)


def _swa_prompt(ps, sinks, bias):
    b, t, _ = ps.shape
    rows = Q_BLOCKS_PER_STEP * ATTN_BLOCK
    kcol = S_WIDTH // S_KV_WIDTH
    prev = lambda j: jnp.maximum(Q_BLOCKS_PER_STEP * j - 1, 0)
    return pl.pallas_call(
        _swa_prompt_kernel,
        grid=(b, t // rows),
        in_specs=[pl.BlockSpec(memory_space=pltpu.SMEM),
                  pl.BlockSpec((None, rows, S_WIDTH), lambda i, j: (i, j, 0)),
                  pl.BlockSpec((None, ATTN_BLOCK, S_KV_WIDTH), lambda i, j: (i, prev(j), kcol)),
                  pl.BlockSpec((None, rows, S_KV_WIDTH), lambda i, j: (i, j, kcol)),
                  pl.BlockSpec((None, ATTN_BLOCK, S_KV_WIDTH), lambda i, j: (i, prev(j), kcol + 1)),
                  pl.BlockSpec((None, rows, S_KV_WIDTH), lambda i, j: (i, j, kcol + 1)),
                  _full_spec(bias.shape)],
        out_specs=pl.BlockSpec((None, rows, S_WIDTH), lambda i, j: (i, j, 0)),
        out_shape=jax.ShapeDtypeStruct((b, t, S_WIDTH), F32),
        compiler_params=pltpu.CompilerParams(dimension_semantics=("arbitrary", "arbitrary"),
                                             vmem_limit_bytes=VMEM_LIMIT_BYTES),
        name="swa_prompt",
    )(sinks, ps, ps, ps, ps, ps, bias)


def _swa_sample_kernel(sink_ref, q_ref, kn_ref, vn_ref, ck_ref, cv_ref, bias_c_ref, bias_n_ref,
                       k_acc_ref, v_acc_ref, o_ref, cko_ref, cvo_ref, *, bb, tp, tv):
    del k_acc_ref, v_acc_ref
    wc = WINDOW
    step = lambda n: jnp.concatenate([lax.broadcasted_iota(jnp.int32, (tp, n), 0)] * S_GROUP, axis=0)
    ti_c = step(wc)
    kj_c = lax.broadcasted_iota(jnp.int32, (S_GROUP * tp, wc), 1)
    dist_c = wc + ti_c - kj_c
    mask_c = (dist_c >= 0) & (dist_c < WINDOW)
    ti_n = step(tp)
    kj_n = lax.broadcasted_iota(jnp.int32, (S_GROUP * tp, tp), 1)
    dist_n = ti_n - kj_n
    mask_n = (dist_n >= 0) & (dist_n < WINDOW) & (kj_n < tv)

    sinks = [jnp.concatenate([jnp.full((tp, 1), sink_ref[kv * S_GROUP + g], F32) for g in range(S_GROUP)], axis=0)
             for kv in range(S_KV_HEADS)]
    old_cols = lax.broadcasted_iota(jnp.int32, (1, wc), 1) < wc - tv
    scale = HEAD_DIM ** -0.5

    def one_group(seqs):
        chains = [(n, kv) for n in range(len(seqs)) for kv in range(S_KV_HEADS)]
        kvs = lambda a, kv: a[:, kv * HEAD_DIM:(kv + 1) * HEAD_DIM]
        kvt = lambda a, kv: a[kv * HEAD_DIM:(kv + 1) * HEAD_DIM, :]
        q_l = [q_ref[i] for i in seqs]
        kn_l = [kn_ref[i] for i in seqs]
        vn_l = [vn_ref[i] for i in seqs]
        ck_l = [ck_ref[i] for i in seqs]
        cv_l = [cv_ref[i] for i in seqs]
        qg = [jnp.concatenate([q_l[n][:, (kv * S_GROUP + g) * HEAD_DIM:(kv * S_GROUP + g + 1) * HEAD_DIM]
                               for g in range(S_GROUP)], axis=0).astype(BF16) for n, kv in chains]
        lc = [_dg(x, kvt(ck_l[n], kv).astype(BF16), NN) for x, (n, kv) in zip(qg, chains)]
        ln = [_dg(x, kvs(kn_l[n], kv).astype(BF16), NT) for x, (n, kv) in zip(qg, chains)]
        lc = [jnp.where(mask_c, x * scale + bias_c_ref[kv], NEG_INF) for x, (n, kv) in zip(lc, chains)]
        ln = [jnp.where(mask_n, x * scale + bias_n_ref[kv], NEG_INF) for x, (n, kv) in zip(ln, chains)]
        m = [jnp.maximum(jnp.maximum(jnp.max(a, axis=-1, keepdims=True), jnp.max(b, axis=-1, keepdims=True)),
                         sinks[kv]) for a, b, (n, kv) in zip(lc, ln, chains)]
        pc = [jnp.exp(a - mm) for a, mm in zip(lc, m)]
        pn = [jnp.exp(a - mm) for a, mm in zip(ln, m)]
        den = [jnp.sum(a, axis=-1, keepdims=True) + jnp.sum(b, axis=-1, keepdims=True) + jnp.exp(sinks[kv] - mm)
               for a, b, mm, (n, kv) in zip(pc, pn, m, chains)]
        oc = [_dg((a / d).astype(BF16), kvt(cv_l[n], kv).astype(BF16), NT) for a, d, (n, kv) in zip(pc, den, chains)]
        on = [_dg((a / d).astype(BF16), kvs(vn_l[n], kv).astype(BF16), NN) for a, d, (n, kv) in zip(pn, den, chains)]
        for a, b, (n, kv) in zip(oc, on, chains):
            og = a + b
            for g in range(S_GROUP):
                hq = kv * S_GROUP + g
                o_ref[seqs[n], :, hq * HEAD_DIM:(hq + 1) * HEAD_DIM] = og[g * tp:(g + 1) * tp, :]
        zpad = jnp.zeros((wc - tp, S_KV_WIDTH), F32)
        as_cols = lambda new: jnp.concatenate([new, zpad], axis=0).T
        for n, i in enumerate(seqs):
            cko_ref[i] = jnp.where(old_cols, pltpu.roll(ck_l[n], wc - tv, 1),
                                   pltpu.roll(as_cols(kn_l[n]), wc - tv, 1))
            cvo_ref[i] = jnp.where(old_cols, pltpu.roll(cv_l[n], wc - tv, 1),
                                   pltpu.roll(as_cols(vn_l[n]), wc - tv, 1))

    _for_each_group(bb, one_group)


def _swa_sample(ps, ck_all, cv_all, l, k_acc, v_acc, sinks, bias_c, bias_n, *, bb, tv):
    b, tp, _ = ps.shape
    bb = min(bb, b)
    assert b % bb == 0
    kcol = S_WIDTH // S_KV_WIDTH
    kern = functools.partial(_swa_sample_kernel, bb=bb, tp=tp, tv=tv)
    kern, acc_specs, acc_args, aliases = _stacked_outputs(kern, 8, (k_acc, v_acc))
    cache_spec = _layer_block((bb, S_KV_WIDTH, WINDOW), l)
    return pl.pallas_call(
        kern,
        grid=(b // bb,),
        in_specs=[pl.BlockSpec(memory_space=pltpu.SMEM),
                  pl.BlockSpec((bb, tp, S_WIDTH), lambda i: (i, 0, 0)),
                  pl.BlockSpec((bb, tp, S_KV_WIDTH), lambda i: (i, 0, kcol)),
                  pl.BlockSpec((bb, tp, S_KV_WIDTH), lambda i: (i, 0, kcol + 1)),
                  cache_spec, cache_spec, _full_spec(bias_c.shape), _full_spec(bias_n.shape)] + acc_specs,
        out_specs=[pl.BlockSpec((bb, tp, S_WIDTH), lambda i: (i, 0, 0)), cache_spec, cache_spec],
        out_shape=[jax.ShapeDtypeStruct((b, tp, S_WIDTH), F32),
                   jax.ShapeDtypeStruct(ck_all.shape, F32), jax.ShapeDtypeStruct(cv_all.shape, F32)],
        input_output_aliases=aliases,
        compiler_params=pltpu.CompilerParams(dimension_semantics=("arbitrary",),
                                             vmem_limit_bytes=VMEM_LIMIT_BYTES),
        name="swa_sample",
    )(sinks, ps, ps, ps, ck_all, cv_all, bias_c, bias_n, *acc_args)


def _prep_layer(l, ffn1_w_in, ffn1_w_out, ln1_g, ln1_b, w_in, rwkv_mu, rwkv_w0, rwkv_w_up, rwkv_a0,
                rwkv_a_up, rwkv_g_up, rwkv_k_k, rwkv_k_a, rwkv_r_k, rwkv_gn_g, rwkv_gn_b, swa_sinks,
                gdn_conv_w, gdn_a_log, gdn_dt_bias, gdn_norm_g, w_out, ln2_g, ln2_b,
                ffn2_w_in, ffn2_w_out, ln3_g, ln3_b):
    row = lambda a: a[l].reshape(1, -1)
    col = lambda a: a[l].reshape(-1, 1)
    lane_tile = lambda a: jnp.zeros((1, LANES), F32).at[0, N_HEADS:2 * N_HEADS].set(a[l])
    return dict(
        ffn1=(ffn1_w_in, ffn1_w_out, row(ln1_g), row(ln1_b), w_in),
        rwkv=(row(rwkv_mu), row(rwkv_w0), rwkv_w_up[l], row(rwkv_a0), rwkv_a_up[l], rwkv_g_up[l],
              row(rwkv_k_k), row(rwkv_k_a), row(rwkv_r_k), row(rwkv_gn_g), row(rwkv_gn_b)),
        rwkv_sample=(col(rwkv_mu), col(rwkv_w0), rwkv_w_up[l].T, col(rwkv_a0), rwkv_a_up[l].T, rwkv_g_up[l].T,
                     col(rwkv_k_k), col(rwkv_k_a), col(rwkv_r_k), col(rwkv_gn_g), col(rwkv_gn_b)),
        sinks=swa_sinks[l],
        gdn=(gdn_conv_w[l], lane_tile(gdn_a_log), lane_tile(gdn_dt_bias), row(gdn_norm_g)),
        gdn_sample=(gdn_a_log[l], gdn_dt_bias[l], gdn_conv_w[l].T, gdn_norm_g[l].reshape(HEAD_DIM, 1)),
        out=(w_out, row(ln2_g), row(ln2_b), ffn2_w_in, ffn2_w_out, row(ln3_g), row(ln3_b)),
    )


def _run_trunk(x, rwkv_s, rwkv_shift, swa_k, swa_v, gdn_s, gdn_conv, layers, biases, *, prompt, tv, alpha,
               tm, n_chunks, bb, bb_rwkv, c, nsub):
    b, t, d = x.shape
    depth = len(layers)
    xf = x.reshape(b * t, d)
    s_r = sh_r = s_g = cb = None
    kc, vc = ([], []) if prompt else (None, None)
    if prompt:
        rwkv_shift = rwkv_shift.reshape(depth, b, 1, R_PROJ)
    else:
        rwkv_s = rwkv_s.transpose(0, 2, 3, 4, 1)
        gdn_s = gdn_s.transpose(0, 2, 3, 4, 1)
        gdn_conv = gdn_conv.transpose(0, 2, 1, 3)
    for l, lp in enumerate(layers):
        x1, pr, ps, pg = _ffn_proj(xf, *lp["ffn1"], l=l, alpha=alpha, tm=tm, n_chunks=n_chunks)
        pr = pr.reshape(b, t, R_PROJ)
        ps = ps.reshape(b, t, S_PROJ)
        pg = pg.reshape(b, t, G_PROJ_PAD)
        if prompt:
            y_r, s_r, sh_r = _rwkv_mixer(pr, rwkv_shift, rwkv_s, l, s_r, sh_r, lp["rwkv"], bb=bb_rwkv, c=c, tv=tv,
                                         nsub=nsub)
        else:
            y_tm, s_r, sh_r = _rwkv_sample(pr[:, :tv].transpose(1, 0, 2), rwkv_shift, rwkv_s, l, s_r, sh_r,
                                           lp["rwkv_sample"], t_pad=t)
            y_r = y_tm.transpose(1, 0, 2)
        if prompt:
            y_s = _swa_prompt(ps, lp["sinks"], biases[0])
            kc.append(ps[:, t - WINDOW:, S_WIDTH:S_WIDTH + S_KV_WIDTH])
            vc.append(ps[:, t - WINDOW:, S_WIDTH + S_KV_WIDTH:])
        else:
            y_s, kc, vc = _swa_sample(ps, swa_k, swa_v, l, kc, vc, lp["sinks"], biases[1], biases[2], bb=bb, tv=tv)
        if prompt:
            y_g, s_g, cb = _gdn_mixer(pg, gdn_conv, gdn_s, l, s_g, cb, lp["gdn"], bb=bb, c=c, tv=tv, nsub=nsub)
        else:
            y_tm, s_g, cb = _gdn_sample(pg[:, :tv].transpose(1, 0, 2), gdn_conv, gdn_s, l, s_g, cb,
                                        lp["gdn_sample"], t_pad=t)
            y_g = y_tm.transpose(1, 0, 2)
        xf = _out_ffn(x1, y_r.reshape(b * t, R_WIDTH), y_s.reshape(b * t, S_WIDTH), y_g.reshape(b * t, G_WIDTH),
                      *lp["out"], l=l, alpha=alpha, tm=tm, n_chunks=n_chunks)
    if prompt:
        kc, vc = jnp.stack(kc, axis=0), jnp.stack(vc, axis=0)
    else:
        s_r = s_r.transpose(0, 4, 1, 2, 3)
        kc, vc = kc.transpose(0, 1, 3, 2), vc.transpose(0, 1, 3, 2)
        s_g = s_g.transpose(0, 4, 1, 2, 3)
        cb = cb.transpose(0, 2, 1, 3)
    cache_shape = (depth, b, WINDOW, S_KV_HEADS, HEAD_DIM)
    return xf.reshape(b, t, d), [s_r, sh_r.reshape(depth, b, R_PROJ), kc.reshape(cache_shape),
                                 vc.reshape(cache_shape), s_g, cb]


def kernel(x_prompt, x_sample, state_rwkv, state_rwkv_shift, cache_swa_k, cache_swa_v, state_gdn, state_gdn_conv, ffn1_w_in, ffn1_w_out, ln1_g, ln1_b, w_in, rwkv_mu, rwkv_w0, rwkv_w_up, rwkv_a0, rwkv_a_up, rwkv_g_up, rwkv_k_k, rwkv_k_a, rwkv_r_k, rwkv_gn_g, rwkv_gn_b, swa_sinks, rel_table, gdn_conv_w, gdn_a_log, gdn_dt_bias, gdn_norm_g, w_out, ln2_g, ln2_b, ffn2_w_in, ffn2_w_out, ln3_g, ln3_b):
    depth = ffn1_w_in.shape[0]
    alpha = (2 * depth) ** 0.25
    ffn1_w_in, ffn1_w_out, ffn2_w_in, ffn2_w_out, w_out = (
        a.astype(BF16) for a in (ffn1_w_in, ffn1_w_out, ffn2_w_in, ffn2_w_out, w_out))
    w_in = jnp.concatenate([w_in, jnp.zeros(w_in.shape[:2] + (PROJ_PAD - w_in.shape[2],), F32)],
                           axis=2).astype(BF16)
    layers = [_prep_layer(l, ffn1_w_in, ffn1_w_out, ln1_g, ln1_b, w_in, rwkv_mu, rwkv_w0, rwkv_w_up, rwkv_a0,
                          rwkv_a_up, rwkv_g_up, rwkv_k_k, rwkv_k_a, rwkv_r_k, rwkv_gn_g, rwkv_gn_b, swa_sinks,
                          gdn_conv_w, gdn_a_log, gdn_dt_bias, gdn_norm_g, w_out, ln2_g, ln2_b,
                          ffn2_w_in, ffn2_w_out, ln3_g, ln3_b) for l in range(depth)]
    bp, tp_len, d = x_prompt.shape
    bs, ts, _ = x_sample.shape
    ts_pad = -(-ts // SUBLANES) * SUBLANES

    assert WINDOW == ATTN_BLOCK
    qi = np.arange(ATTN_BLOCK)[:, None]
    kc = np.arange(ATTN_BLOCK)[None, :]
    dist_p = np.where(kc > qi, ATTN_BLOCK + qi - kc, qi - kc)
    bias_p = jnp.stack([_rel_bias(rel_table, dist_p, kc <= qi), _rel_bias(rel_table, dist_p)], axis=0)
    ti = (np.arange(S_GROUP * ts_pad) % ts_pad)[:, None]
    wc = cache_swa_k.shape[2]
    bias_c = _rel_bias(rel_table, wc + ti - np.arange(wc)[None, :])
    bias_n = _rel_bias(rel_table, ti - np.arange(ts_pad)[None, :])
    regroup = lambda a: jnp.stack([jnp.concatenate([a[kv * S_GROUP + g, g * ts_pad:(g + 1) * ts_pad]
                                                    for g in range(S_GROUP)], axis=0)
                                   for kv in range(S_KV_HEADS)], axis=0)
    biases = (bias_p, regroup(bias_c), regroup(bias_n))

    zeros = lambda *s: jnp.zeros((depth, bp) + s, F32)
    y_prompt, p_states = _run_trunk(
        x_prompt, zeros(N_HEADS, HEAD_DIM, HEAD_DIM), zeros(R_PROJ), None, None,
        zeros(N_HEADS, HEAD_DIM, HEAD_DIM), zeros(G_CONV - 1, G_QKV), layers, biases,
        prompt=True, tv=64, alpha=alpha, tm=512, n_chunks=2, bb=bp, bb_rwkv=bp, c=64, nsub=2)

    xs = jnp.concatenate([x_sample, jnp.zeros((bs, ts_pad - ts, d), F32)], axis=1)
    ck = cache_swa_k.reshape(depth, bs, wc, S_KV_WIDTH).transpose(0, 1, 3, 2)
    cv = cache_swa_v.reshape(depth, bs, wc, S_KV_WIDTH).transpose(0, 1, 3, 2)
    y_sample, s_states = _run_trunk(
        xs, state_rwkv, state_rwkv_shift, ck, cv, state_gdn, state_gdn_conv, layers, biases,
        prompt=False, tv=ts, alpha=alpha, tm=512, n_chunks=2, bb=8, bb_rwkv=16, c=ts_pad, nsub=1)
    return (y_prompt, y_sample[:, :ts]) + tuple(p_states) + tuple(s_states)
```

```python
import functools
import math

import numpy as np
import jax
import jax.numpy as jnp
from jax import lax
from jax.experimental import pallas as pl
from jax.experimental.pallas import tpu as pltpu

F32 = jnp.float32
BF16 = jnp.bfloat16

HEAD_DIM = 64
N_HEADS = 4
R_WIDTH = N_HEADS * HEAD_DIM
R_PROJ = 896
S_Q_HEADS = 8
S_KV_HEADS = 2
S_GROUP = S_Q_HEADS // S_KV_HEADS
S_WIDTH = S_Q_HEADS * HEAD_DIM
S_KV_WIDTH = S_KV_HEADS * HEAD_DIM
S_PROJ = S_WIDTH + 2 * S_KV_WIDTH
G_WIDTH = N_HEADS * HEAD_DIM
G_CONV = 4
G_QKV = 3 * G_WIDTH
G_PROJ_PAD = 4 * G_WIDTH + 128
WINDOW = 128
ATTN_BLOCK = 128
REL_BUCKETS = 32
REL_MAX_EXACT = 16
REL_MAX_DIST = 128
NEG_INF = -1e30
R_GN_EPS = 64e-5
G_NORM_EPS = 1e-6
LN_EPS = 1e-5
L2_EPS = 1e-6
PROJ_PAD = R_PROJ + S_PROJ + G_PROJ_PAD

LANES = 128
SUBLANES = 8
VMEM_LIMIT_BYTES = 56 * 1024 * 1024

NN = ((1,), (0,))
NT = ((1,), (1,))
TN = ((0,), (0,))


def _dg(a, b, dims):
    return lax.dot_general(a, b, (dims, ((), ())), preferred_element_type=F32)


def _split2(a):
    hi = a.astype(BF16)
    lo = (a - hi.astype(F32)).astype(BF16)
    return hi, lo


def _mm_exact_lhs(a_bf16, b, dims=NN):
    b1 = b.astype(BF16)
    r1 = b - b1.astype(F32)
    b2 = r1.astype(BF16)
    b3 = (r1 - b2.astype(F32)).astype(BF16)
    return _dg(a_bf16, b1, dims) + (_dg(a_bf16, b2, dims) + _dg(a_bf16, b3, dims))


def _sigmoid(x):
    return 1.0 / (1.0 + jnp.exp(-x))


def _silu(x):
    return x * _sigmoid(x)


def _softplus(x):
    return jnp.maximum(x, 0.0) + jnp.log(1.0 + jnp.exp(-jnp.abs(x)))


def _layer_norm(z, g, b):
    mu = jnp.mean(z, axis=-1, keepdims=True)
    zc = z - mu
    var = jnp.mean(zc * zc, axis=-1, keepdims=True)
    return zc * lax.rsqrt(var + LN_EPS) * g + b


def _mm3_each(a_list, b_list, dims=NN):
    sa = [_split2(a) for a in a_list]
    sb = [_split2(b) for b in b_list]
    return [_dg(ah, bh, dims) + (_dg(ah, bl, dims) + _dg(al, bh, dims))
            for (ah, al), (bh, bl) in zip(sa, sb)]


def _mm1_each(a_list, b_list, dims=NN):
    return [_dg(a.astype(BF16), b.astype(BF16), dims) for a, b in zip(a_list, b_list)]


def _bf16_each(a_list):
    return [a.astype(BF16) for a in a_list]


def _tri_inverse_each(m_list, c):
    row = lax.broadcasted_iota(jnp.int32, (c, c), 0)
    col = lax.broadcasted_iota(jnp.int32, (c, c), 1)
    eye = jnp.where(row == col, 1.0, 0.0).astype(F32)
    t = [eye + m for m in m_list]
    mp = _bf16_each(m_list)
    span = 2
    while span < c:
        mp = _bf16_each(_mm1_each(mp, mp))
        t = [a + b for a, b in zip(t, _mm1_each(t, mp))]
        span *= 2
    return t


def _for_each_group(bb, one_group):
    one_group(list(range(bb)))


def _full_spec(shape):
    nd = len(shape)
    return pl.BlockSpec(shape, lambda *_: (0,) * nd)


def _resident_spec(shape, index_map):
    return pl.BlockSpec(shape, index_map, pipeline_mode=pl.Buffered(1))


SUB_ROWS = 256


def _row_tiles(tm):
    n = max(tm // SUB_ROWS, 1)
    return [slice(i * (tm // n), (i + 1) * (tm // n)) for i in range(n)]


def _swiglu_each(xb_l, wi_ref, wo_ref, d_ff, n_chunks):
    cw = d_ff // n_chunks
    acc = [None] * len(xb_l)
    for j in range(n_chunks):
        gate = [_dg(xb, wi_ref[:, j * cw:(j + 1) * cw], NN) for xb in xb_l]
        up = [_dg(xb, wi_ref[:, d_ff + j * cw:d_ff + (j + 1) * cw], NN) for xb in xb_l]
        act = [(_silu(g) * u).astype(BF16) for g, u in zip(gate, up)]
        part = [_dg(a, wo_ref[j * cw:(j + 1) * cw, :], NN) for a in act]
        acc = [p if a is None else a + p for a, p in zip(acc, part)]
    return acc


def _ffn_proj_kernel(x_ref, wi_ref, wo_ref, g_ref, b_ref, win_ref,
                     x1_ref, pr_ref, ps_ref, pg_ref, *, alpha, d_ff, n_chunks, tm):
    tiles = _row_tiles(tm)
    x_l = [x_ref[r, :] for r in tiles]
    y_l = _swiglu_each([x.astype(BF16) for x in x_l], wi_ref, wo_ref, d_ff, n_chunks)
    x1_l = [_layer_norm(alpha * x + 0.5 * y, g_ref[...], b_ref[...]) for x, y in zip(x_l, y_l)]
    p_l = [_dg(x1.astype(BF16), win_ref[...], NN) for x1 in x1_l]
    for r, x1, p in zip(tiles, x1_l, p_l):
        x1_ref[r, :] = x1
        pr_ref[r, :] = p[:, :R_PROJ]
        ps_ref[r, :] = p[:, R_PROJ:R_PROJ + S_PROJ]
        pg_ref[r, :] = p[:, R_PROJ + S_PROJ:]


def _weight_spec(a, l):
    if a.ndim == 3:
        return _resident_spec((None,) + a.shape[1:], lambda i: (l, 0, 0))
    return _resident_spec(a.shape, lambda i: (0, 0))


def _ffn_proj(x, wi, wo, g, b, win, *, l, alpha, tm, n_chunks):
    m, d = x.shape
    tm = min(tm, m)
    assert m % tm == 0
    d_ff = wo.shape[-2]
    kern = functools.partial(_ffn_proj_kernel, alpha=alpha, d_ff=d_ff, n_chunks=n_chunks, tm=tm)
    row = lambda w: pl.BlockSpec((tm, w), lambda i: (i, 0))
    return pl.pallas_call(
        kern,
        grid=(m // tm,),
        in_specs=[row(d)] + [_weight_spec(a, l) for a in (wi, wo, g, b, win)],
        out_specs=[row(d), row(R_PROJ), row(S_PROJ), row(G_PROJ_PAD)],
        out_shape=[jax.ShapeDtypeStruct((m, d), F32), jax.ShapeDtypeStruct((m, R_PROJ), F32),
                   jax.ShapeDtypeStruct((m, S_PROJ), F32), jax.ShapeDtypeStruct((m, G_PROJ_PAD), F32)],
        compiler_params=pltpu.CompilerParams(dimension_semantics=("arbitrary",),
                                             vmem_limit_bytes=VMEM_LIMIT_BYTES),
        name="ffn_proj",
    )(x, wi, wo, g, b, win)


def _out_ffn_kernel(x_ref, yr_ref, ys_ref, yg_ref, wout_ref, g2_ref, b2_ref, wi_ref, wo_ref, g3_ref, b3_ref,
                    o_ref, *, alpha, d_ff, n_chunks, tm):
    tiles = _row_tiles(tm)
    x_l = [x_ref[r, :] for r in tiles]
    mix_l = [(_dg(yr_ref[r, :].astype(BF16), wout_ref[0:R_WIDTH, :], NN)
              + _dg(ys_ref[r, :].astype(BF16), wout_ref[R_WIDTH:R_WIDTH + S_WIDTH, :], NN)
              + _dg(yg_ref[r, :].astype(BF16), wout_ref[R_WIDTH + S_WIDTH:, :], NN)) for r in tiles]
    x2_l = [_layer_norm(alpha * x + mix, g2_ref[...], b2_ref[...]) for x, mix in zip(x_l, mix_l)]
    y_l = _swiglu_each([x2.astype(BF16) for x2 in x2_l], wi_ref, wo_ref, d_ff, n_chunks)
    for r, x2, y in zip(tiles, x2_l, y_l):
        o_ref[r, :] = _layer_norm(alpha * x2 + 0.5 * y, g3_ref[...], b3_ref[...])


def _out_ffn(x, yr, ys, yg, wout, g2, b2, wi, wo, g3, b3, *, l, alpha, tm, n_chunks):
    m, d = x.shape
    tm = min(tm, m)
    assert m % tm == 0
    d_ff = wo.shape[-2]
    kern = functools.partial(_out_ffn_kernel, alpha=alpha, d_ff=d_ff, n_chunks=n_chunks, tm=tm)
    row = lambda w: pl.BlockSpec((tm, w), lambda i: (i, 0))
    return pl.pallas_call(
        kern,
        grid=(m // tm,),
        in_specs=[row(d), row(R_WIDTH), row(S_WIDTH), row(G_WIDTH)]
                 + [_weight_spec(a, l) for a in (wout, g2, b2, wi, wo, g3, b3)],
        out_specs=row(d),
        out_shape=jax.ShapeDtypeStruct((m, d), F32),
        compiler_params=pltpu.CompilerParams(dimension_semantics=("arbitrary",),
                                             vmem_limit_bytes=VMEM_LIMIT_BYTES),
        name="out_ffn",
    )(x, yr, ys, yg, wout, g2, b2, wi, wo, g3, b3)


def _rwkv_kernel(f_ref, shift_ref, s0_ref, mu_ref, w0_ref, wup_ref, a0_ref, aup_ref, gup_ref,
                 kk_ref, ka_ref, rk_ref, gng_ref, gnb_ref, s_acc_ref, sh_acc_ref,
                 y_ref, sout_ref, shout_ref, s_scr, prev_scr, *, bb, c, nsub, tv, nc):
    del s_acc_ref, sh_acc_ref
    ci = pl.program_id(1)

    @pl.when(ci == 0)
    def _():
        s_scr[...] = s0_ref[...]
        prev_scr[...] = shift_ref[...]

    row = lax.broadcasted_iota(jnp.int32, (c, 1), 0)
    ri = lax.broadcasted_iota(jnp.int32, (c, 2 * c), 0)
    cj = lax.broadcasted_iota(jnp.int32, (c, 2 * c), 1)
    cj = jnp.where(cj >= c, cj - c, cj)
    strict = ri > cj
    incl = ri >= cj
    tri = (lax.broadcasted_iota(jnp.int32, (c, c), 0) >= lax.broadcasted_iota(jnp.int32, (c, c), 1))
    tri = jnp.where(tri, 1.0, 0.0).astype(BF16)
    valid = row < tv

    rows = nsub * c
    first_row = lax.broadcasted_iota(jnp.int32, (rows, 1), 0) == 0

    def one_group(seqs):
        n_seq = len(seqs)
        heads = [(n, h) for n in range(n_seq) for h in range(N_HEADS)]
        hs = lambda arr, h: arr[:, h * HEAD_DIM:(h + 1) * HEAD_DIM]
        fl = [f_ref[i] for i in seqs]
        prevs = [jnp.where(first_row, prev_scr[i], pltpu.roll(f, 1, 0)) for i, f in zip(seqs, fl)]
        for i, f in zip(seqs, fl):
            prev_scr[i] = f[rows - 1:rows, :]

        fsl = [f + (p - f) * mu_ref[...] for f, p in zip(fl, prevs)]
        w_l = _mm3_each([jnp.tanh(fs[:, 768:800]) for fs in fsl], [wup_ref[...]] * n_seq)
        lr_l = _mm3_each([fs[:, 800:832] for fs in fsl], [aup_ref[...]] * n_seq)
        gate_l = _mm3_each([_sigmoid(fs[:, 832:896]) for fs in fsl], [gup_ref[...]] * n_seq)
        lr_l = [_sigmoid(a0_ref[...] + x) for x in lr_l]
        lw_l = [-jnp.exp(-_softplus(-(w0_ref[...] + x)) - 0.5) for x in w_l]
        r_l = [fs[:, 0:R_WIDTH] for fs in fsl]
        k_l = [fs[:, R_WIDTH:2 * R_WIDTH] for fs in fsl]
        v_l = [fs[:, 2 * R_WIDTH:3 * R_WIDTH] for fs in fsl]
        kkx_l = [k * kk_ref[...] for k in k_l]
        k2_l = [k * (1.0 + (lr - 1.0) * ka_ref[...]) for k, lr in zip(k_l, lr_l)]
        if tv < c:
            lw_l = [jnp.where(valid, x, 0.0) for x in lw_l]
            kkx_l = [jnp.where(valid, x, 0.0) for x in kkx_l]
            k2_l = [jnp.where(valid, x, 0.0) for x in k2_l]
        cum_l = [jnp.concatenate([_mm_exact_lhs(tri, lw[s * c:(s + 1) * c]) for s in range(nsub)], axis=0)
                 if nsub > 1 else _mm_exact_lhs(tri, lw) for lw in lw_l]
        p_in_l = [jnp.exp(x) for x in cum_l]
        p_prev_l = [jnp.exp(x - lw) for x, lw in zip(cum_l, lw_l)]
        p_inv_l = [jnp.exp(-x) for x in cum_l]

        def intra(s):
            hs = lambda arr, h: arr[s * c:(s + 1) * c, h * HEAD_DIM:(h + 1) * HEAD_DIM]
            xs, y2s, vhs, plast, rk2 = [], [], [], [], []
            for n, h in heads:
                kx = hs(kkx_l[n], h)
                kkh = kx * lax.rsqrt(jnp.sum(kx * kx, axis=-1, keepdims=True) + L2_EPS)
                at = -kkh * hs(p_prev_l[n], h)
                bt = kkh * hs(lr_l[n], h) * hs(p_inv_l[n], h)
                kt = hs(k2_l[n], h) * hs(p_inv_l[n], h)
                qt = hs(r_l[n], h) * hs(p_in_l[n], h)
                xs.append(jnp.concatenate([at, qt], axis=0))
                y2s.append(jnp.concatenate([bt, kt], axis=0))
                vhs.append(hs(v_l[n], h))
                plast.append(hs(p_in_l[n], h)[c - 1:c])
                rk2.append(hs(r_l[n], h) * hs(k2_l[n], h))
            xs = _bf16_each(xs)
            y2s = _bf16_each(y2s)
            gram = _mm1_each(xs, y2s, NT)
            xa = [jnp.where(strict, g[:c], 0.0) for g in gram]
            xq = [jnp.where(incl, g[c:], 0.0) for g in gram]
            lakv = _mm1_each([x[:, c:] for x in xa], vhs)
            t = _tri_inverse_each([x[:, :c] for x in xa], c)
            gate = [hs(gate_l[n], h) for n, h in heads]
            return dict(xs=xs, y2s=y2s, vhs=vhs, xq=xq, lakv=lakv, t=t, plast=plast, rk2=rk2, gate=gate)

        parts = [intra(s) for s in range(nsub)]
        ss = [s_scr[seqs[n], h] for n, h in heads]
        for s, p in enumerate(parts):
            xst = _mm1_each(p["xs"], ss, NT)
            u = _mm1_each(p["t"], [a[:c] + b for a, b in zip(xst, p["lakv"])])
            uv = _bf16_each([jnp.concatenate([a, b], axis=0) for a, b in zip(u, p["vhs"])])
            o_l = [a[c:] + b for a, b in zip(xst, _mm1_each(p["xq"], uv))]
            ds = _mm1_each(uv, p["y2s"], TN)
            ss = [(st + d) * pl_ for st, d, pl_ in zip(ss, ds, p["plast"])]
            for (n, h), o, vh, rk2, gate in zip(heads, o_l, p["vhs"], p["rk2"], p["gate"]):
                sl = slice(h * HEAD_DIM, (h + 1) * HEAD_DIM)
                mean = jnp.mean(o, axis=-1, keepdims=True)
                oc = o - mean
                var = jnp.mean(oc * oc, axis=-1, keepdims=True)
                on = oc * lax.rsqrt(var + R_GN_EPS) * gng_ref[:, sl] + gnb_ref[:, sl]
                bonus = jnp.sum(rk2 * rk_ref[:, sl], axis=-1, keepdims=True) * vh
                y_ref[seqs[n], s * c:(s + 1) * c, sl] = (on + bonus) * gate
        for (n, h), st in zip(heads, ss):
            s_scr[seqs[n], h] = st

        @pl.when(ci == nc - 1)
        def _():
            last = rows - c + tv
            for i, f in zip(seqs, fl):
                shout_ref[i] = f[last - 1:last, :]

    _for_each_group(bb, one_group)

    @pl.when(ci == nc - 1)
    def _():
        sout_ref[...] = s_scr[...]


def _layer_block(shape, l):
    nd = len(shape)
    return pl.BlockSpec((None,) + tuple(shape), lambda i, j=0: (l, i) + (0,) * (nd - 1))


_ANY_SPEC = pl.BlockSpec(memory_space=pl.ANY)


def _stacked_outputs(kern, n_in, accs):
    if accs[0] is None:
        def first(*refs):
            return kern(*refs[:n_in], None, None, *refs[n_in:])
        return first, [], [], {}
    return kern, [_ANY_SPEC, _ANY_SPEC], list(accs), {n_in: 1, n_in + 1: 2}


def _rwkv_mixer(f, shift_all, s_all, l, s_acc, sh_acc, params, *, bb, c, tv, nsub=1):
    b, t, _ = f.shape
    bb = min(bb, b)
    assert b % bb == 0
    rows = nsub * c
    nc = t // rows
    assert nsub == 1 or tv == c
    kern = functools.partial(_rwkv_kernel, bb=bb, c=c, nsub=nsub, tv=tv, nc=nc)
    state_spec = _layer_block((bb, N_HEADS, HEAD_DIM, HEAD_DIM), l)
    shift_spec = _layer_block((bb, 1, R_PROJ), l)
    kern, acc_specs, acc_args, aliases = _stacked_outputs(kern, 3 + len(params), (s_acc, sh_acc))
    return pl.pallas_call(
        kern,
        grid=(b // bb, nc),
        in_specs=[pl.BlockSpec((bb, rows, R_PROJ), lambda i, j: (i, j, 0)), shift_spec, state_spec]
                 + [_full_spec(p.shape) for p in params] + acc_specs,
        out_specs=[pl.BlockSpec((bb, rows, R_WIDTH), lambda i, j: (i, j, 0)), state_spec, shift_spec],
        out_shape=[jax.ShapeDtypeStruct((b, t, R_WIDTH), F32),
                   jax.ShapeDtypeStruct(s_all.shape, F32), jax.ShapeDtypeStruct(shift_all.shape, F32)],
        input_output_aliases=aliases,
        scratch_shapes=[pltpu.VMEM((bb, N_HEADS, HEAD_DIM, HEAD_DIM), F32),
                        pltpu.VMEM((bb, 1, R_PROJ), F32)],
        compiler_params=pltpu.CompilerParams(dimension_semantics=("arbitrary", "arbitrary"),
                                             vmem_limit_bytes=VMEM_LIMIT_BYTES),
        name="rwkv7",
    )(f, shift_all, s_all, *params, *acc_args)


def _gdn_kernel(x_ref, z_ref, gb_ref, cb_ref, s0_ref, cw_ref, alog_ref, dtb_ref, ng_ref, s_acc_ref, cb_acc_ref,
                y_ref, sout_ref, cbout_ref, s_scr, tail_scr, *, bb, c, nsub, tv, nc):
    del s_acc_ref, cb_acc_ref
    ci = pl.program_id(1)

    @pl.when(ci == 0)
    def _():
        s_scr[...] = s0_ref[...]
        tail_scr[...] = jnp.zeros_like(tail_scr)
        tail_scr[:, SUBLANES - (G_CONV - 1):SUBLANES, :] = cb_ref[...]

    row = lax.broadcasted_iota(jnp.int32, (c, 1), 0)
    ri = lax.broadcasted_iota(jnp.int32, (c, c), 0)
    cj = lax.broadcasted_iota(jnp.int32, (c, c), 1)
    strict = ri > cj
    incl = ri >= cj
    tri = jnp.where(incl, 1.0, 0.0).astype(BF16)
    valid = row < tv

    rows = nsub * c

    def one_group(seqs):
        n_seq = len(seqs)
        heads = [(n, h) for n in range(n_seq) for h in range(N_HEADS)]
        chains = [(n, s, h) for s in range(nsub) for n, h in heads]
        sub = lambda a, s: a[s * c:(s + 1) * c]
        x_l = [x_ref[i] for i in seqs]
        act_l = []
        for i, x in zip(seqs, x_l):
            xe = jnp.concatenate([tail_scr[i], x], axis=0)
            conv = x * cw_ref[G_CONV - 1:G_CONV, :]
            for s in range(1, G_CONV):
                conv = conv + pltpu.roll(xe, s, 0)[SUBLANES:, :] * cw_ref[G_CONV - 1 - s:G_CONV - s, :]
            tail_scr[i] = x[rows - SUBLANES:, :]
            act_l.append(_silu(conv))
        gb_l = [gb_ref[i] for i in seqs]
        beta_l = [_sigmoid(g) for g in gb_l]
        g_l = [-jnp.exp(alog_ref[...]) * _softplus(g + dtb_ref[...]) for g in gb_l]
        if tv < c:
            beta_l = [jnp.where(valid, x, 0.0) for x in beta_l]
            g_l = [jnp.where(valid, x, 0.0) for x in g_l]
        pad = lambda g: jnp.concatenate([g, jnp.zeros((LANES - c, LANES), F32)], axis=0) if c < LANES else g
        gcum = {(n, s): _mm_exact_lhs(tri, sub(g_l[n], s)) for s in range(nsub) for n in range(n_seq)}
        gcum_t = {k: pad(g).T for k, g in gcum.items()}
        qegs, khs, vbs, kbs, egs, decs, kdec, elast = [], [], [], [], [], [], [], []
        for n, s, h in chains:
            act = sub(act_l[n], s)
            qx = act[:, h * HEAD_DIM:(h + 1) * HEAD_DIM]
            kx = act[:, G_WIDTH + h * HEAD_DIM:G_WIDTH + (h + 1) * HEAD_DIM]
            vh = act[:, 2 * G_WIDTH + h * HEAD_DIM:2 * G_WIDTH + (h + 1) * HEAD_DIM]
            qh = qx * lax.rsqrt(jnp.sum(qx * qx, axis=-1, keepdims=True) + L2_EPS) * (HEAD_DIM ** -0.5)
            kh = kx * lax.rsqrt(jnp.sum(kx * kx, axis=-1, keepdims=True) + L2_EPS)
            if tv < c:
                kh = jnp.where(valid, kh, 0.0)
            beta = sub(beta_l[n], s)[:, h:h + 1]
            gc = gcum[n, s]
            gcol = gc[:, N_HEADS + h:N_HEADS + h + 1]
            grow = gcum_t[n, s][N_HEADS + h:N_HEADS + h + 1, 0:c]
            glast = gc[c - 1:c, N_HEADS + h:N_HEADS + h + 1]
            decs.append(jnp.exp(jnp.where(incl, gcol - grow, NEG_INF)))
            eg = jnp.exp(gcol)
            qegs.append((qh, eg))
            khs.append(kh)
            kbs.append(kh * beta)
            vbs.append(vh * beta)
            egs.append(eg)
            kdec.append(kh * jnp.exp(glast - gcol))
            elast.append(jnp.exp(glast))
        sol, qk = [], []
        for s in range(nsub):
            pick = lambda lst: lst[s * len(heads):(s + 1) * len(heads)]
            gram = _mm1_each([jnp.concatenate([kb, qh], axis=0) for kb, (qh, _) in zip(pick(kbs), pick(qegs))],
                             pick(khs), NT)
            t = _tri_inverse_each([-jnp.where(strict, g[:c] * d, 0.0) for g, d in zip(gram, pick(decs))], c)
            sol += _mm3_each(t, [jnp.concatenate([vb, kb * eg], axis=1)
                                 for vb, kb, eg in zip(pick(vbs), pick(kbs), pick(egs))])
            qk += [g[c:] * d for g, d in zip(gram, pick(decs))]
        wq = [jnp.concatenate([so[:, HEAD_DIM:], qh * eg], axis=0) for so, (qh, eg) in zip(sol, qegs)]
        ss = [s_scr[seqs[n], h] for n, h in heads]
        for s in range(nsub):
            pick = lambda lst: lst[s * len(heads):(s + 1) * len(heads)]
            ws = _mm3_each(pick(wq), ss)
            u = [so[:, :HEAD_DIM] - w[:c] for so, w in zip(pick(sol), ws)]
            qku = _mm1_each(pick(qk), u)
            ds = _mm3_each(pick(kdec), u, TN)
            ss = [el * st + d for el, st, d in zip(pick(elast), ss, ds)]
            for (n, h), w, qu in zip(heads, ws, qku):
                sl = slice(h * HEAD_DIM, (h + 1) * HEAD_DIM)
                o = w[c:] + qu
                o = o * lax.rsqrt(jnp.mean(o * o, axis=-1, keepdims=True) + G_NORM_EPS) * ng_ref[...]
                y_ref[seqs[n], s * c:(s + 1) * c, sl] = o * _silu(z_ref[seqs[n], s * c:(s + 1) * c, sl])
        for (n, h), st in zip(heads, ss):
            s_scr[seqs[n], h] = st

        @pl.when(ci == nc - 1)
        def _():
            shift = (rows - (rows - c + tv - (G_CONV - 1))) % rows
            for i, x in zip(seqs, x_l):
                xs = pltpu.roll(x, shift, 0) if shift else x
                cbout_ref[i] = xs[0:G_CONV - 1, :]

    _for_each_group(bb, one_group)

    @pl.when(ci == nc - 1)
    def _():
        sout_ref[...] = s_scr[...]


def _gdn_mixer(pg, cb_all, s_all, l, s_acc, cb_acc, params, *, bb, c, tv, nsub=1):
    b, t, _ = pg.shape
    bb = min(bb, b)
    assert b % bb == 0
    rows = nsub * c
    nc = t // rows
    assert nsub == 1 or tv == c
    kern = functools.partial(_gdn_kernel, bb=bb, c=c, nsub=nsub, tv=tv, nc=nc)
    state_spec = _layer_block((bb, N_HEADS, HEAD_DIM, HEAD_DIM), l)
    cb_spec = _layer_block((bb, G_CONV - 1, G_QKV), l)
    kern, acc_specs, acc_args, aliases = _stacked_outputs(kern, 5 + len(params), (s_acc, cb_acc))
    return pl.pallas_call(
        kern,
        grid=(b // bb, nc),
        in_specs=[pl.BlockSpec((bb, rows, G_QKV), lambda i, j: (i, j, 0)),
                  pl.BlockSpec((bb, rows, G_WIDTH), lambda i, j: (i, j, G_QKV // G_WIDTH)),
                  pl.BlockSpec((bb, rows, LANES), lambda i, j: (i, j, (G_QKV + G_WIDTH) // LANES)),
                  cb_spec, state_spec] + [_full_spec(p.shape) for p in params] + acc_specs,
        out_specs=[pl.BlockSpec((bb, rows, G_WIDTH), lambda i, j: (i, j, 0)), state_spec, cb_spec],
        out_shape=[jax.ShapeDtypeStruct((b, t, G_WIDTH), F32),
                   jax.ShapeDtypeStruct(s_all.shape, F32), jax.ShapeDtypeStruct(cb_all.shape, F32)],
        input_output_aliases=aliases,
        scratch_shapes=[pltpu.VMEM((bb, N_HEADS, HEAD_DIM, HEAD_DIM), F32),
                        pltpu.VMEM((bb, SUBLANES, G_QKV), F32)],
        compiler_params=pltpu.CompilerParams(dimension_semantics=("arbitrary", "arbitrary"),
                                             vmem_limit_bytes=VMEM_LIMIT_BYTES),
        name="gdn",
    )(pg, pg, pg, cb_all, s_all, *params, *acc_args)


def _to_feature_major(a, width):
    return jnp.concatenate([a[:, i * LANES:(i + 1) * LANES].T for i in range(width // LANES)], axis=0)


def _to_batch_major(a, width):
    return jnp.concatenate([a[i * LANES:(i + 1) * LANES, :].T for i in range(width // LANES)], axis=1)


def _gdn_sample_kernel(alog_ref, dtb_ref, x_ref, cb_ref, s_ref, cwt_ref, ng_ref, s_acc_ref, cb_acc_ref,
                       y_ref, sout_ref, cbout_ref, kq_scr, *, tv, t_pad):
    del s_acc_ref, cb_acc_ref
    nb = x_ref.shape[1]
    xs = [x_ref[t] for t in range(tv)]
    full = ([_to_feature_major(cb_ref[j], G_QKV) for j in range(G_CONV - 1)]
            + [_to_feature_major(x[:, :G_QKV], G_QKV) for x in xs])
    zt = [_to_feature_major(x[:, G_QKV:G_QKV + G_WIDTH], G_WIDTH) for x in xs]
    gbt = [x[:, G_QKV + G_WIDTH:].T for x in xs]
    wcol = [jnp.broadcast_to(cwt_ref[:, j:j + 1], (G_QKV, nb)) for j in range(G_CONV)]
    zero = jnp.zeros((HEAD_DIM, nb), F32)
    for t in range(tv):
        conv = full[t] * wcol[0]
        for j in range(1, G_CONV):
            conv = conv + full[t + j] * wcol[j]
        act = _silu(conv)
        y_heads = []
        for h in range(N_HEADS):
            hs = slice(h * HEAD_DIM, (h + 1) * HEAD_DIM)
            qx = act[h * HEAD_DIM:(h + 1) * HEAD_DIM]
            kx = act[G_WIDTH + h * HEAD_DIM:G_WIDTH + (h + 1) * HEAD_DIM]
            vh = act[2 * G_WIDTH + h * HEAD_DIM:2 * G_WIDTH + (h + 1) * HEAD_DIM]
            qh = qx * lax.rsqrt(jnp.sum(qx * qx, axis=0, keepdims=True) + L2_EPS) * (HEAD_DIM ** -0.5)
            kh = kx * lax.rsqrt(jnp.sum(kx * kx, axis=0, keepdims=True) + L2_EPS)
            beta = _sigmoid(gbt[t][h:h + 1, :])
            neg_a = -jnp.exp(jnp.full((1, nb), alog_ref[h], F32))
            eg = jnp.exp(neg_a * _softplus(gbt[t][N_HEADS + h:N_HEADS + h + 1, :] + dtb_ref[h]))
            slot = t * N_HEADS + h
            kq_scr[slot, 0] = kh
            kq_scr[slot, 1] = qh
            src = s_ref if t == 0 else sout_ref

            def pass1(k, acc, h=h, slot=slot, src=src):
                return acc + src[h, k] * kq_scr[slot, 0, pl.ds(k, 1), :]

            stk = lax.fori_loop(0, HEAD_DIM, pass1, zero, unroll=8)
            u = beta * (vh - eg * stk)

            def pass2(k, acc, h=h, slot=slot, src=src, eg=eg, u=u):
                sn = eg * src[h, k] + kq_scr[slot, 0, pl.ds(k, 1), :] * u
                sout_ref[h, k] = sn
                return acc + sn * kq_scr[slot, 1, pl.ds(k, 1), :]

            o = lax.fori_loop(0, HEAD_DIM, pass2, zero, unroll=8)
            o = o * lax.rsqrt(jnp.mean(o * o, axis=0, keepdims=True) + G_NORM_EPS) * ng_ref[...]
            y_heads.append(o * _silu(zt[t][hs]))
        y_ref[t] = _to_batch_major(jnp.concatenate(y_heads, axis=0), G_WIDTH)
    for t in range(tv, t_pad):
        y_ref[t] = jnp.zeros((nb, G_WIDTH), F32)
    for j in range(G_CONV - 1):
        cbout_ref[j] = _to_batch_major(full[tv + j], G_QKV)


def _gdn_sample(x_tm, cb_tm, s_nat, l, s_acc, cb_acc, params, *, t_pad):
    tv, nb, _ = x_tm.shape
    alog, dtb, cwt, ng = params
    kern = functools.partial(_gdn_sample_kernel, tv=tv, t_pad=t_pad)
    kern, acc_specs, acc_args, aliases = _stacked_outputs(kern, 7, (s_acc, cb_acc))
    state_spec = pl.BlockSpec((None, N_HEADS, HEAD_DIM, HEAD_DIM, nb), lambda i: (l, 0, 0, 0, 0))
    cb_spec = pl.BlockSpec((None, G_CONV - 1, nb, G_QKV), lambda i: (l, 0, 0, 0))
    smem = pl.BlockSpec(memory_space=pltpu.SMEM)
    return pl.pallas_call(
        kern,
        grid=(1,),
        in_specs=[smem, smem, _full_spec(x_tm.shape), cb_spec, state_spec,
                  _full_spec(cwt.shape), _full_spec(ng.shape)] + acc_specs,
        out_specs=[_full_spec((t_pad, nb, G_WIDTH)), state_spec, cb_spec],
        out_shape=[jax.ShapeDtypeStruct((t_pad, nb, G_WIDTH), F32),
                   jax.ShapeDtypeStruct(s_nat.shape, F32), jax.ShapeDtypeStruct(cb_tm.shape, F32)],
        input_output_aliases=aliases,
        scratch_shapes=[pltpu.VMEM((tv * N_HEADS, 2, HEAD_DIM, nb), F32)],
        compiler_params=pltpu.CompilerParams(dimension_semantics=("arbitrary",),
                                             vmem_limit_bytes=VMEM_LIMIT_BYTES),
        name="gdn_sample",
    )(alog, dtb, x_tm, cb_tm, s_nat, cwt, ng, *acc_args)


def _rwkv_sample_kernel(x_ref, shift_ref, s_ref, mu_ref, w0_ref, wupt_ref, a0_ref, aupt_ref, gupt_ref,
                        kk_ref, ka_ref, rk_ref, gng_ref, gnb_ref, s_acc_ref, sh_acc_ref,
                        y_ref, sout_ref, shout_ref, vec_scr, y_scr, *, tv, t_pad):
    del s_acc_ref, sh_acc_ref
    nb = x_ref.shape[1]
    xs = [x_ref[t] for t in range(tv)]
    shout_ref[...] = xs[tv - 1]
    prev = _to_feature_major(shift_ref[...], R_PROJ)
    per_step = []
    for t in range(tv):
        f = _to_feature_major(xs[t], R_PROJ)
        fs = f + (prev - f) * mu_ref[...]
        prev = f
        r = fs[0:R_WIDTH]
        k = fs[R_WIDTH:2 * R_WIDTH]
        v = fs[2 * R_WIDTH:3 * R_WIDTH]
        w = w0_ref[...] + _mm3_each([wupt_ref[...]], [jnp.tanh(fs[768:800])])[0]
        lr = _sigmoid(a0_ref[...] + _mm3_each([aupt_ref[...]], [fs[800:832]])[0])
        gate = _mm3_each([gupt_ref[...]], [_sigmoid(fs[832:896])])[0]
        decay = jnp.exp(-jnp.exp(-_softplus(-w) - 0.5))
        kkx = k * kk_ref[...]
        k2 = k * (1.0 + (lr - 1.0) * ka_ref[...])
        bonus = []
        for h in range(N_HEADS):
            hs = slice(h * HEAD_DIM, (h + 1) * HEAD_DIM)
            kx = kkx[hs]
            kkh = kx * lax.rsqrt(jnp.sum(kx * kx, axis=0, keepdims=True) + L2_EPS)
            vec_scr[t, h, 0] = -kkh
            vec_scr[t, h, 1] = decay[hs]
            vec_scr[t, h, 2] = kkh * lr[hs]
            vec_scr[t, h, 3] = k2[hs]
            vec_scr[t, h, 4] = r[hs]
            vec_scr[t, h, 5] = v[hs]
            bonus.append(jnp.sum(r[hs] * k2[hs] * rk_ref[hs, :], axis=0, keepdims=True) * v[hs])
        per_step.append((jnp.concatenate(bonus, axis=0), gate))

    for h in range(N_HEADS):
        def row(vi, carry, h=h):
            s = s_ref[h, vi]
            for t in range(tv):
                sa = jnp.sum(s * vec_scr[t, h, 0], axis=0, keepdims=True)
                s = (s * vec_scr[t, h, 1] + sa * vec_scr[t, h, 2]
                     + vec_scr[t, h, 5, pl.ds(vi, 1), :] * vec_scr[t, h, 3])
                y_scr[t, h, pl.ds(vi, 1), :] = jnp.sum(s * vec_scr[t, h, 4], axis=0, keepdims=True)
            sout_ref[h, vi] = s
            return carry

        lax.fori_loop(0, HEAD_DIM, row, 0, unroll=4)

    for t in range(tv):
        bonus, gate = per_step[t]
        outs = []
        for h in range(N_HEADS):
            hs = slice(h * HEAD_DIM, (h + 1) * HEAD_DIM)
            o = y_scr[t, h]
            oc = o - jnp.mean(o, axis=0, keepdims=True)
            var = jnp.mean(oc * oc, axis=0, keepdims=True)
            outs.append(oc * lax.rsqrt(var + R_GN_EPS) * gng_ref[hs, :] + gnb_ref[hs, :])
        y = (jnp.concatenate(outs, axis=0) + bonus) * gate
        y_ref[t] = _to_batch_major(y, R_WIDTH)
    for t in range(tv, t_pad):
        y_ref[t] = jnp.zeros((nb, R_WIDTH), F32)


def _rwkv_sample(x_tm, shift_all, s_nat, l, s_acc, sh_acc, params, *, t_pad):
    tv, nb, _ = x_tm.shape
    kern = functools.partial(_rwkv_sample_kernel, tv=tv, t_pad=t_pad)
    kern, acc_specs, acc_args, aliases = _stacked_outputs(kern, 3 + len(params), (s_acc, sh_acc))
    state_spec = pl.BlockSpec((None, N_HEADS, HEAD_DIM, HEAD_DIM, nb), lambda i: (l, 0, 0, 0, 0))
    shift_spec = pl.BlockSpec((None, nb, R_PROJ), lambda i: (l, 0, 0))
    return pl.pallas_call(
        kern,
        grid=(1,),
        in_specs=[_full_spec(x_tm.shape), shift_spec, state_spec] + [_full_spec(p.shape) for p in params] + acc_specs,
        out_specs=[_full_spec((t_pad, nb, R_WIDTH)), state_spec, shift_spec],
        out_shape=[jax.ShapeDtypeStruct((t_pad, nb, R_WIDTH), F32),
                   jax.ShapeDtypeStruct(s_nat.shape, F32), jax.ShapeDtypeStruct(shift_all.shape, F32)],
        input_output_aliases=aliases,
        scratch_shapes=[pltpu.VMEM((tv, N_HEADS, 6, HEAD_DIM, nb), F32),
                        pltpu.VMEM((tv, N_HEADS, HEAD_DIM, nb), F32)],
        compiler_params=pltpu.CompilerParams(dimension_semantics=("arbitrary",),
                                             vmem_limit_bytes=VMEM_LIMIT_BYTES),
        name="rwkv7_sample",
    )(x_tm, shift_all, s_nat, *params, *acc_args)


def _rel_buckets(dist):
    n = np.maximum(dist, 0)
    nf = np.maximum(n, 1).astype(np.float32)
    large = REL_MAX_EXACT + (np.log(nf / REL_MAX_EXACT) / math.log(REL_MAX_DIST / REL_MAX_EXACT)
                             * (REL_BUCKETS - REL_MAX_EXACT)).astype(np.int32)
    return np.where(n < REL_MAX_EXACT, n, np.minimum(large, REL_BUCKETS - 1)).astype(np.int32)


def _bias_kernel(table_ref, bucket_ref, o_ref):
    bucket = bucket_ref[...]
    for h in range(S_Q_HEADS):
        acc = jnp.full(bucket.shape, NEG_INF, F32)
        for k in range(REL_BUCKETS):
            acc = jnp.where(bucket == k, table_ref[k, h], acc)
        o_ref[h] = acc


def _rel_bias(table, dist, visible=None):
    bucket = _rel_buckets(dist)
    if visible is not None:
        bucket = np.where(visible, bucket, -1).astype(np.int32)
    bucket = jnp.asarray(bucket)
    return pl.pallas_call(
        _bias_kernel,
        in_specs=[pl.BlockSpec(memory_space=pltpu.SMEM), _full_spec(bucket.shape)],
        out_specs=_full_spec((S_Q_HEADS,) + bucket.shape),
        out_shape=jax.ShapeDtypeStruct((S_Q_HEADS,) + bucket.shape, F32),
        grid=(1,),
        name="rel_bias",
    )(table, bucket)


Q_BLOCKS_PER_STEP = 2


def _swa_prompt_kernel(sink_ref, q_ref, kp_ref, kc_ref, vp_ref, vc_ref, bias_ref, o_ref):
    n = pl.program_id(1)
    q = q_ref[...]
    kall = jnp.concatenate([kp_ref[...], kc_ref[...]], axis=0).astype(BF16)
    vall = jnp.concatenate([vp_ref[...], vc_ref[...]], axis=0).astype(BF16)
    chains = [(s, h) for s in range(Q_BLOCKS_PER_STEP) for h in range(S_Q_HEADS)]
    band = lambda a, s, h: a[s * ATTN_BLOCK:(s + 2) * ATTN_BLOCK,
                             (h // S_GROUP) * HEAD_DIM:(h // S_GROUP + 1) * HEAD_DIM]
    from_prev = (lax.broadcasted_iota(jnp.int32, (ATTN_BLOCK, ATTN_BLOCK), 1)
                 > lax.broadcasted_iota(jnp.int32, (ATTN_BLOCK, ATTN_BLOCK), 0))
    has_prev = [jnp.where(n == 0, 0, 1) if s == 0 else 1 for s in range(Q_BLOCKS_PER_STEP)]
    qk = [_dg(q[s * ATTN_BLOCK:(s + 1) * ATTN_BLOCK, h * HEAD_DIM:(h + 1) * HEAD_DIM].astype(BF16),
              band(kall, s, h), NT) for s, h in chains]
    logits = [jnp.where(from_prev, x[:, :ATTN_BLOCK], x[:, ATTN_BLOCK:]) * (HEAD_DIM ** -0.5)
              + bias_ref[has_prev[s], h] for x, (s, h) in zip(qk, chains)]
    m = [jnp.maximum(jnp.max(x, axis=-1, keepdims=True), sink_ref[h]) for x, (s, h) in zip(logits, chains)]
    p = [jnp.exp(x - mm) for x, mm in zip(logits, m)]
    probs = [x / (jnp.sum(x, axis=-1, keepdims=True) + jnp.exp(sink_ref[h] - mm))
             for x, mm, (s, h) in zip(p, m, chains)]
    for x, (s, h) in zip(probs, chains):
        pband = jnp.concatenate([jnp.where(from_prev, x, 0.0), jnp.where(from_prev, 0.0, x)], axis=1)
        o_ref[s * ATTN_BLOCK:(s + 1) * ATTN_BLOCK, h * HEAD_DIM:(h + 1) * HEAD_DIM] = _dg(
            pband.astype(BF16), band(vall, s, h), NN)


def _swa_prompt(ps, sinks, bias):
    b, t, _ = ps.shape
    rows = Q_BLOCKS_PER_STEP * ATTN_BLOCK
    kcol = S_WIDTH // S_KV_WIDTH
    prev = lambda j: jnp.maximum(Q_BLOCKS_PER_STEP * j - 1, 0)
    return pl.pallas_call(
        _swa_prompt_kernel,
        grid=(b, t // rows),
        in_specs=[pl.BlockSpec(memory_space=pltpu.SMEM),
                  pl.BlockSpec((None, rows, S_WIDTH), lambda i, j: (i, j, 0)),
                  pl.BlockSpec((None, ATTN_BLOCK, S_KV_WIDTH), lambda i, j: (i, prev(j), kcol)),
                  pl.BlockSpec((None, rows, S_KV_WIDTH), lambda i, j: (i, j, kcol)),
                  pl.BlockSpec((None, ATTN_BLOCK, S_KV_WIDTH), lambda i, j: (i, prev(j), kcol + 1)),
                  pl.BlockSpec((None, rows, S_KV_WIDTH), lambda i, j: (i, j, kcol + 1)),
                  _full_spec(bias.shape)],
        out_specs=pl.BlockSpec((None, rows, S_WIDTH), lambda i, j: (i, j, 0)),
        out_shape=jax.ShapeDtypeStruct((b, t, S_WIDTH), F32),
        compiler_params=pltpu.CompilerParams(dimension_semantics=("arbitrary", "arbitrary"),
                                             vmem_limit_bytes=VMEM_LIMIT_BYTES),
        name="swa_prompt",
    )(sinks, ps, ps, ps, ps, ps, bias)


def _swa_sample_kernel(sink_ref, q_ref, kn_ref, vn_ref, ck_ref, cv_ref, bias_c_ref, bias_n_ref,
                       k_acc_ref, v_acc_ref, o_ref, cko_ref, cvo_ref, *, bb, tp, tv):
    del k_acc_ref, v_acc_ref
    wc = WINDOW
    step = lambda n: jnp.concatenate([lax.broadcasted_iota(jnp.int32, (tp, n), 0)] * S_GROUP, axis=0)
    ti_c = step(wc)
    kj_c = lax.broadcasted_iota(jnp.int32, (S_GROUP * tp, wc), 1)
    dist_c = wc + ti_c - kj_c
    mask_c = (dist_c >= 0) & (dist_c < WINDOW)
    ti_n = step(tp)
    kj_n = lax.broadcasted_iota(jnp.int32, (S_GROUP * tp, tp), 1)
    dist_n = ti_n - kj_n
    mask_n = (dist_n >= 0) & (dist_n < WINDOW) & (kj_n < tv)

    sinks = [jnp.concatenate([jnp.full((tp, 1), sink_ref[kv * S_GROUP + g], F32) for g in range(S_GROUP)], axis=0)
             for kv in range(S_KV_HEADS)]
    old_cols = lax.broadcasted_iota(jnp.int32, (1, wc), 1) < wc - tv
    scale = HEAD_DIM ** -0.5

    def one_group(seqs):
        chains = [(n, kv) for n in range(len(seqs)) for kv in range(S_KV_HEADS)]
        kvs = lambda a, kv: a[:, kv * HEAD_DIM:(kv + 1) * HEAD_DIM]
        kvt = lambda a, kv: a[kv * HEAD_DIM:(kv + 1) * HEAD_DIM, :]
        q_l = [q_ref[i] for i in seqs]
        kn_l = [kn_ref[i] for i in seqs]
        vn_l = [vn_ref[i] for i in seqs]
        ck_l = [ck_ref[i] for i in seqs]
        cv_l = [cv_ref[i] for i in seqs]
        qg = [jnp.concatenate([q_l[n][:, (kv * S_GROUP + g) * HEAD_DIM:(kv * S_GROUP + g + 1) * HEAD_DIM]
                               for g in range(S_GROUP)], axis=0).astype(BF16) for n, kv in chains]
        lc = [_dg(x, kvt(ck_l[n], kv).astype(BF16), NN) for x, (n, kv) in zip(qg, chains)]
        ln = [_dg(x, kvs(kn_l[n], kv).astype(BF16), NT) for x, (n, kv) in zip(qg, chains)]
        lc = [jnp.where(mask_c, x * scale + bias_c_ref[kv], NEG_INF) for x, (n, kv) in zip(lc, chains)]
        ln = [jnp.where(mask_n, x * scale + bias_n_ref[kv], NEG_INF) for x, (n, kv) in zip(ln, chains)]
        m = [jnp.maximum(jnp.maximum(jnp.max(a, axis=-1, keepdims=True), jnp.max(b, axis=-1, keepdims=True)),
                         sinks[kv]) for a, b, (n, kv) in zip(lc, ln, chains)]
        pc = [jnp.exp(a - mm) for a, mm in zip(lc, m)]
        pn = [jnp.exp(a - mm) for a, mm in zip(ln, m)]
        den = [jnp.sum(a, axis=-1, keepdims=True) + jnp.sum(b, axis=-1, keepdims=True) + jnp.exp(sinks[kv] - mm)
               for a, b, mm, (n, kv) in zip(pc, pn, m, chains)]
        oc = [_dg((a / d).astype(BF16), kvt(cv_l[n], kv).astype(BF16), NT) for a, d, (n, kv) in zip(pc, den, chains)]
        on = [_dg((a / d).astype(BF16), kvs(vn_l[n], kv).astype(BF16), NN) for a, d, (n, kv) in zip(pn, den, chains)]
        for a, b, (n, kv) in zip(oc, on, chains):
            og = a + b
            for g in range(S_GROUP):
                hq = kv * S_GROUP + g
                o_ref[seqs[n], :, hq * HEAD_DIM:(hq + 1) * HEAD_DIM] = og[g * tp:(g + 1) * tp, :]
        zpad = jnp.zeros((wc - tp, S_KV_WIDTH), F32)
        as_cols = lambda new: jnp.concatenate([new, zpad], axis=0).T
        for n, i in enumerate(seqs):
            cko_ref[i] = jnp.where(old_cols, pltpu.roll(ck_l[n], wc - tv, 1),
                                   pltpu.roll(as_cols(kn_l[n]), wc - tv, 1))
            cvo_ref[i] = jnp.where(old_cols, pltpu.roll(cv_l[n], wc - tv, 1),
                                   pltpu.roll(as_cols(vn_l[n]), wc - tv, 1))

    _for_each_group(bb, one_group)


def _swa_sample(ps, ck_all, cv_all, l, k_acc, v_acc, sinks, bias_c, bias_n, *, bb, tv):
    b, tp, _ = ps.shape
    bb = min(bb, b)
    assert b % bb == 0
    kcol = S_WIDTH // S_KV_WIDTH
    kern = functools.partial(_swa_sample_kernel, bb=bb, tp=tp, tv=tv)
    kern, acc_specs, acc_args, aliases = _stacked_outputs(kern, 8, (k_acc, v_acc))
    cache_spec = _layer_block((bb, S_KV_WIDTH, WINDOW), l)
    return pl.pallas_call(
        kern,
        grid=(b // bb,),
        in_specs=[pl.BlockSpec(memory_space=pltpu.SMEM),
                  pl.BlockSpec((bb, tp, S_WIDTH), lambda i: (i, 0, 0)),
                  pl.BlockSpec((bb, tp, S_KV_WIDTH), lambda i: (i, 0, kcol)),
                  pl.BlockSpec((bb, tp, S_KV_WIDTH), lambda i: (i, 0, kcol + 1)),
                  cache_spec, cache_spec, _full_spec(bias_c.shape), _full_spec(bias_n.shape)] + acc_specs,
        out_specs=[pl.BlockSpec((bb, tp, S_WIDTH), lambda i: (i, 0, 0)), cache_spec, cache_spec],
        out_shape=[jax.ShapeDtypeStruct((b, tp, S_WIDTH), F32),
                   jax.ShapeDtypeStruct(ck_all.shape, F32), jax.ShapeDtypeStruct(cv_all.shape, F32)],
        input_output_aliases=aliases,
        compiler_params=pltpu.CompilerParams(dimension_semantics=("arbitrary",),
                                             vmem_limit_bytes=VMEM_LIMIT_BYTES),
        name="swa_sample",
    )(sinks, ps, ps, ps, ck_all, cv_all, bias_c, bias_n, *acc_args)


def _prep_layer(l, ffn1_w_in, ffn1_w_out, ln1_g, ln1_b, w_in, rwkv_mu, rwkv_w0, rwkv_w_up, rwkv_a0,
                rwkv_a_up, rwkv_g_up, rwkv_k_k, rwkv_k_a, rwkv_r_k, rwkv_gn_g, rwkv_gn_b, swa_sinks,
                gdn_conv_w, gdn_a_log, gdn_dt_bias, gdn_norm_g, w_out, ln2_g, ln2_b,
                ffn2_w_in, ffn2_w_out, ln3_g, ln3_b):
    row = lambda a: a[l].reshape(1, -1)
    col = lambda a: a[l].reshape(-1, 1)
    lane_tile = lambda a: jnp.zeros((1, LANES), F32).at[0, N_HEADS:2 * N_HEADS].set(a[l])
    return dict(
        ffn1=(ffn1_w_in, ffn1_w_out, row(ln1_g), row(ln1_b), w_in),
        rwkv=(row(rwkv_mu), row(rwkv_w0), rwkv_w_up[l], row(rwkv_a0), rwkv_a_up[l], rwkv_g_up[l],
              row(rwkv_k_k), row(rwkv_k_a), row(rwkv_r_k), row(rwkv_gn_g), row(rwkv_gn_b)),
        rwkv_sample=(col(rwkv_mu), col(rwkv_w0), rwkv_w_up[l].T, col(rwkv_a0), rwkv_a_up[l].T, rwkv_g_up[l].T,
                     col(rwkv_k_k), col(rwkv_k_a), col(rwkv_r_k), col(rwkv_gn_g), col(rwkv_gn_b)),
        sinks=swa_sinks[l],
        gdn=(gdn_conv_w[l], lane_tile(gdn_a_log), lane_tile(gdn_dt_bias), row(gdn_norm_g)),
        gdn_sample=(gdn_a_log[l], gdn_dt_bias[l], gdn_conv_w[l].T, gdn_norm_g[l].reshape(HEAD_DIM, 1)),
        out=(w_out, row(ln2_g), row(ln2_b), ffn2_w_in, ffn2_w_out, row(ln3_g), row(ln3_b)),
    )


def _run_trunk(x, rwkv_s, rwkv_shift, swa_k, swa_v, gdn_s, gdn_conv, layers, biases, *, prompt, tv, alpha,
               tm, n_chunks, bb, bb_rwkv, c, nsub):
    b, t, d = x.shape
    depth = len(layers)
    xf = x.reshape(b * t, d)
    s_r = sh_r = s_g = cb = None
    kc, vc = ([], []) if prompt else (None, None)
    if prompt:
        rwkv_shift = rwkv_shift.reshape(depth, b, 1, R_PROJ)
    else:
        rwkv_s = rwkv_s.transpose(0, 2, 3, 4, 1)
        gdn_s = gdn_s.transpose(0, 2, 3, 4, 1)
        gdn_conv = gdn_conv.transpose(0, 2, 1, 3)
    for l, lp in enumerate(layers):
        x1, pr, ps, pg = _ffn_proj(xf, *lp["ffn1"], l=l, alpha=alpha, tm=tm, n_chunks=n_chunks)
        pr = pr.reshape(b, t, R_PROJ)
        ps = ps.reshape(b, t, S_PROJ)
        pg = pg.reshape(b, t, G_PROJ_PAD)
        if prompt:
            y_r, s_r, sh_r = _rwkv_mixer(pr, rwkv_shift, rwkv_s, l, s_r, sh_r, lp["rwkv"], bb=bb_rwkv, c=c, tv=tv,
                                         nsub=nsub)
        else:
            y_tm, s_r, sh_r = _rwkv_sample(pr[:, :tv].transpose(1, 0, 2), rwkv_shift, rwkv_s, l, s_r, sh_r,
                                           lp["rwkv_sample"], t_pad=t)
            y_r = y_tm.transpose(1, 0, 2)
        if prompt:
            y_s = _swa_prompt(ps, lp["sinks"], biases[0])
            kc.append(ps[:, t - WINDOW:, S_WIDTH:S_WIDTH + S_KV_WIDTH])
            vc.append(ps[:, t - WINDOW:, S_WIDTH + S_KV_WIDTH:])
        else:
            y_s, kc, vc = _swa_sample(ps, swa_k, swa_v, l, kc, vc, lp["sinks"], biases[1], biases[2], bb=2 * bb, tv=tv)
        if prompt:
            y_g, s_g, cb = _gdn_mixer(pg, gdn_conv, gdn_s, l, s_g, cb, lp["gdn"], bb=bb, c=c, tv=tv, nsub=nsub)
        else:
            y_tm, s_g, cb = _gdn_sample(pg[:, :tv].transpose(1, 0, 2), gdn_conv, gdn_s, l, s_g, cb,
                                        lp["gdn_sample"], t_pad=t)
            y_g = y_tm.transpose(1, 0, 2)
        xf = _out_ffn(x1, y_r.reshape(b * t, R_WIDTH), y_s.reshape(b * t, S_WIDTH), y_g.reshape(b * t, G_WIDTH),
                      *lp["out"], l=l, alpha=alpha, tm=tm, n_chunks=n_chunks)
    if prompt:
        kc, vc = jnp.stack(kc, axis=0), jnp.stack(vc, axis=0)
    else:
        s_r = s_r.transpose(0, 4, 1, 2, 3)
        kc, vc = kc.transpose(0, 1, 3, 2), vc.transpose(0, 1, 3, 2)
        s_g = s_g.transpose(0, 4, 1, 2, 3)
        cb = cb.transpose(0, 2, 1, 3)
    cache_shape = (depth, b, WINDOW, S_KV_HEADS, HEAD_DIM)
    return xf.reshape(b, t, d), [s_r, sh_r.reshape(depth, b, R_PROJ), kc.reshape(cache_shape),
                                 vc.reshape(cache_shape), s_g, cb]


def kernel(x_prompt, x_sample, state_rwkv, state_rwkv_shift, cache_swa_k, cache_swa_v, state_gdn, state_gdn_conv, ffn1_w_in, ffn1_w_out, ln1_g, ln1_b, w_in, rwkv_mu, rwkv_w0, rwkv_w_up, rwkv_a0, rwkv_a_up, rwkv_g_up, rwkv_k_k, rwkv_k_a, rwkv_r_k, rwkv_gn_g, rwkv_gn_b, swa_sinks, rel_table, gdn_conv_w, gdn_a_log, gdn_dt_bias, gdn_norm_g, w_out, ln2_g, ln2_b, ffn2_w_in, ffn2_w_out, ln3_g, ln3_b):
    depth = ffn1_w_in.shape[0]
    alpha = (2 * depth) ** 0.25
    ffn1_w_in, ffn1_w_out, ffn2_w_in, ffn2_w_out, w_out = (
        a.astype(BF16) for a in (ffn1_w_in, ffn1_w_out, ffn2_w_in, ffn2_w_out, w_out))
    w_in = jnp.concatenate([w_in, jnp.zeros(w_in.shape[:2] + (PROJ_PAD - w_in.shape[2],), F32)],
                           axis=2).astype(BF16)
    layers = [_prep_layer(l, ffn1_w_in, ffn1_w_out, ln1_g, ln1_b, w_in, rwkv_mu, rwkv_w0, rwkv_w_up, rwkv_a0,
                          rwkv_a_up, rwkv_g_up, rwkv_k_k, rwkv_k_a, rwkv_r_k, rwkv_gn_g, rwkv_gn_b, swa_sinks,
                          gdn_conv_w, gdn_a_log, gdn_dt_bias, gdn_norm_g, w_out, ln2_g, ln2_b,
                          ffn2_w_in, ffn2_w_out, ln3_g, ln3_b) for l in range(depth)]
    bp, tp_len, d = x_prompt.shape
    bs, ts, _ = x_sample.shape
    ts_pad = -(-ts // SUBLANES) * SUBLANES

    assert WINDOW == ATTN_BLOCK
    qi = np.arange(ATTN_BLOCK)[:, None]
    kc = np.arange(ATTN_BLOCK)[None, :]
    dist_p = np.where(kc > qi, ATTN_BLOCK + qi - kc, qi - kc)
    bias_p = jnp.stack([_rel_bias(rel_table, dist_p, kc <= qi), _rel_bias(rel_table, dist_p)], axis=0)
    ti = (np.arange(S_GROUP * ts_pad) % ts_pad)[:, None]
    wc = cache_swa_k.shape[2]
    bias_c = _rel_bias(rel_table, wc + ti - np.arange(wc)[None, :])
    bias_n = _rel_bias(rel_table, ti - np.arange(ts_pad)[None, :])
    regroup = lambda a: jnp.stack([jnp.concatenate([a[kv * S_GROUP + g, g * ts_pad:(g + 1) * ts_pad]
                                                    for g in range(S_GROUP)], axis=0)
                                   for kv in range(S_KV_HEADS)], axis=0)
    biases = (bias_p, regroup(bias_c), regroup(bias_n))

    zeros = lambda *s: jnp.zeros((depth, bp) + s, F32)
    y_prompt, p_states = _run_trunk(
        x_prompt, zeros(N_HEADS, HEAD_DIM, HEAD_DIM), zeros(R_PROJ), None, None,
        zeros(N_HEADS, HEAD_DIM, HEAD_DIM), zeros(G_CONV - 1, G_QKV), layers, biases,
        prompt=True, tv=64, alpha=alpha, tm=512, n_chunks=2, bb=bp, bb_rwkv=bp, c=64, nsub=2)

    xs = jnp.concatenate([x_sample, jnp.zeros((bs, ts_pad - ts, d), F32)], axis=1)
    ck = cache_swa_k.reshape(depth, bs, wc, S_KV_WIDTH).transpose(0, 1, 3, 2)
    cv = cache_swa_v.reshape(depth, bs, wc, S_KV_WIDTH).transpose(0, 1, 3, 2)
    y_sample, s_states = _run_trunk(
        xs, state_rwkv, state_rwkv_shift, ck, cv, state_gdn, state_gdn_conv, layers, biases,
        prompt=False, tv=ts, alpha=alpha, tm=512, n_chunks=2, bb=8, bb_rwkv=16, c=ts_pad, nsub=1)
    return (y_prompt, y_sample[:, :ts]) + tuple(p_states) + tuple(s_states)
```
